```python
import jax, jax.numpy as jnp
from jax import lax
import numpy as np

D_MODEL = 4096
BATCH = 8
SEQ = 2048
DEPTH = 2

HEAD_DIM = 128
BLOCK = 128
NEG = -1e30
ATTN_SCALE = HEAD_DIM ** -0.5

A_PAIRS = ((128, 1), (512, 4), (2048, 16))
A_HEADS_PER_PAIR = 4
A_HEADS = A_HEADS_PER_PAIR * len(A_PAIRS)

B_HEADS = 8
B_KV_GROUPS = 2
B_GROUP_SIZE = B_HEADS // B_KV_GROUPS
B_CMP_LEN = 32
B_CMP_STRIDE = 16
B_SEL_LEN = 64
B_SEL_RATIO = B_SEL_LEN // B_CMP_STRIDE
B_SEL_OVERLAP = (1.0, 2.0, 2.0, 2.0, 1.0)
B_N_SEL = 8
B_WINDOW = 512
B_CMP_HIDDEN = 512
B_SEL_QBLOCK = 64

C_HEADS = 8

D_HEADS = 8
D_IDX_HEADS = 8
D_IDX_DIM = 64
D_TOPK = 256

D_FF = 8192
N_BRANCH = 4
ALPHA = (2 * DEPTH) ** 0.25
BETA = (8 * DEPTH) ** -0.25
LN_EPS = 1e-5

N_ALIBI = A_HEADS + B_HEADS + D_HEADS
A_SLOPE_IDX = (0, 1, 2, 3, 12, 13, 14, 15, 24, 25, 26, 27)
B_SLOPE_IDX = (4, 5, 6, 7, 8, 9, 10, 11)
D_SLOPE_IDX = (16, 17, 18, 19, 20, 21, 22, 23)

IN_SIZES = (
    A_HEADS * HEAD_DIM, A_HEADS * HEAD_DIM, A_HEADS * HEAD_DIM,
    B_HEADS * HEAD_DIM, 3 * 2 * B_KV_GROUPS * HEAD_DIM, 3 * B_HEADS,
    C_HEADS * HEAD_DIM, C_HEADS * HEAD_DIM, C_HEADS * HEAD_DIM, C_HEADS,
    D_HEADS * HEAD_DIM, HEAD_DIM, HEAD_DIM,
    D_IDX_HEADS * D_IDX_DIM, D_IDX_DIM, D_IDX_HEADS,
    N_BRANCH * D_MODEL,
)
D_IN = sum(IN_SIZES)
BRANCH_SIZES = (A_HEADS_PER_PAIR * HEAD_DIM, B_HEADS * HEAD_DIM, C_HEADS * HEAD_DIM, D_HEADS * HEAD_DIM)
D_BRANCH = sum(BRANCH_SIZES)

kernel_name = 'hybrid_gated_sparse_attention_trunk'


def _split(t, sizes, axis):
    return jnp.split(t, np.cumsum(sizes)[:-1].tolist(), axis=axis)


def alibi_slopes():
    return jnp.exp2(-8.0 * jnp.arange(1, N_ALIBI + 1, dtype=jnp.float32) / N_ALIBI)


def layer_norm(x, g, b):
    xf = x.astype(jnp.float32)
    mu = jnp.mean(xf, axis=-1, keepdims=True)
    var = jnp.mean(jnp.square(xf - mu), axis=-1, keepdims=True)
    y = (xf - mu) * lax.rsqrt(var + LN_EPS)
    return (y * g.astype(jnp.float32) + b.astype(jnp.float32)).astype(x.dtype)


def swiglu(x, w_gate, w_up, w_down):
    return (jax.nn.silu(x @ w_gate) * (x @ w_up)) @ w_down


def masked_softmax(s, mask):
    s = jnp.where(mask, s.astype(jnp.float32), NEG)
    m = jnp.max(s, axis=-1, keepdims=True)
    e = jnp.where(mask, jnp.exp(s - m), 0.0)
    den = jnp.maximum(jnp.sum(e, axis=-1, keepdims=True), 1e-30)
    return e / den, (m + jnp.log(den))[..., 0]


def banded_attention(q, k, v, window, n_prev, slopes, step):
    B_, N, G, R, Dh = q.shape
    nb = N // BLOCK
    kw = (n_prev + 1) * BLOCK
    qb = q.reshape(B_, nb, BLOCK, G, R, Dh)

    def bands(t):
        tp = jnp.pad(t, ((0, 0), (n_prev * BLOCK, 0), (0, 0), (0, 0)))
        tp = tp.reshape(B_, nb + n_prev, BLOCK, G, Dh)
        return jnp.concatenate([tp[:, i:i + nb] for i in range(n_prev + 1)], axis=2)

    kb, vb = bands(k), bands(v)
    dist = jnp.arange(BLOCK)[:, None] + n_prev * BLOCK - jnp.arange(kw)[None, :]
    key_abs = jnp.arange(nb)[:, None] * BLOCK - n_prev * BLOCK + jnp.arange(kw)[None, :]
    mask = ((dist >= 0) & (dist <= window))[None] & (key_abs >= 0)[:, None, :]
    s = jnp.einsum('bnqgrd,bnkgd->bngrqk', qb, kb).astype(jnp.float32) * ATTN_SCALE
    s = s - slopes[None, None, :, :, None, None] * (step * dist).astype(jnp.float32)
    p, lse = masked_softmax(s, mask[None, :, None, None])
    o = jnp.einsum('bngrqk,bnkgd->bnqgrd', p.astype(v.dtype), vb).reshape(B_, N, G, R, Dh)
    lse = jnp.moveaxis(lse, -1, 2).reshape(B_, N, G, R)
    return o, lse


def dilated_attention(q, k, v, slopes):
    B_, L = q.shape[:2]
    outs, lses = [], []
    for g, (window, dil) in enumerate(A_PAIRS):
        hs = slice(g * A_HEADS_PER_PAIR, (g + 1) * A_HEADS_PER_PAIR)
        n = L // dil
        n_pad = -(-n // BLOCK) * BLOCK

        def sub(t):
            t = t[:, :, hs].reshape(B_, n, dil * A_HEADS_PER_PAIR, HEAD_DIM)
            return jnp.pad(t, ((0, 0), (0, n_pad - n), (0, 0), (0, 0)))

        steps = window // dil
        o, lse = banded_attention(sub(q)[:, :, :, None], sub(k), sub(v), steps, -(-steps // BLOCK),
                                  jnp.tile(slopes[hs], dil)[:, None], dil)
        outs.append(o[:, :n].reshape(B_, L, A_HEADS_PER_PAIR, HEAD_DIM))
        lses.append(lse[:, :n].reshape(B_, L, A_HEADS_PER_PAIR))
    w = jax.nn.softmax(jnp.stack(lses), axis=0)
    o = jnp.sum(w[..., None].astype(q.dtype) * jnp.stack(outs), axis=0)
    return o.reshape(B_, L, A_HEADS_PER_PAIR * HEAD_DIM)


def native_sparse_attention(q, kv, gate_logits, cmp_w1, cmp_w2, cmp_pos, slopes):
    B_, L = q.shape[:2]
    G, R = B_KV_GROUPS, B_GROUP_SIZE
    q = q.reshape(B_, L, G, R, HEAD_DIM)
    kv = kv.reshape(B_, L, 3, 2, G, HEAD_DIM)
    slopes = slopes.reshape(G, R)
    t_pos = jnp.arange(L)

    ratio = B_CMP_LEN // B_CMP_STRIDE
    n_chunk = L // B_CMP_STRIDE
    n_cmp = n_chunk - ratio + 1

    def compress(t, j):
        c = t.reshape(B_, n_chunk, B_CMP_STRIDE, G, HEAD_DIM)
        blk = jnp.concatenate([c[:, i:i + n_cmp] for i in range(ratio)], axis=2) + cmp_pos[j][:, None, :]
        blk = blk.transpose(0, 1, 3, 2, 4).reshape(B_, n_cmp, G, B_CMP_LEN * HEAD_DIM)
        return jax.nn.gelu(blk @ cmp_w1[j]) @ cmp_w2[j]

    k_cmp = compress(kv[:, :, 0, 0], 0)
    v_cmp = compress(kv[:, :, 0, 1], 1)
    dist_c = t_pos[:, None] - (jnp.arange(n_cmp) * B_CMP_STRIDE + B_CMP_LEN - 1)[None, :]
    s = jnp.einsum('blgrd,bngd->bgrln', q, k_cmp).astype(jnp.float32) * ATTN_SCALE
    s = s - slopes[:, :, None, None] * dist_c.astype(jnp.float32)
    p_cmp, _ = masked_softmax(s, dist_c >= 0)
    o_cmp = jnp.einsum('bgrln,bngd->blgrd', p_cmp.astype(q.dtype), v_cmp)

    imp = p_cmp.sum(axis=2)
    n_slc = L // B_SEL_LEN
    imp = jnp.pad(imp, ((0, 0), (0, 0), (0, 0), (1, B_SEL_RATIO * n_slc - n_cmp)))
    p_slc = sum(w * imp[..., o:o + B_SEL_RATIO * (n_slc - 1) + 1:B_SEL_RATIO]
                for o, w in enumerate(B_SEL_OVERLAP))
    blk_j = jnp.arange(n_slc)[None, :]
    cur = (t_pos // B_SEL_LEN)[:, None]
    forced = (blk_j == 0) | (blk_j == cur) | (blk_j == cur - 1)
    score = jnp.where(forced, 1e9, jnp.where(blk_j <= cur, p_slc, -1e9))
    n_sel = min(B_N_SEL, n_slc)
    _, sel = lax.top_k(score, n_sel)

    ks_b = kv[:, :, 1, 0].transpose(0, 2, 1, 3).reshape(B_, G, n_slc, B_SEL_LEN, HEAD_DIM)
    vs_b = kv[:, :, 1, 1].transpose(0, 2, 1, 3).reshape(B_, G, n_slc, B_SEL_LEN, HEAD_DIM)
    gather = jax.vmap(jax.vmap(lambda kk, ii: kk[ii]))
    n_keys = n_sel * B_SEL_LEN

    def sel_block(i):
        qb = lax.dynamic_slice_in_dim(q, i * B_SEL_QBLOCK, B_SEL_QBLOCK, axis=1)
        ib = lax.dynamic_slice_in_dim(sel, i * B_SEL_QBLOCK, B_SEL_QBLOCK, axis=2)
        kg = gather(ks_b, ib)
        vg = gather(vs_b, ib).reshape(B_, G, B_SEL_QBLOCK, n_keys, HEAD_DIM)
        t = i * B_SEL_QBLOCK + jnp.arange(B_SEL_QBLOCK)
        kpos = ib[..., None] * B_SEL_LEN + jnp.arange(B_SEL_LEN)
        dist = t[None, None, :, None, None] - kpos
        s = jnp.einsum('bqgrd,bgqnkd->bgrqnk', qb, kg).astype(jnp.float32) * ATTN_SCALE
        s = s - slopes[None, :, :, None, None, None] * dist[:, :, None].astype(jnp.float32)
        s = s.reshape(B_, G, R, B_SEL_QBLOCK, n_keys)
        mask = (dist >= 0)[:, :, None].reshape(B_, G, 1, B_SEL_QBLOCK, n_keys)
        p, _ = masked_softmax(s, mask)
        return jnp.einsum('bgrqk,bgqkd->bqgrd', p.astype(vg.dtype), vg)

    o_slc = lax.map(sel_block, jnp.arange(L // B_SEL_QBLOCK))
    o_slc = jnp.moveaxis(o_slc, 0, 1).reshape(B_, L, G, R, HEAD_DIM)

    w_steps = B_WINDOW - 1
    o_win, _ = banded_attention(q, kv[:, :, 2, 0], kv[:, :, 2, 1], w_steps, -(-w_steps // BLOCK), slopes, 1)

    g = jax.nn.sigmoid(gate_logits.astype(jnp.float32)).astype(q.dtype).reshape(B_, L, 3, G, R, 1)
    o = g[:, :, 0] * o_cmp + g[:, :, 1] * o_slc + g[:, :, 2] * o_win
    return o.reshape(B_, L, B_HEADS * HEAD_DIM)


def forgetting_attention(q, k, v, f_logit):
    B_, L = q.shape[:2]
    c = jnp.cumsum(jax.nn.log_sigmoid(f_logit.astype(jnp.float32)), axis=1).transpose(0, 2, 1)
    k_pos = jnp.arange(L)

    def block(i):
        qb = lax.dynamic_slice_in_dim(q, i * BLOCK, BLOCK, axis=1)
        cq = lax.dynamic_slice_in_dim(c, i * BLOCK, BLOCK, axis=2)
        s = jnp.einsum('bqhd,bkhd->bhqk', qb, k).astype(jnp.float32) * ATTN_SCALE
        s = s + cq[..., None] - c[:, :, None, :]
        mask = (i * BLOCK + jnp.arange(BLOCK))[:, None] >= k_pos[None, :]
        p, _ = masked_softmax(s, mask)
        return jnp.einsum('bhqk,bkhd->bqhd', p.astype(v.dtype), v)

    o = lax.map(block, jnp.arange(L // BLOCK))
    return jnp.moveaxis(o, 0, 1).reshape(B_, L, -1)


def indexed_sparse_attention(q, k, v, iq, ik, iw, slopes):
    B_, L = q.shape[:2]
    n_top = min(D_TOPK, L // 4)
    k_pos = jnp.arange(L)
    gather = jax.vmap(lambda kk, ii: kk[ii])

    def block(i):
        t = i * BLOCK + jnp.arange(BLOCK)
        qb = lax.dynamic_slice_in_dim(q, i * BLOCK, BLOCK, axis=1)
        iqb = lax.dynamic_slice_in_dim(iq, i * BLOCK, BLOCK, axis=1)
        iwb = lax.dynamic_slice_in_dim(iw, i * BLOCK, BLOCK, axis=1)
        rel = jax.nn.relu(jnp.einsum('bqhd,bkd->bqhk', iqb, ik).astype(jnp.float32))
        score = jnp.einsum('bqh,bqhk->bqk', iwb.astype(jnp.float32), rel)
        score = jnp.where(t[:, None] >= k_pos[None, :], score, NEG)
        _, idx = lax.top_k(score, n_top)
        kg, vg = gather(k, idx), gather(v, idx)
        dist = t[None, :, None] - idx
        s = jnp.einsum('bqhd,bqkd->bqhk', qb, kg).astype(jnp.float32) * ATTN_SCALE
        s = s - slopes[:, None] * dist[:, :, None, :].astype(jnp.float32)
        p, _ = masked_softmax(s, (dist >= 0)[:, :, None, :])
        return jnp.einsum('bqhk,bqkd->bqhd', p.astype(vg.dtype), vg)

    o = lax.map(block, jnp.arange(L // BLOCK))
    return jnp.moveaxis(o, 0, 1).reshape(B_, L, -1)


def hybrid_mixer(x, w_in, b_forget, b_gate, cmp_w1, cmp_w2, cmp_pos, w_branch, w_out):
    B_, L, _ = x.shape

    def heads(t, h):
        return t.reshape(B_, L, h, HEAD_DIM)

    (a_q, a_k, a_v, b_q, b_kv, b_g, c_q, c_k, c_v, c_f,
     d_q, d_k, d_v, d_iq, d_ik, d_iw, g) = _split(x @ w_in, IN_SIZES, -1)
    slopes = alibi_slopes()
    o_a = dilated_attention(heads(a_q, A_HEADS), heads(a_k, A_HEADS), heads(a_v, A_HEADS),
                            slopes[np.array(A_SLOPE_IDX)])
    o_b = native_sparse_attention(b_q, b_kv, b_g, cmp_w1, cmp_w2, cmp_pos, slopes[np.array(B_SLOPE_IDX)])
    o_c = forgetting_attention(heads(c_q, C_HEADS), heads(c_k, C_HEADS), heads(c_v, C_HEADS), c_f + b_forget)
    o_d = indexed_sparse_attention(heads(d_q, D_HEADS), d_k, d_v,
                                   d_iq.reshape(B_, L, D_IDX_HEADS, D_IDX_DIM), d_ik, d_iw,
                                   slopes[np.array(D_SLOPE_IDX)])
    gates = jax.nn.sigmoid((g + b_gate).astype(jnp.float32)).astype(x.dtype).reshape(B_, L, N_BRANCH, D_MODEL)
    w_a, w_b, w_c, w_d = _split(w_branch, BRANCH_SIZES, 0)
    merged = (gates[:, :, 0] * (o_a @ w_a) + gates[:, :, 1] * (o_b @ w_b)
              + gates[:, :, 2] * (o_c @ w_c) + gates[:, :, 3] * (o_d @ w_d))
    return merged @ w_out


def setup_inputs(seed: int = 0) -> dict:
    key = jax.random.key(seed)
    ks = jax.random.split(key, 18)

    def nrm(k, shape, scale):
        return jax.random.normal(k, shape, jnp.float32) * scale

    branch_scale = jnp.asarray(np.repeat(np.array([s ** -0.5 for s in BRANCH_SIZES], np.float32),
                                         BRANCH_SIZES))[:, None] * BETA
    return {
        'x': nrm(ks[0], (BATCH, SEQ, D_MODEL), 1.0),
        'ln_g': 1.0 + nrm(ks[1], (DEPTH, 3, D_MODEL), 0.05),
        'ln_b': nrm(ks[2], (DEPTH, 3, D_MODEL), 0.02),
        'ffn1_w_gate': nrm(ks[3], (DEPTH, D_MODEL, D_FF), D_MODEL ** -0.5),
        'ffn1_w_up': nrm(ks[4], (DEPTH, D_MODEL, D_FF), D_MODEL ** -0.5),
        'ffn1_w_down': nrm(ks[5], (DEPTH, D_FF, D_MODEL), BETA * D_FF ** -0.5),
        'w_in': nrm(ks[6], (DEPTH, D_MODEL, D_IN), D_MODEL ** -0.5),
        'b_forget': jax.random.uniform(ks[7], (DEPTH, C_HEADS), jnp.float32, 1.0, 6.0),
        'b_gate': nrm(ks[8], (DEPTH, N_BRANCH * D_MODEL), 0.02),
        'cmp_w1': nrm(ks[9], (DEPTH, 2, B_CMP_LEN * HEAD_DIM, B_CMP_HIDDEN), (B_CMP_LEN * HEAD_DIM) ** -0.5),
        'cmp_w2': nrm(ks[10], (DEPTH, 2, B_CMP_HIDDEN, HEAD_DIM), (2.0 / B_CMP_HIDDEN) ** 0.5),
        'cmp_pos': nrm(ks[11], (DEPTH, 2, B_CMP_LEN, HEAD_DIM), 0.1),
        'w_branch': nrm(ks[12], (DEPTH, D_BRANCH, D_MODEL), 1.0) * branch_scale,
        'w_out': nrm(ks[13], (DEPTH, D_MODEL, D_MODEL), BETA * D_MODEL ** -0.5),
        'ffn2_w_gate': nrm(ks[14], (DEPTH, D_MODEL, D_FF), D_MODEL ** -0.5),
        'ffn2_w_up': nrm(ks[15], (DEPTH, D_MODEL, D_FF), D_MODEL ** -0.5),
        'ffn2_w_down': nrm(ks[16], (DEPTH, D_FF, D_MODEL), BETA * D_FF ** -0.5),
    }


def reference(x, ln_g, ln_b, ffn1_w_gate, ffn1_w_up, ffn1_w_down, w_in, b_forget, b_gate,
              cmp_w1, cmp_w2, cmp_pos, w_branch, w_out, ffn2_w_gate, ffn2_w_up, ffn2_w_down):
    for l in range(DEPTH):
        x = layer_norm(ALPHA * x + 0.5 * swiglu(x, ffn1_w_gate[l], ffn1_w_up[l], ffn1_w_down[l]),
                       ln_g[l, 0], ln_b[l, 0])
        x = layer_norm(ALPHA * x + hybrid_mixer(x, w_in[l], b_forget[l], b_gate[l], cmp_w1[l], cmp_w2[l],
                                                cmp_pos[l], w_branch[l], w_out[l]),
                       ln_g[l, 1], ln_b[l, 1])
        x = layer_norm(ALPHA * x + 0.5 * swiglu(x, ffn2_w_gate[l], ffn2_w_up[l], ffn2_w_down[l]),
                       ln_g[l, 2], ln_b[l, 2])
    return x
```

```python
import functools

import numpy as np
import jax
import jax.numpy as jnp
from jax import lax
from jax.experimental import pallas as pl
from jax.experimental.pallas import tpu as pltpu

F32 = jnp.float32
BF16 = jnp.bfloat16

HEAD_DIM = 128
BLOCK = 128
LANES = 128
NEG = -1e30
ATTN_SCALE = HEAD_DIM ** -0.5
DEPTH = 2
ALPHA = (2 * DEPTH) ** 0.25
LN_EPS = 1e-5

A_PAIRS = ((128, 1), (512, 4), (2048, 16))
A_HEADS_PER_PAIR = 4
A_HEADS = A_HEADS_PER_PAIR * len(A_PAIRS)
B_HEADS = 8
B_KV_GROUPS = 2
B_GROUP_SIZE = B_HEADS // B_KV_GROUPS
B_CMP_LEN = 32
B_CMP_STRIDE = 16
B_SEL_LEN = 64
B_SEL_RATIO = B_SEL_LEN // B_CMP_STRIDE
B_N_SEL = 8
B_WINDOW = 512
B_CMP_HIDDEN = 512
C_HEADS = 8
D_HEADS = 8
D_IDX_HEADS = 8
D_IDX_DIM = 64
D_TOPK = 256
N_BRANCH = 4
N_ALIBI = A_HEADS + B_HEADS + D_HEADS
A_SLOPE_IDX = (0, 1, 2, 3, 12, 13, 14, 15, 24, 25, 26, 27)
B_SLOPE_IDX = (4, 5, 6, 7, 8, 9, 10, 11)
D_SLOPE_IDX = (16, 17, 18, 19, 20, 21, 22, 23)

CB_A = 0
CB_BQ = 36
CB_BKV = 44
CB_CQ, CB_CK, CB_CV = 56, 64, 72
CB_DQ = 80
CB_DIQ = 88
CB_DK, CB_DV = 92, 93
CB_MISC = 94
N_CB = 96
MISC_BG, MISC_CF, MISC_IK, MISC_IW = 0, 24, 32, 96

VMEM_LIMIT = 56 * 1024 * 1024


def _cparams(sem):
    return pltpu.CompilerParams(dimension_semantics=sem, vmem_limit_bytes=VMEM_LIMIT)


def _tile(dim, pref):
    return pref if dim % pref == 0 else dim


def _masked_softmax(s, mask):
    s = jnp.where(mask, s, NEG)
    m = jnp.max(s, axis=-1, keepdims=True)
    e = jnp.where(mask, jnp.exp(s - m), 0.0)
    den = jnp.maximum(jnp.sum(e, axis=-1, keepdims=True), 1e-30)
    return e / den, m + jnp.log(den)


def _dot_nt(a, b):
    return lax.dot_general(a, b, (((1,), (1,)), ((), ())), preferred_element_type=F32)


def _mm_kernel(x_ref, w_ref, o_ref, acc_ref):
    k = pl.program_id(2)

    @pl.when(k == 0)
    def _():
        acc_ref[...] = jnp.zeros_like(acc_ref)

    acc_ref[...] += jnp.dot(x_ref[...], w_ref[...], preferred_element_type=F32)

    @pl.when(k == pl.num_programs(2) - 1)
    def _():
        o_ref[...] = acc_ref[...].astype(o_ref.dtype)


def _matmul(x, w, out_dtype):
    M, K = x.shape
    N = w.shape[1]
    tm, tn, tk = _tile(M, 1024), _tile(N, 1024), _tile(K, 1024)
    return pl.pallas_call(
        _mm_kernel,
        grid=(M // tm, N // tn, K // tk),
        in_specs=[pl.BlockSpec((tm, tk), lambda i, j, k: (i, k)),
                  pl.BlockSpec((tk, tn), lambda i, j, k: (k, j))],
        out_specs=pl.BlockSpec((tm, tn), lambda i, j, k: (i, j)),
        out_shape=jax.ShapeDtypeStruct((M, N), out_dtype),
        scratch_shapes=[pltpu.VMEM((tm, tn), F32)],
        compiler_params=_cparams(("parallel", "parallel", "arbitrary")),
    )(x, w)


def _ffn_up_kernel(x_ref, wg_ref, wu_ref, o_ref, accg_ref, accu_ref):
    k = pl.program_id(2)

    @pl.when(k == 0)
    def _():
        accg_ref[...] = jnp.zeros_like(accg_ref)
        accu_ref[...] = jnp.zeros_like(accu_ref)

    x = x_ref[...]
    accg_ref[...] += jnp.dot(x, wg_ref[...], preferred_element_type=F32)
    accu_ref[...] += jnp.dot(x, wu_ref[...], preferred_element_type=F32)

    @pl.when(k == pl.num_programs(2) - 1)
    def _():
        g = accg_ref[...]
        o_ref[...] = (g * jax.nn.sigmoid(g) * accu_ref[...]).astype(o_ref.dtype)


def _ffn_up(x, wg, wu):
    M, K = x.shape
    N = wg.shape[1]
    tm, tn, tk = _tile(M, 1024), _tile(N, 1024), _tile(K, 1024)
    return pl.pallas_call(
        _ffn_up_kernel,
        grid=(M // tm, N // tn, K // tk),
        in_specs=[pl.BlockSpec((tm, tk), lambda i, j, k: (i, k)),
                  pl.BlockSpec((tk, tn), lambda i, j, k: (k, j)),
                  pl.BlockSpec((tk, tn), lambda i, j, k: (k, j))],
        out_specs=pl.BlockSpec((tm, tn), lambda i, j, k: (i, j)),
        out_shape=jax.ShapeDtypeStruct((M, N), BF16),
        scratch_shapes=[pltpu.VMEM((tm, tn), F32), pltpu.VMEM((tm, tn), F32)],
        compiler_params=_cparams(("parallel", "parallel", "arbitrary")),
    )(x, wg, wu)


def _add_ln_kernel(x_ref, y_ref, g_ref, b_ref, o_ref, ob_ref, *, cy):
    z = ALPHA * x_ref[...] + cy * y_ref[...]
    mu = jnp.mean(z, axis=-1, keepdims=True)
    zc = z - mu
    var = jnp.mean(zc * zc, axis=-1, keepdims=True)
    out = zc * lax.rsqrt(var + LN_EPS) * g_ref[...] + b_ref[...]
    o_ref[...] = out
    ob_ref[...] = out.astype(BF16)


def _add_ln(x, y, g, b, cy):
    M, D = x.shape
    tm = _tile(M, 256)
    row = pl.BlockSpec((tm, D), lambda i: (i, 0))
    vec = pl.BlockSpec((1, D), lambda i: (0, 0))
    return pl.pallas_call(
        functools.partial(_add_ln_kernel, cy=cy),
        grid=(M // tm,),
        in_specs=[row, row, vec, vec],
        out_specs=[row, row],
        out_shape=[jax.ShapeDtypeStruct((M, D), F32), jax.ShapeDtypeStruct((M, D), BF16)],
        compiler_params=_cparams(("parallel",)),
    )(x, y, g.reshape(1, D), b.reshape(1, D))


def _banded_kernel(slope_ref, q_ref, k_ref, v_ref, *out_refs, window, n_prev, step, nb):
    hh = pl.program_id(1)
    i = pl.program_id(2)
    kwb = min(n_prev + 1, nb)
    kw = kwb * BLOCK
    start = pl.multiple_of(jnp.maximum(i - n_prev, 0) * BLOCK, BLOCK)
    q = q_ref[0].astype(BF16)
    kwin = k_ref[0, pl.ds(start, kw), :].astype(BF16)
    vwin = v_ref[0, pl.ds(start, kw), :].astype(BF16)
    qpos = i * BLOCK + lax.broadcasted_iota(jnp.int32, (BLOCK, kw), 0)
    kpos = start + lax.broadcasted_iota(jnp.int32, (BLOCK, kw), 1)
    dist = qpos - kpos
    mask = (dist >= 0) & (dist <= window)
    s = _dot_nt(q, kwin) * ATTN_SCALE
    s = s - slope_ref[hh] * (step * dist).astype(F32)
    p, lse = _masked_softmax(s, mask)
    out_refs[0][0] = jnp.dot(p.astype(BF16), vwin, preferred_element_type=F32)
    if len(out_refs) > 1:
        out_refs[1][0] = jnp.broadcast_to(lse, (BLOCK, LANES))


def _banded_attention(qa, ka, va, slopes, *, n_heads, q_blk, k_blk, v_blk, window, n_prev, step, with_lse):
    B, N, _ = qa.shape
    nb = N // BLOCK
    qspec = pl.BlockSpec((1, BLOCK, HEAD_DIM), lambda b, h, i: (b, i, q_blk(h)))
    kspec = pl.BlockSpec((1, N, HEAD_DIM), lambda b, h, i: (b, 0, k_blk(h)))
    vspec = pl.BlockSpec((1, N, HEAD_DIM), lambda b, h, i: (b, 0, v_blk(h)))
    ospec = pl.BlockSpec((1, BLOCK, HEAD_DIM), lambda b, h, i: (b, i, h))
    oshape = jax.ShapeDtypeStruct((B, N, n_heads * HEAD_DIM), F32)
    return pl.pallas_call(
        functools.partial(_banded_kernel, window=window, n_prev=n_prev, step=step, nb=nb),
        grid=(B, n_heads, nb),
        in_specs=[pl.BlockSpec(memory_space=pltpu.SMEM), qspec, kspec, vspec],
        out_specs=[ospec, ospec] if with_lse else [ospec],
        out_shape=[oshape, oshape] if with_lse else [oshape],
        compiler_params=_cparams(("parallel", "parallel", "arbitrary")),
    )(slopes, qa, ka, va)


def _a_combine_kernel(o0, o1, o2, l0, l1, l2, out_ref):
    a, b, c = l0[...], l1[...], l2[...]
    m = jnp.maximum(jnp.maximum(a, b), c)
    ea, eb, ec = jnp.exp(a - m), jnp.exp(b - m), jnp.exp(c - m)
    tot = ea + eb + ec
    out = (ea / tot) * o0[...] + (eb / tot) * o1[...] + (ec / tot) * o2[...]
    out_ref[...] = out.astype(out_ref.dtype)


def _a_combine(outs, lses):
    M, C = outs[0].shape
    tm = _tile(M, 512)
    spec = pl.BlockSpec((tm, C), lambda i: (i, 0))
    return pl.pallas_call(
        _a_combine_kernel,
        grid=(M // tm,),
        in_specs=[spec] * 6,
        out_specs=spec,
        out_shape=jax.ShapeDtypeStruct((M, C), BF16),
        compiler_params=_cparams(("parallel",)),
    )(*outs, *lses)


def _dilated_attention(y3, slopes_all):
    B, L, C = y3.shape
    slopes = slopes_all[np.array(A_SLOPE_IDX)]
    outs, lses = [], []
    for g, (window, dil) in enumerate(A_PAIRS):
        n = L // dil
        assert n % BLOCK == 0
        steps = window // dil
        n_prev = -(-steps // BLOCK)
        yv = y3.reshape(B, n, dil * C)
        hpp = A_HEADS_PER_PAIR

        def blk(part, g=g):
            return lambda hh: (hh // hpp) * N_CB + CB_A + part * A_HEADS + g * hpp + hh % hpp

        sl = jnp.tile(slopes[g * hpp:(g + 1) * hpp], dil)
        o, lse = _banded_attention(yv, yv, yv, sl, n_heads=dil * hpp, q_blk=blk(0), k_blk=blk(1), v_blk=blk(2),
                                   window=steps, n_prev=n_prev, step=dil, with_lse=True)
        outs.append(o.reshape(B * L, hpp * HEAD_DIM))
        lses.append(lse.reshape(B * L, hpp * HEAD_DIM))
    return _a_combine(outs, lses)


def _logsig_cumsum_kernel(z_ref, bias_ref, o_ref):
    L = z_ref.shape[1]
    row = lax.broadcasted_iota(jnp.int32, (BLOCK, BLOCK), 0)
    col = lax.broadcasted_iota(jnp.int32, (BLOCK, BLOCK), 1)
    tri = jnp.where(row >= col, 1.0, 0.0).astype(F32)
    carry = jnp.zeros((1, LANES), F32)
    for j in range(L // BLOCK):
        z = z_ref[0, j * BLOCK:(j + 1) * BLOCK, :] + bias_ref[...]
        ls = jnp.minimum(z, 0.0) - jnp.log(1.0 + jnp.exp(-jnp.abs(z)))
        c = jnp.dot(tri, ls, preferred_element_type=F32, precision=lax.Precision.HIGHEST) + carry
        o_ref[0, j * BLOCK:(j + 1) * BLOCK, :] = c
        carry = c[BLOCK - 1:BLOCK, :]


def _logsig_cumsum(z, bias):
    B, L, _ = z.shape
    spec = pl.BlockSpec((1, L, LANES), lambda b: (b, 0, 0))
    return pl.pallas_call(
        _logsig_cumsum_kernel,
        grid=(B,),
        in_specs=[spec, pl.BlockSpec((1, LANES), lambda b: (0, 0))],
        out_specs=spec,
        out_shape=jax.ShapeDtypeStruct((B, L, LANES), F32),
        compiler_params=_cparams(("parallel",)),
    )(z, bias)


def _fox_kernel(q_ref, k_ref, v_ref, cq_ref, ck_ref, o_ref):
    i = pl.program_id(2)
    L = k_ref.shape[1]
    q = q_ref[0].astype(BF16)
    k = k_ref[0].astype(BF16)
    s = _dot_nt(q, k) * ATTN_SCALE
    s = s + cq_ref[0, 0] - ck_ref[0, 0]
    qpos = i * BLOCK + lax.broadcasted_iota(jnp.int32, (BLOCK, L), 0)
    kpos = lax.broadcasted_iota(jnp.int32, (BLOCK, L), 1)
    p, _ = _masked_softmax(s, qpos >= kpos)
    o_ref[0] = jnp.dot(p.astype(BF16), v_ref[0].astype(BF16), preferred_element_type=F32).astype(o_ref.dtype)


def _forgetting_attention(y3, c_col, c_row):
    B, L, _ = y3.shape
    nb = L // BLOCK
    return pl.pallas_call(
        _fox_kernel,
        grid=(B, C_HEADS, nb),
        in_specs=[pl.BlockSpec((1, BLOCK, HEAD_DIM), lambda b, h, i: (b, i, CB_CQ + h)),
                  pl.BlockSpec((1, L, HEAD_DIM), lambda b, h, i: (b, 0, CB_CK + h)),
                  pl.BlockSpec((1, L, HEAD_DIM), lambda b, h, i: (b, 0, CB_CV + h)),
                  pl.BlockSpec((1, 1, BLOCK, 1), lambda b, h, i: (b, h, i, 0)),
                  pl.BlockSpec((1, 1, 1, L), lambda b, h, i: (b, h, 0, 0))],
        out_specs=pl.BlockSpec((1, BLOCK, HEAD_DIM), lambda b, h, i: (b, i, h)),
        out_shape=jax.ShapeDtypeStruct((B, L, C_HEADS * HEAD_DIM), BF16),
        compiler_params=_cparams(("parallel", "parallel", "arbitrary")),
    )(y3, y3, y3, c_col, c_row)


def _order_key(x):
    bits = lax.bitcast_convert_type(x, jnp.int32)
    return bits ^ ((bits >> 31) & jnp.int32(0x7FFFFFFF))


def _kth_largest_key(key, k):
    rows = key.shape[0]

    def count_ge(t):
        return jnp.sum(jnp.where(key >= t, 1.0, 0.0), axis=-1, keepdims=True)

    t0 = jnp.where(count_ge(jnp.zeros((rows, 1), jnp.int32)) >= k,
                   jnp.int32(0), jnp.int32(-2 ** 31)) + jnp.zeros((rows, 1), jnp.int32)

    def body(it, t):
        cand = t | jnp.left_shift(jnp.int32(1), 30 - it)
        return jnp.where(count_ge(cand) >= k, cand, t)

    return lax.fori_loop(0, 31, body, t0)


def _dsa_kernel(slope_ref, iq_ref, ik_ref, iw_ref, q_ref, k_ref, v_ref, o_ref, *, n_top):
    i = pl.program_id(1)
    L = k_ref.shape[1]
    ik = ik_ref[0].astype(BF16)
    iw = iw_ref[0]
    score = jnp.zeros((BLOCK, L), F32)
    for h in range(D_IDX_HEADS):
        iq = iq_ref[0, :, h * D_IDX_DIM:(h + 1) * D_IDX_DIM].astype(BF16)
        rel = jnp.maximum(_dot_nt(iq, ik), 0.0)
        score = score + iw[:, h:h + 1] * rel
    qpos = i * BLOCK + lax.broadcasted_iota(jnp.int32, (BLOCK, L), 0)
    kpos = lax.broadcasted_iota(jnp.int32, (BLOCK, L), 1)
    dist = qpos - kpos
    causal = dist >= 0
    key = _order_key(jnp.where(causal, score, NEG))

    thr = _kth_largest_key(key, float(n_top))
    gt = key > thr
    eq = key == thr
    need = float(n_top) - jnp.sum(jnp.where(gt, 1.0, 0.0), axis=-1, keepdims=True)
    row = lax.broadcasted_iota(jnp.int32, (BLOCK, BLOCK), 0)
    col = lax.broadcasted_iota(jnp.int32, (BLOCK, BLOCK), 1)
    upper = jnp.where(row <= col, 1.0, 0.0).astype(BF16)
    eqf = jnp.where(eq, 1.0, 0.0).astype(BF16)
    carry = jnp.zeros((BLOCK, 1), F32)
    rank_tiles = []
    for j in range(L // BLOCK):
        rank = carry + jnp.dot(eqf[:, j * BLOCK:(j + 1) * BLOCK], upper, preferred_element_type=F32)
        rank_tiles.append(rank)
        carry = rank[:, BLOCK - 1:BLOCK]
    rank = jnp.concatenate(rank_tiles, axis=1)
    mask = (gt | (eq & (rank <= need))) & causal

    k = k_ref[0].astype(BF16)
    v = v_ref[0].astype(BF16)
    distf = dist.astype(F32)
    for h in range(D_HEADS):
        q = q_ref[0, :, h * HEAD_DIM:(h + 1) * HEAD_DIM].astype(BF16)
        s = _dot_nt(q, k) * ATTN_SCALE
        s = s - slope_ref[h] * distf
        p, _ = _masked_softmax(s, mask)
        o_ref[0, :, h * HEAD_DIM:(h + 1) * HEAD_DIM] = jnp.dot(
            p.astype(BF16), v, preferred_element_type=F32).astype(o_ref.dtype)


def _indexed_sparse_attention(y3, ik, iw, slopes_all):
    B, L, _ = y3.shape
    nb = L // BLOCK
    n_top = min(D_TOPK, L // 4)
    slopes = slopes_all[np.array(D_SLOPE_IDX)]
    wide = D_HEADS * HEAD_DIM // LANES
    return pl.pallas_call(
        functools.partial(_dsa_kernel, n_top=n_top),
        grid=(B, nb),
        in_specs=[pl.BlockSpec(memory_space=pltpu.SMEM),
                  pl.BlockSpec((1, BLOCK, D_IDX_HEADS * D_IDX_DIM), lambda b, i: (b, i, CB_DIQ * LANES // (D_IDX_HEADS * D_IDX_DIM))),
                  pl.BlockSpec((1, L, D_IDX_DIM), lambda b, i: (b, 0, 0)),
                  pl.BlockSpec((1, BLOCK, D_IDX_HEADS), lambda b, i: (b, i, 0)),
                  pl.BlockSpec((1, BLOCK, D_HEADS * HEAD_DIM), lambda b, i: (b, i, CB_DQ // wide)),
                  pl.BlockSpec((1, L, HEAD_DIM), lambda b, i: (b, 0, CB_DK)),
                  pl.BlockSpec((1, L, HEAD_DIM), lambda b, i: (b, 0, CB_DV))],
        out_specs=pl.BlockSpec((1, BLOCK, D_HEADS * HEAD_DIM), lambda b, i: (b, i, 0)),
        out_shape=jax.ShapeDtypeStruct((B, L, D_HEADS * HEAD_DIM), BF16),
        compiler_params=_cparams(("parallel", "arbitrary")),
    )(slopes, y3, ik, iw, y3, y3, y3)


def _compress_kernel(x_ref, pos_ref, w1_ref, w2_ref, o_ref):
    x = x_ref[0, 0]
    nxt = pltpu.roll(x, x.shape[0] - 1, 0)
    blk = jnp.concatenate([x, nxt], axis=1) + pos_ref[0]
    h = jnp.dot(blk.astype(BF16), w1_ref[0], preferred_element_type=F32)
    h = jax.nn.gelu(h, approximate=True)
    o_ref[0, 0] = jnp.dot(h.astype(BF16), w2_ref[0], preferred_element_type=F32)


def _compress(xc, pos, w1, w2):
    B, _, n_chunk, cw = xc.shape
    G = B_KV_GROUPS
    return pl.pallas_call(
        _compress_kernel,
        grid=(2 * G, B),
        in_specs=[pl.BlockSpec((1, 1, n_chunk, cw), lambda a, b: (b, a, 0, 0)),
                  pl.BlockSpec((1, 1, 2 * cw), lambda a, b: (a // G, 0, 0)),
                  pl.BlockSpec((1, 2 * cw, B_CMP_HIDDEN), lambda a, b: (a // G, 0, 0)),
                  pl.BlockSpec((1, B_CMP_HIDDEN, HEAD_DIM), lambda a, b: (a // G, 0, 0))],
        out_specs=pl.BlockSpec((1, 1, n_chunk, HEAD_DIM), lambda a, b: (b, a, 0, 0)),
        out_shape=jax.ShapeDtypeStruct((B, 2 * G, n_chunk, HEAD_DIM), F32),
        compiler_params=_cparams(("arbitrary", "arbitrary")),
    )(xc, pos, w1, w2)


def _nsa_cmp_kernel(slope_ref, q_ref, kc_ref, vc_ref, o_ref, sel_ref, *, n_cmp, n_slc):
    g = pl.program_id(1)
    i = pl.program_id(2)
    nc = kc_ref.shape[2]
    kc = kc_ref[0, 0].astype(BF16)
    vc = vc_ref[0, 0].astype(BF16)
    t = i * BLOCK + lax.broadcasted_iota(jnp.int32, (BLOCK, nc), 0)
    n = lax.broadcasted_iota(jnp.int32, (BLOCK, nc), 1)
    dist_c = t - (n * B_CMP_STRIDE + B_CMP_LEN - 1)
    mask = (dist_c >= 0) & (n < n_cmp)
    distf = dist_c.astype(F32)
    imp = jnp.zeros((BLOCK, nc), F32)
    for r in range(B_GROUP_SIZE):
        q = q_ref[0, :, r * HEAD_DIM:(r + 1) * HEAD_DIM].astype(BF16)
        s = _dot_nt(q, kc) * ATTN_SCALE
        s = s - slope_ref[g * B_GROUP_SIZE + r] * distf
        p, _ = _masked_softmax(s, mask)
        o_ref[0, :, r * HEAD_DIM:(r + 1) * HEAD_DIM] = jnp.dot(p.astype(BF16), vc, preferred_element_type=F32)
        imp = imp + p

    nn = lax.broadcasted_iota(jnp.int32, (nc, LANES), 0)
    jj = lax.broadcasted_iota(jnp.int32, (nc, LANES), 1)
    off = nn - B_SEL_RATIO * jj + 1
    w = jnp.where((off == 0) | (off == B_SEL_RATIO), 1.0, jnp.where((off > 0) & (off < B_SEL_RATIO), 2.0, 0.0))
    w = jnp.where((nn < n_cmp) & (jj < n_slc), w, 0.0).astype(F32)
    p_slc = jnp.dot(imp, w, preferred_element_type=F32, precision=lax.Precision.HIGHEST)

    tq = i * BLOCK + lax.broadcasted_iota(jnp.int32, (BLOCK, LANES), 0)
    j = lax.broadcasted_iota(jnp.int32, (BLOCK, LANES), 1)
    cur = tq >> int(np.log2(B_SEL_LEN))
    forced = (j == 0) | (j == cur) | (j == cur - 1)
    score = jnp.where(forced, 1e9, jnp.where(j <= cur, p_slc, -1e9))
    score = jnp.where(j < n_slc, score, -3e38)
    jf = j.astype(F32)
    sel = jnp.zeros((BLOCK, LANES), F32)
    for _ in range(min(B_N_SEL, n_slc)):
        m = jnp.max(score, axis=-1, keepdims=True)
        first = jnp.min(jnp.where(score == m, jf, float(LANES)), axis=-1, keepdims=True)
        hit = jf == first
        sel = jnp.where(hit, 1.0, sel)
        score = jnp.where(hit, -3e38, score)
    sel_ref[0, 0] = sel.astype(sel_ref.dtype)


def _nsa_cmp(y3, cmp_kv, slopes, n_cmp, n_slc):
    B, L, _ = y3.shape
    G, R = B_KV_GROUPS, B_GROUP_SIZE
    nc = cmp_kv.shape[2]
    wide = R * HEAD_DIM // LANES
    return pl.pallas_call(
        functools.partial(_nsa_cmp_kernel, n_cmp=n_cmp, n_slc=n_slc),
        grid=(B, G, L // BLOCK),
        in_specs=[pl.BlockSpec(memory_space=pltpu.SMEM),
                  pl.BlockSpec((1, BLOCK, R * HEAD_DIM), lambda b, g, i: (b, i, CB_BQ // wide + g)),
                  pl.BlockSpec((1, 1, nc, HEAD_DIM), lambda b, g, i: (b, g, 0, 0)),
                  pl.BlockSpec((1, 1, nc, HEAD_DIM), lambda b, g, i: (b, G + g, 0, 0))],
        out_specs=[pl.BlockSpec((1, BLOCK, R * HEAD_DIM), lambda b, g, i: (b, i, g)),
                   pl.BlockSpec((1, 1, BLOCK, LANES), lambda b, g, i: (b, g, i, 0))],
        out_shape=[jax.ShapeDtypeStruct((B, L, B_HEADS * HEAD_DIM), F32),
                   jax.ShapeDtypeStruct((B, G, L, LANES), BF16)],
        compiler_params=_cparams(("parallel", "parallel", "arbitrary")),
    )(slopes, y3, cmp_kv, cmp_kv)


def _nsa_slc_kernel(slope_ref, q_ref, k_ref, v_ref, sel_ref, o_ref):
    g = pl.program_id(1)
    i = pl.program_id(2)
    L = k_ref.shape[1]
    jj = lax.broadcasted_iota(jnp.int32, (LANES, L), 0)
    ss = lax.broadcasted_iota(jnp.int32, (LANES, L), 1)
    expand = jnp.where((ss >> int(np.log2(B_SEL_LEN))) == jj, 1.0, 0.0).astype(BF16)
    picked = jnp.dot(sel_ref[0, 0], expand, preferred_element_type=F32) > 0.5
    qpos = i * BLOCK + lax.broadcasted_iota(jnp.int32, (BLOCK, L), 0)
    kpos = lax.broadcasted_iota(jnp.int32, (BLOCK, L), 1)
    dist = qpos - kpos
    mask = picked & (dist >= 0)
    distf = dist.astype(F32)
    k = k_ref[0].astype(BF16)
    v = v_ref[0].astype(BF16)
    for r in range(B_GROUP_SIZE):
        q = q_ref[0, :, r * HEAD_DIM:(r + 1) * HEAD_DIM].astype(BF16)
        s = _dot_nt(q, k) * ATTN_SCALE
        s = s - slope_ref[g * B_GROUP_SIZE + r] * distf
        p, _ = _masked_softmax(s, mask)
        o_ref[0, :, r * HEAD_DIM:(r + 1) * HEAD_DIM] = jnp.dot(p.astype(BF16), v, preferred_element_type=F32)


def _nsa_slc(y3, sel, slopes):
    B, L, _ = y3.shape
    G, R = B_KV_GROUPS, B_GROUP_SIZE
    wide = R * HEAD_DIM // LANES
    return pl.pallas_call(
        _nsa_slc_kernel,
        grid=(B, G, L // BLOCK),
        in_specs=[pl.BlockSpec(memory_space=pltpu.SMEM),
                  pl.BlockSpec((1, BLOCK, R * HEAD_DIM), lambda b, g, i: (b, i, CB_BQ // wide + g)),
                  pl.BlockSpec((1, L, HEAD_DIM), lambda b, g, i: (b, 0, CB_BKV + 4 + g)),
                  pl.BlockSpec((1, L, HEAD_DIM), lambda b, g, i: (b, 0, CB_BKV + 6 + g)),
                  pl.BlockSpec((1, 1, BLOCK, LANES), lambda b, g, i: (b, g, i, 0))],
        out_specs=pl.BlockSpec((1, BLOCK, R * HEAD_DIM), lambda b, g, i: (b, i, g)),
        out_shape=jax.ShapeDtypeStruct((B, L, B_HEADS * HEAD_DIM), F32),
        compiler_params=_cparams(("parallel", "parallel", "arbitrary")),
    )(slopes, y3, y3, y3, sel)


def _b_gate_kernel(gl_ref, oc_ref, os_ref, ow_ref, out_ref):
    gate = jax.nn.sigmoid(gl_ref[...])
    for h in range(B_HEADS):
        sl = slice(h * HEAD_DIM, (h + 1) * HEAD_DIM)
        out = (gate[:, h:h + 1] * oc_ref[:, sl] + gate[:, B_HEADS + h:B_HEADS + h + 1] * os_ref[:, sl]
               + gate[:, 2 * B_HEADS + h:2 * B_HEADS + h + 1] * ow_ref[:, sl])
        out_ref[:, sl] = out.astype(out_ref.dtype)


def _b_gate(gl, o_cmp, o_slc, o_win):
    M, C = o_cmp.shape
    tm = _tile(M, 512)
    spec = pl.BlockSpec((tm, C), lambda i: (i, 0))
    return pl.pallas_call(
        _b_gate_kernel,
        grid=(M // tm,),
        in_specs=[pl.BlockSpec((tm, gl.shape[1]), lambda i: (i, 0)), spec, spec, spec],
        out_specs=spec,
        out_shape=jax.ShapeDtypeStruct((M, C), BF16),
        compiler_params=_cparams(("parallel",)),
    )(gl, o_cmp, o_slc, o_win)


def _native_sparse_attention(y3, gate_logits, cmp_w1, cmp_w2, cmp_pos, slopes_all):
    B, L, _ = y3.shape
    G = B_KV_GROUPS
    slopes = slopes_all[np.array(B_SLOPE_IDX)]
    n_chunk = L // B_CMP_STRIDE
    n_cmp = n_chunk - B_CMP_LEN // B_CMP_STRIDE + 1
    n_slc = L // B_SEL_LEN
    assert B_CMP_LEN == 2 * B_CMP_STRIDE and n_chunk % 8 == 0 and n_slc <= LANES

    xc = y3[:, :, CB_BKV * LANES:(CB_BKV + 2 * G) * LANES].reshape(B, L, 2 * G, HEAD_DIM)
    xc = xc.transpose(0, 2, 1, 3).reshape(B, 2 * G, n_chunk, B_CMP_STRIDE * HEAD_DIM)
    cmp_kv = _compress(xc, cmp_pos.reshape(2, 1, B_CMP_LEN * HEAD_DIM), cmp_w1.astype(BF16), cmp_w2.astype(BF16))

    o_cmp, sel = _nsa_cmp(y3, cmp_kv, slopes, n_cmp, n_slc)
    o_slc = _nsa_slc(y3, sel, slopes)
    w_steps = B_WINDOW - 1
    (o_win,) = _banded_attention(y3, y3, y3, slopes, n_heads=B_HEADS,
                                 q_blk=lambda hh: CB_BQ + hh,
                                 k_blk=lambda hh: CB_BKV + 8 + hh // B_GROUP_SIZE,
                                 v_blk=lambda hh: CB_BKV + 10 + hh // B_GROUP_SIZE,
                                 window=w_steps, n_prev=-(-w_steps // BLOCK), step=1, with_lse=False)
    C = B_HEADS * HEAD_DIM
    return _b_gate(gate_logits.reshape(B * L, -1), o_cmp.reshape(B * L, C), o_slc.reshape(B * L, C),
                   o_win.reshape(B * L, C))


def _merge_kernel(oa, ob, oc, od, wa, wb, wc, wd, g0, g1, g2, g3, b0, b1, b2, b3, out_ref):
    acc = jax.nn.sigmoid(g0[...] + b0[...]) * jnp.dot(oa[...], wa[...], preferred_element_type=F32)
    acc = acc + jax.nn.sigmoid(g1[...] + b1[...]) * jnp.dot(ob[...], wb[...], preferred_element_type=F32)
    acc = acc + jax.nn.sigmoid(g2[...] + b2[...]) * jnp.dot(oc[...], wc[...], preferred_element_type=F32)
    acc = acc + jax.nn.sigmoid(g3[...] + b3[...]) * jnp.dot(od[...], wd[...], preferred_element_type=F32)
    out_ref[...] = acc.astype(out_ref.dtype)


def _merge(branch_outs, branch_ws, gate_logits, b_gate, D):
    M = gate_logits.shape[0]
    tm, tn = _tile(M, 512), _tile(D, 512)
    nj = D // tn
    o_specs = [pl.BlockSpec((tm, o.shape[1]), lambda i, j: (i, 0)) for o in branch_outs]
    w_specs = [pl.BlockSpec((w.shape[0], tn), lambda i, j: (0, j)) for w in branch_ws]
    g_specs = [pl.BlockSpec((tm, tn), lambda i, j, c=c: (i, c * nj + j)) for c in range(N_BRANCH)]
    b_specs = [pl.BlockSpec((1, tn), lambda i, j, c=c: (0, c * nj + j)) for c in range(N_BRANCH)]
    return pl.pallas_call(
        _merge_kernel,
        grid=(M // tm, nj),
        in_specs=o_specs + w_specs + g_specs + b_specs,
        out_specs=pl.BlockSpec((tm, tn), lambda i, j: (i, j)),
        out_shape=jax.ShapeDtypeStruct((M, D), BF16),
        compiler_params=_cparams(("parallel", "arbitrary")),
    )(*branch_outs, *branch_ws, *([gate_logits] * N_BRANCH), *([b_gate.reshape(1, -1)] * N_BRANCH))


def _attn_weight(w_in):
    D = w_in.shape[0]
    sizes = (A_HEADS * HEAD_DIM,) * 3 + (B_HEADS * HEAD_DIM, 3 * 2 * B_KV_GROUPS * HEAD_DIM, 3 * B_HEADS) + \
            (C_HEADS * HEAD_DIM,) * 3 + (C_HEADS,) + (D_HEADS * HEAD_DIM, HEAD_DIM, HEAD_DIM) + \
            (D_IDX_HEADS * D_IDX_DIM, D_IDX_DIM, D_IDX_HEADS)
    offs = np.concatenate([[0], np.cumsum(sizes)])
    (a_q, a_k, a_v, b_q, b_kv, b_g, c_q, c_k, c_v, c_f, d_q, d_k, d_v, d_iq, d_ik, d_iw) = [
        w_in[:, offs[n]:offs[n + 1]] for n in range(len(sizes))]
    misc = jnp.zeros((D, 2 * LANES), w_in.dtype)
    misc = misc.at[:, MISC_BG:MISC_BG + 3 * B_HEADS].set(b_g)
    misc = misc.at[:, MISC_CF:MISC_CF + C_HEADS].set(c_f)
    misc = misc.at[:, MISC_IK:MISC_IK + D_IDX_DIM].set(d_ik)
    misc = misc.at[:, MISC_IW:MISC_IW + D_IDX_HEADS].set(d_iw)
    w = jnp.concatenate([a_q, a_k, a_v, b_q, b_kv, c_q, c_k, c_v, d_q, d_iq, d_k, d_v, misc], axis=1)
    assert w.shape[1] == N_CB * LANES
    return w.astype(BF16), int(offs[-1])


def _hybrid_mixer(xb, B, L, w_in, b_forget, b_gate, cmp_w1, cmp_w2, cmp_pos, w_branch, w_out):
    M, D = xb.shape
    slopes_all = jnp.exp2(-8.0 * jnp.arange(1, N_ALIBI + 1, dtype=F32) / N_ALIBI)
    w_attn, gate_off = _attn_weight(w_in)
    y = _matmul(xb, w_attn, F32)
    gates = _matmul(xb, w_in[:, gate_off:].astype(BF16), F32)
    y3 = y.reshape(B, L, N_CB * LANES)

    misc = y3[:, :, CB_MISC * LANES:(CB_MISC + 1) * LANES]
    bias = jnp.zeros((1, LANES), F32).at[0, MISC_CF:MISC_CF + C_HEADS].set(b_forget)
    c = _logsig_cumsum(misc, bias)[:, :, MISC_CF:MISC_CF + C_HEADS].transpose(0, 2, 1)

    o_a = _dilated_attention(y3, slopes_all)
    o_b = _native_sparse_attention(y3, misc[:, :, MISC_BG:MISC_BG + 3 * B_HEADS], cmp_w1, cmp_w2, cmp_pos, slopes_all)
    o_c = _forgetting_attention(y3, c[..., None], c[:, :, None, :])
    o_d = _indexed_sparse_attention(y3, misc[:, :, MISC_IK:MISC_IK + D_IDX_DIM],
                                    misc[:, :, MISC_IW:MISC_IW + D_IDX_HEADS], slopes_all)

    sizes = (A_HEADS_PER_PAIR * HEAD_DIM, B_HEADS * HEAD_DIM, C_HEADS * HEAD_DIM, D_HEADS * HEAD_DIM)
    offs = np.concatenate([[0], np.cumsum(sizes)])
    ws = [w_branch[offs[n]:offs[n + 1]].astype(BF16) for n in range(N_BRANCH)]
    outs = [o_a, o_b, o_c.reshape(M, -1), o_d.reshape(M, -1)]
    merged = _merge(outs, ws, gates, b_gate, D)
    return _matmul(merged, w_out.astype(BF16), F32)


def kernel(x, ln_g, ln_b, ffn1_w_gate, ffn1_w_up, ffn1_w_down, w_in, b_forget, b_gate, cmp_w1, cmp_w2, cmp_pos,
           w_branch, w_out, ffn2_w_gate, ffn2_w_up, ffn2_w_down):
    B, L, D = x.shape
    assert L % BLOCK == 0 and D % LANES == 0
    xf = x.reshape(B * L, D)
    xb = xf.astype(BF16)

    def ffn(xf, xb, wg, wu, wd, g, b):
        h = _ffn_up(xb, wg.astype(BF16), wu.astype(BF16))
        y = _matmul(h, wd.astype(BF16), F32)
        return _add_ln(xf, y, g, b, 0.5)

    for l in range(ln_g.shape[0]):
        xf, xb = ffn(xf, xb, ffn1_w_gate[l], ffn1_w_up[l], ffn1_w_down[l], ln_g[l, 0], ln_b[l, 0])
        y = _hybrid_mixer(xb, B, L, w_in[l], b_forget[l], b_gate[l], cmp_w1[l], cmp_w2[l], cmp_pos[l],
                          w_branch[l], w_out[l])
        xf, xb = _add_ln(xf, y, ln_g[l, 1], ln_b[l, 1], 1.0)
        xf, xb = ffn(xf, xb, ffn2_w_gate[l], ffn2_w_up[l], ffn2_w_down[l], ln_g[l, 2], ln_b[l, 2])
    return xf.reshape(B, L, D)
```

```python
import functools

import numpy as np
import jax
import jax.numpy as jnp
from jax import lax
from jax.experimental import pallas as pl
from jax.experimental.pallas import tpu as pltpu

F32 = jnp.float32
BF16 = jnp.bfloat16

HEAD_DIM = 128
BLOCK = 128
LANES = 128
NEG = -1e30
ATTN_SCALE = HEAD_DIM ** -0.5
LOG2E = 1.4426950408889634
LN2 = 0.6931471805599453
DEPTH = 2
ALPHA = (2 * DEPTH) ** 0.25
LN_EPS = 1e-5

A_PAIRS = ((128, 1), (512, 4), (2048, 16))
A_HEADS_PER_PAIR = 4
A_HEADS = A_HEADS_PER_PAIR * len(A_PAIRS)
B_HEADS = 8
B_KV_GROUPS = 2
B_GROUP_SIZE = B_HEADS // B_KV_GROUPS
B_CMP_LEN = 32
B_CMP_STRIDE = 16
B_SEL_LEN = 64
B_SEL_RATIO = B_SEL_LEN // B_CMP_STRIDE
B_N_SEL = 8
B_WINDOW = 512
B_CMP_HIDDEN = 512
C_HEADS = 8
D_HEADS = 8
D_IDX_HEADS = 8
D_IDX_DIM = 64
D_TOPK = 256
N_BRANCH = 4
N_ALIBI = A_HEADS + B_HEADS + D_HEADS
A_SLOPE_IDX = (0, 1, 2, 3, 12, 13, 14, 15, 24, 25, 26, 27)
B_SLOPE_IDX = (4, 5, 6, 7, 8, 9, 10, 11)
D_SLOPE_IDX = (16, 17, 18, 19, 20, 21, 22, 23)

CB_A = 0
CB_BQ = 36
CB_BKV = 44
CB_CQ, CB_CK, CB_CV = 56, 64, 72
CB_DQ = 80
CB_DIQ = 88
CB_DK, CB_DV = 92, 93
CB_MISC = 94
N_CB = 96
MISC_BG, MISC_CF, MISC_IK, MISC_IW = 0, 24, 32, 96

CAUSAL_CLASSES = 4
VMEM_LIMIT = 56 * 1024 * 1024


def _cparams(sem):
    return pltpu.CompilerParams(dimension_semantics=sem, vmem_limit_bytes=VMEM_LIMIT)


def _tile(dim, pref):
    return pref if dim % pref == 0 else dim


def _dot_nt(a, b):
    return lax.dot_general(a, b, (((1,), (1,)), ((), ())), preferred_element_type=F32)


def _softmax2_pv(s2, v):
    m = jnp.max(s2, axis=-1, keepdims=True)
    e = jnp.exp2(s2 - m)
    den = jnp.maximum(jnp.sum(e, axis=-1, keepdims=True), 1e-30)
    return jnp.dot(e.astype(BF16), v, preferred_element_type=F32) / den


def _for_causal_class(i, nb, body):
    n_cls = CAUSAL_CLASSES if nb % CAUSAL_CLASSES == 0 else 1
    per = nb // n_cls
    for c in range(n_cls):
        pl.when((i >= c * per) & (i < (c + 1) * per))(functools.partial(body, (c + 1) * per * BLOCK))


def _mm_kernel(x_ref, w_ref, o_ref, acc_ref):
    k = pl.program_id(2)

    @pl.when(k == 0)
    def _():
        acc_ref[...] = jnp.zeros_like(acc_ref)

    acc_ref[...] += jnp.dot(x_ref[...], w_ref[...], preferred_element_type=F32)

    @pl.when(k == pl.num_programs(2) - 1)
    def _():
        o_ref[...] = acc_ref[...].astype(o_ref.dtype)


def _matmul(x, w, out_dtype, tiles=(1024, 1024, 1024)):
    M, K = x.shape
    N = w.shape[1]
    tm, tn, tk = _tile(M, tiles[0]), _tile(N, tiles[1]), _tile(K, tiles[2])
    return pl.pallas_call(
        _mm_kernel,
        grid=(M // tm, N // tn, K // tk),
        in_specs=[pl.BlockSpec((tm, tk), lambda i, j, k: (i, k)),
                  pl.BlockSpec((tk, tn), lambda i, j, k: (k, j))],
        out_specs=pl.BlockSpec((tm, tn), lambda i, j, k: (i, j)),
        out_shape=jax.ShapeDtypeStruct((M, N), out_dtype),
        scratch_shapes=[pltpu.VMEM((tm, tn), F32)],
        compiler_params=_cparams(("parallel", "parallel", "arbitrary")),
        name="matmul",
    )(x, w)


def _ffn_up_kernel(x_ref, wg_ref, wu_ref, o_ref, accg_ref, accu_ref):
    k = pl.program_id(2)

    @pl.when(k == 0)
    def _():
        accg_ref[...] = jnp.zeros_like(accg_ref)
        accu_ref[...] = jnp.zeros_like(accu_ref)

    x = x_ref[...]
    accg_ref[...] += jnp.dot(x, wg_ref[...], preferred_element_type=F32)
    accu_ref[...] += jnp.dot(x, wu_ref[...], preferred_element_type=F32)

    @pl.when(k == pl.num_programs(2) - 1)
    def _():
        g = accg_ref[...]
        o_ref[...] = (g * jax.nn.sigmoid(g) * accu_ref[...]).astype(o_ref.dtype)


def _ffn_up(x, wg, wu, tiles=(1024, 1024, 1024)):
    M, K = x.shape
    N = wg.shape[1]
    tm, tn, tk = _tile(M, tiles[0]), _tile(N, tiles[1]), _tile(K, tiles[2])
    return pl.pallas_call(
        _ffn_up_kernel,
        grid=(M // tm, N // tn, K // tk),
        in_specs=[pl.BlockSpec((tm, tk), lambda i, j, k: (i, k)),
                  pl.BlockSpec((tk, tn), lambda i, j, k: (k, j)),
                  pl.BlockSpec((tk, tn), lambda i, j, k: (k, j))],
        out_specs=pl.BlockSpec((tm, tn), lambda i, j, k: (i, j)),
        out_shape=jax.ShapeDtypeStruct((M, N), BF16),
        scratch_shapes=[pltpu.VMEM((tm, tn), F32), pltpu.VMEM((tm, tn), F32)],
        compiler_params=_cparams(("parallel", "parallel", "arbitrary")),
        name="ffn_up",
    )(x, wg, wu)


def _add_ln_kernel(x_ref, y_ref, g_ref, b_ref, o_ref, ob_ref, *, cy):
    z = ALPHA * x_ref[...] + cy * y_ref[...]
    mu = jnp.mean(z, axis=-1, keepdims=True)
    zc = z - mu
    var = jnp.mean(zc * zc, axis=-1, keepdims=True)
    out = zc * lax.rsqrt(var + LN_EPS) * g_ref[...] + b_ref[...]
    o_ref[...] = out
    ob_ref[...] = out.astype(BF16)


def _add_ln(x, y, g, b, cy):
    M, D = x.shape
    tm = _tile(M, 256)
    row = pl.BlockSpec((tm, D), lambda i: (i, 0))
    vec = pl.BlockSpec((1, D), lambda i: (0, 0))
    return pl.pallas_call(
        functools.partial(_add_ln_kernel, cy=cy),
        grid=(M // tm,),
        in_specs=[row, row, vec, vec],
        out_specs=[row, row],
        out_shape=[jax.ShapeDtypeStruct((M, D), F32), jax.ShapeDtypeStruct((M, D), BF16)],
        compiler_params=_cparams(("parallel",)),
        name="add_ln",
    )(x, y, g.reshape(1, D), b.reshape(1, D))


def _banded_kernel(slope_ref, q_ref, k_ref, v_ref, *out_refs, window, n_prev, step, nb, hp, shared_kv):
    g = pl.program_id(1)
    i = pl.program_id(2)
    kwb = min(n_prev + 1, nb)
    kw = kwb * BLOCK
    start = pl.multiple_of(jnp.maximum(i - n_prev, 0) * BLOCK, BLOCK)
    qpos = i * BLOCK + lax.broadcasted_iota(jnp.int32, (BLOCK, kw), 0)
    kpos = start + lax.broadcasted_iota(jnp.int32, (BLOCK, kw), 1)
    dist = qpos - kpos
    maskadd = jnp.where((dist >= 0) & (dist <= window), 0.0, NEG)
    distf = (step * dist).astype(F32)
    for h in range(hp):
        cs = slice(h * HEAD_DIM, (h + 1) * HEAD_DIM)
        ks = slice(0, HEAD_DIM) if shared_kv else cs
        q = q_ref[0, :, cs].astype(BF16)
        kwin = k_ref[0, pl.ds(start, kw), ks].astype(BF16)
        vwin = v_ref[0, pl.ds(start, kw), ks].astype(BF16)
        s = _dot_nt(q, kwin) * ATTN_SCALE
        s = s - slope_ref[g * hp + h] * distf + maskadd
        m = jnp.max(s, axis=-1, keepdims=True)
        e = jnp.exp(s - m)
        den = jnp.maximum(jnp.sum(e, axis=-1, keepdims=True), 1e-30)
        out_refs[0][0, :, cs] = jnp.dot(e.astype(BF16), vwin, preferred_element_type=F32) / den
        if len(out_refs) > 1:
            out_refs[1][0, :, cs] = jnp.broadcast_to(m + jnp.log(den), (BLOCK, LANES))


def _banded_attention(qa, ka, va, slopes, *, n_groups, hp, shared_kv, q_blk, k_blk, v_blk, window, n_prev, step,
                      with_lse):
    B, N, _ = qa.shape
    nb = N // BLOCK
    kvw = HEAD_DIM if shared_kv else hp * HEAD_DIM
    qspec = pl.BlockSpec((1, BLOCK, hp * HEAD_DIM), lambda b, g, i: (b, i, q_blk(g)))
    kspec = pl.BlockSpec((1, N, kvw), lambda b, g, i: (b, 0, k_blk(g)))
    vspec = pl.BlockSpec((1, N, kvw), lambda b, g, i: (b, 0, v_blk(g)))
    ospec = pl.BlockSpec((1, BLOCK, hp * HEAD_DIM), lambda b, g, i: (b, i, g))
    oshape = jax.ShapeDtypeStruct((B, N, n_groups * hp * HEAD_DIM), F32)
    return pl.pallas_call(
        functools.partial(_banded_kernel, window=window, n_prev=n_prev, step=step, nb=nb, hp=hp, shared_kv=shared_kv),
        grid=(B, n_groups, nb),
        in_specs=[pl.BlockSpec(memory_space=pltpu.SMEM), qspec, kspec, vspec],
        out_specs=[ospec, ospec] if with_lse else [ospec],
        out_shape=[oshape, oshape] if with_lse else [oshape],
        compiler_params=_cparams(("parallel", "parallel", "arbitrary")),
        name="banded_attention",
    )(slopes, qa, ka, va)


def _a_combine_kernel(o0, o1, o2, l0, l1, l2, out_ref):
    a, b, c = l0[...], l1[...], l2[...]
    m = jnp.maximum(jnp.maximum(a, b), c)
    ea, eb, ec = jnp.exp(a - m), jnp.exp(b - m), jnp.exp(c - m)
    tot = ea + eb + ec
    out = (ea / tot) * o0[...] + (eb / tot) * o1[...] + (ec / tot) * o2[...]
    out_ref[...] = out.astype(out_ref.dtype)


def _a_combine(outs, lses):
    M, C = outs[0].shape
    tm = _tile(M, 512)
    spec = pl.BlockSpec((tm, C), lambda i: (i, 0))
    return pl.pallas_call(
        _a_combine_kernel,
        grid=(M // tm,),
        in_specs=[spec] * 6,
        out_specs=spec,
        out_shape=jax.ShapeDtypeStruct((M, C), BF16),
        compiler_params=_cparams(("parallel",)),
        name="a_combine",
    )(*outs, *lses)


def _dilated_attention(y3, slopes_all):
    B, L, _ = y3.shape
    slopes = slopes_all[np.array(A_SLOPE_IDX)]
    hpp = A_HEADS_PER_PAIR
    gw = 3 * hpp * HEAD_DIM
    outs, lses = [], []
    for g, (window, dil) in enumerate(A_PAIRS):
        n = L // dil
        assert n % BLOCK == 0
        steps = window // dil
        if dil == 1:
            ya, base = y3, CB_A // hpp + g
            blk = lambda part: (lambda r: base + part * (A_HEADS // hpp))
        else:
            cols = [y3[:, :, (CB_A + p * A_HEADS + g * hpp) * LANES:(CB_A + p * A_HEADS + (g + 1) * hpp) * LANES]
                    for p in range(3)]
            ya = jnp.concatenate(cols, axis=-1).reshape(B, n, dil * gw)
            blk = lambda part: (lambda r: r * 3 + part)
        o, lse = _banded_attention(ya, ya, ya, jnp.tile(slopes[g * hpp:(g + 1) * hpp], dil),
                                   n_groups=dil, hp=hpp, shared_kv=False, q_blk=blk(0), k_blk=blk(1), v_blk=blk(2),
                                   window=steps, n_prev=-(-steps // BLOCK), step=dil, with_lse=True)
        outs.append(o.reshape(B * L, hpp * HEAD_DIM))
        lses.append(lse.reshape(B * L, hpp * HEAD_DIM))
    return _a_combine(outs, lses)


def _logsig_cumsum_kernel(z_ref, bias_ref, o_ref):
    L = z_ref.shape[1]
    row = lax.broadcasted_iota(jnp.int32, (BLOCK, BLOCK), 0)
    col = lax.broadcasted_iota(jnp.int32, (BLOCK, BLOCK), 1)
    tri = jnp.where(row >= col, 1.0, 0.0).astype(F32)
    carry = jnp.zeros((1, LANES), F32)
    for j in range(L // BLOCK):
        z = z_ref[0, j * BLOCK:(j + 1) * BLOCK, :] + bias_ref[...]
        ls = jnp.minimum(z, 0.0) - jnp.log(1.0 + jnp.exp(-jnp.abs(z)))
        c = jnp.dot(tri, ls, preferred_element_type=F32, precision=lax.Precision.HIGHEST) + carry
        o_ref[0, j * BLOCK:(j + 1) * BLOCK, :] = c
        carry = c[BLOCK - 1:BLOCK, :]


def _logsig_cumsum(z, bias):
    B, L, _ = z.shape
    spec = pl.BlockSpec((1, L, LANES), lambda b: (b, 0, 0))
    return pl.pallas_call(
        _logsig_cumsum_kernel,
        grid=(B,),
        in_specs=[spec, pl.BlockSpec((1, LANES), lambda b: (0, 0))],
        out_specs=spec,
        out_shape=jax.ShapeDtypeStruct((B, L, LANES), F32),
        compiler_params=_cparams(("parallel",)),
        name="logsig_cumsum",
    )(z, bias)


def _fox_kernel(q_ref, k_ref, v_ref, cq_ref, ck_ref, o_ref):
    i = pl.program_id(2)
    nb = k_ref.shape[1] // BLOCK
    q = q_ref[0].astype(BF16)
    cq2 = cq_ref[0, 0] * LOG2E

    def body(kw):
        k = k_ref[0, :kw, :].astype(BF16)
        v = v_ref[0, :kw, :].astype(BF16)
        s2 = _dot_nt(q, k) * (ATTN_SCALE * LOG2E) + cq2 - ck_ref[0, 0, :, :kw] * LOG2E
        qpos = i * BLOCK + lax.broadcasted_iota(jnp.int32, (BLOCK, kw), 0)
        kpos = lax.broadcasted_iota(jnp.int32, (BLOCK, kw), 1)
        s2 = jnp.where(qpos >= kpos, s2, NEG)
        o_ref[0] = _softmax2_pv(s2, v).astype(o_ref.dtype)

    _for_causal_class(i, nb, body)


def _forgetting_attention(y3, c_col, c_row):
    B, L, _ = y3.shape
    nb = L // BLOCK
    return pl.pallas_call(
        _fox_kernel,
        grid=(B, C_HEADS, nb),
        in_specs=[pl.BlockSpec((1, BLOCK, HEAD_DIM), lambda b, h, i: (b, i, CB_CQ + h)),
                  pl.BlockSpec((1, L, HEAD_DIM), lambda b, h, i: (b, 0, CB_CK + h)),
                  pl.BlockSpec((1, L, HEAD_DIM), lambda b, h, i: (b, 0, CB_CV + h)),
                  pl.BlockSpec((1, 1, BLOCK, 1), lambda b, h, i: (b, h, i, 0)),
                  pl.BlockSpec((1, 1, 1, L), lambda b, h, i: (b, h, 0, 0))],
        out_specs=pl.BlockSpec((1, BLOCK, HEAD_DIM), lambda b, h, i: (b, i, h)),
        out_shape=jax.ShapeDtypeStruct((B, L, C_HEADS * HEAD_DIM), BF16),
        compiler_params=_cparams(("parallel", "parallel", "arbitrary")),
        name="forgetting_attention",
    )(y3, y3, y3, c_col, c_row)


def _order_key(x):
    bits = lax.bitcast_convert_type(x, jnp.int32)
    return bits ^ ((bits >> 31) & jnp.int32(0x7FFFFFFF))


def _kth_largest_key(key, k):
    rows = key.shape[0]

    def count_ge(t):
        return jnp.sum(jnp.where(key >= t, 1.0, 0.0), axis=-1, keepdims=True)

    t0 = jnp.where(count_ge(jnp.zeros((rows, 1), jnp.int32)) >= k,
                   jnp.int32(0), jnp.int32(-2 ** 31)) + jnp.zeros((rows, 1), jnp.int32)

    def body(it, t):
        cand = t | jnp.left_shift(jnp.int32(1), 30 - it)
        return jnp.where(count_ge(cand) >= k, cand, t)

    return lax.fori_loop(0, 31, body, t0)


def _dsa_kernel(slope_ref, iq_ref, ik_ref, iw_ref, q_ref, k_ref, v_ref, o_ref, *, n_top):
    i = pl.program_id(1)
    nb = k_ref.shape[1] // BLOCK
    iw = iw_ref[0]

    def body(kw):
        ik = ik_ref[0, :kw, :].astype(BF16)
        score = jnp.zeros((BLOCK, kw), F32)
        for h in range(D_IDX_HEADS):
            iq = iq_ref[0, :, h * D_IDX_DIM:(h + 1) * D_IDX_DIM].astype(BF16)
            rel = jnp.maximum(_dot_nt(iq, ik), 0.0)
            score = score + iw[:, h:h + 1] * rel
        qpos = i * BLOCK + lax.broadcasted_iota(jnp.int32, (BLOCK, kw), 0)
        kpos = lax.broadcasted_iota(jnp.int32, (BLOCK, kw), 1)
        dist = qpos - kpos
        causal = dist >= 0
        key = _order_key(jnp.where(causal, score, NEG))

        thr = _kth_largest_key(key, float(n_top))
        gt = key > thr
        eq = key == thr
        need = float(n_top) - jnp.sum(jnp.where(gt, 1.0, 0.0), axis=-1, keepdims=True)
        row = lax.broadcasted_iota(jnp.int32, (BLOCK, BLOCK), 0)
        col = lax.broadcasted_iota(jnp.int32, (BLOCK, BLOCK), 1)
        upper = jnp.where(row <= col, 1.0, 0.0).astype(BF16)
        eqf = jnp.where(eq, 1.0, 0.0).astype(BF16)
        carry = jnp.zeros((BLOCK, 1), F32)
        rank_tiles = []
        for j in range(kw // BLOCK):
            rank = carry + jnp.dot(eqf[:, j * BLOCK:(j + 1) * BLOCK], upper, preferred_element_type=F32)
            rank_tiles.append(rank)
            carry = rank[:, BLOCK - 1:BLOCK]
        rank = jnp.concatenate(rank_tiles, axis=1)
        mask = (gt | (eq & (rank <= need))) & causal
        maskadd = jnp.where(mask, 0.0, NEG)

        k = k_ref[0, :kw, :].astype(BF16)
        v = v_ref[0, :kw, :].astype(BF16)
        distf = dist.astype(F32)
        for h in range(D_HEADS):
            q = q_ref[0, :, h * HEAD_DIM:(h + 1) * HEAD_DIM].astype(BF16)
            s2 = _dot_nt(q, k) * (ATTN_SCALE * LOG2E) - (slope_ref[h] * LOG2E) * distf + maskadd
            o_ref[0, :, h * HEAD_DIM:(h + 1) * HEAD_DIM] = _softmax2_pv(s2, v).astype(o_ref.dtype)

    _for_causal_class(i, nb, body)


def _indexed_sparse_attention(y3, ik, iw, slopes_all):
    B, L, _ = y3.shape
    nb = L // BLOCK
    n_top = min(D_TOPK, L // 4)
    slopes = slopes_all[np.array(D_SLOPE_IDX)]
    qw, iqw = D_HEADS * HEAD_DIM, D_IDX_HEADS * D_IDX_DIM
    return pl.pallas_call(
        functools.partial(_dsa_kernel, n_top=n_top),
        grid=(B, nb),
        in_specs=[pl.BlockSpec(memory_space=pltpu.SMEM),
                  pl.BlockSpec((1, BLOCK, iqw), lambda b, i: (b, i, CB_DIQ * LANES // iqw)),
                  pl.BlockSpec((1, L, D_IDX_DIM), lambda b, i: (b, 0, 0)),
                  pl.BlockSpec((1, BLOCK, D_IDX_HEADS), lambda b, i: (b, i, 0)),
                  pl.BlockSpec((1, BLOCK, qw), lambda b, i: (b, i, CB_DQ * LANES // qw)),
                  pl.BlockSpec((1, L, HEAD_DIM), lambda b, i: (b, 0, CB_DK)),
                  pl.BlockSpec((1, L, HEAD_DIM), lambda b, i: (b, 0, CB_DV))],
        out_specs=pl.BlockSpec((1, BLOCK, qw), lambda b, i: (b, i, 0)),
        out_shape=jax.ShapeDtypeStruct((B, L, qw), BF16),
        compiler_params=_cparams(("parallel", "arbitrary")),
        name="indexed_sparse_attention",
    )(slopes, y3, ik, iw, y3, y3, y3)


def _compress_kernel(x_ref, pos_ref, w1_ref, w2_ref, o_ref):
    x = x_ref[0, 0].astype(F32)
    nxt = pltpu.roll(x, x.shape[0] - 1, 0)
    blk = jnp.concatenate([x, nxt], axis=1) + pos_ref[0]
    h = jnp.dot(blk.astype(BF16), w1_ref[0], preferred_element_type=F32)
    h = jax.nn.gelu(h, approximate=True)
    o_ref[0, 0] = jnp.dot(h.astype(BF16), w2_ref[0], preferred_element_type=F32).astype(o_ref.dtype)


def _compress(xc, pos, w1, w2):
    B, _, n_chunk, cw = xc.shape
    G = B_KV_GROUPS
    return pl.pallas_call(
        _compress_kernel,
        grid=(2 * G, B),
        in_specs=[pl.BlockSpec((1, 1, n_chunk, cw), lambda a, b: (b, a, 0, 0)),
                  pl.BlockSpec((1, 1, 2 * cw), lambda a, b: (a // G, 0, 0)),
                  pl.BlockSpec((1, 2 * cw, B_CMP_HIDDEN), lambda a, b: (a // G, 0, 0)),
                  pl.BlockSpec((1, B_CMP_HIDDEN, HEAD_DIM), lambda a, b: (a // G, 0, 0))],
        out_specs=pl.BlockSpec((1, 1, n_chunk, HEAD_DIM), lambda a, b: (b, a, 0, 0)),
        out_shape=jax.ShapeDtypeStruct((B, 2 * G, n_chunk, HEAD_DIM), BF16),
        compiler_params=_cparams(("arbitrary", "arbitrary")),
        name="nsa_compress",
    )(xc, pos, w1, w2)


def _nsa_cmp_kernel(slope_ref, *refs, n_cmp, n_slc):
    G, R = B_KV_GROUPS, B_GROUP_SIZE
    q_refs, (ckv_ref, o_ref, sel_ref) = refs[:G], refs[G:]
    i = pl.program_id(1)
    nc = ckv_ref.shape[2]
    t = i * BLOCK + lax.broadcasted_iota(jnp.int32, (BLOCK, nc), 0)
    n = lax.broadcasted_iota(jnp.int32, (BLOCK, nc), 1)
    dist_c = t - (n * B_CMP_STRIDE + B_CMP_LEN - 1)
    mask = (dist_c >= 0) & (n < n_cmp)
    distf = dist_c.astype(F32)

    nn = lax.broadcasted_iota(jnp.int32, (nc, LANES), 0)
    jj = lax.broadcasted_iota(jnp.int32, (nc, LANES), 1)
    off = nn - B_SEL_RATIO * jj + 1
    w = jnp.where((off == 0) | (off == B_SEL_RATIO), 1.0, jnp.where((off > 0) & (off < B_SEL_RATIO), 2.0, 0.0))
    w = jnp.where((nn < n_cmp) & (jj < n_slc), w, 0.0).astype(F32)
    tq = i * BLOCK + lax.broadcasted_iota(jnp.int32, (BLOCK, LANES), 0)
    j = lax.broadcasted_iota(jnp.int32, (BLOCK, LANES), 1)
    cur = tq >> int(np.log2(B_SEL_LEN))
    forced = (j == 0) | (j == cur) | (j == cur - 1)
    jf = j.astype(F32)

    for g in range(G):
        kc = ckv_ref[0, g]
        vc = ckv_ref[0, G + g]
        imp = jnp.zeros((BLOCK, nc), F32)
        for r in range(R):
            q = q_refs[g][0, :, r * HEAD_DIM:(r + 1) * HEAD_DIM].astype(BF16)
            s = _dot_nt(q, kc) * ATTN_SCALE
            s = jnp.where(mask, s - slope_ref[g * R + r] * distf, NEG)
            m = jnp.max(s, axis=-1, keepdims=True)
            e = jnp.where(mask, jnp.exp(s - m), 0.0)
            p = e / jnp.maximum(jnp.sum(e, axis=-1, keepdims=True), 1e-30)
            o_ref[0, :, (g * R + r) * HEAD_DIM:(g * R + r + 1) * HEAD_DIM] = jnp.dot(
                p.astype(BF16), vc, preferred_element_type=F32)
            imp = imp + p
        p_slc = jnp.dot(imp, w, preferred_element_type=F32, precision=lax.Precision.HIGHEST)
        score = jnp.where(forced, 1e9, jnp.where(j <= cur, p_slc, -1e9))
        score = jnp.where(j < n_slc, score, -3e38)
        sel = jnp.zeros((BLOCK, LANES), F32)
        for _ in range(min(B_N_SEL, n_slc)):
            m = jnp.max(score, axis=-1, keepdims=True)
            first = jnp.min(jnp.where(score == m, jf, float(LANES)), axis=-1, keepdims=True)
            hit = jf == first
            sel = jnp.where(hit, 1.0, sel)
            score = jnp.where(hit, -3e38, score)
        sel_ref[0, g] = sel.astype(sel_ref.dtype)


def _nsa_q_specs():
    return [pl.BlockSpec((1, BLOCK, B_GROUP_SIZE * HEAD_DIM), lambda b, i, g=g: (b, i, CB_BQ // B_GROUP_SIZE + g))
            for g in range(B_KV_GROUPS)]


def _nsa_cmp(y3, cmp_kv, slopes, n_cmp, n_slc):
    B, L, _ = y3.shape
    G = B_KV_GROUPS
    nc = cmp_kv.shape[2]
    return pl.pallas_call(
        functools.partial(_nsa_cmp_kernel, n_cmp=n_cmp, n_slc=n_slc),
        grid=(B, L // BLOCK),
        in_specs=[pl.BlockSpec(memory_space=pltpu.SMEM)] + _nsa_q_specs() +
                 [pl.BlockSpec((1, 2 * G, nc, HEAD_DIM), lambda b, i: (b, 0, 0, 0))],
        out_specs=[pl.BlockSpec((1, BLOCK, B_HEADS * HEAD_DIM), lambda b, i: (b, i, 0)),
                   pl.BlockSpec((1, G, BLOCK, LANES), lambda b, i: (b, 0, i, 0))],
        out_shape=[jax.ShapeDtypeStruct((B, L, B_HEADS * HEAD_DIM), F32),
                   jax.ShapeDtypeStruct((B, G, L, LANES), BF16)],
        compiler_params=_cparams(("parallel", "arbitrary")),
        name="nsa_compressed_attention",
    )(slopes, *([y3] * G), cmp_kv)


def _nsa_slc_kernel(slope_ref, *refs):
    G, R = B_KV_GROUPS, B_GROUP_SIZE
    q_refs, k_refs, v_refs, (sel_ref, o_ref) = refs[:G], refs[G:2 * G], refs[2 * G:3 * G], refs[3 * G:]
    i = pl.program_id(1)
    nb = k_refs[0].shape[1] // BLOCK

    def body(kw):
        jj = lax.broadcasted_iota(jnp.int32, (LANES, kw), 0)
        ss = lax.broadcasted_iota(jnp.int32, (LANES, kw), 1)
        expand = jnp.where((ss >> int(np.log2(B_SEL_LEN))) == jj, 1.0, 0.0).astype(BF16)
        qpos = i * BLOCK + lax.broadcasted_iota(jnp.int32, (BLOCK, kw), 0)
        kpos = lax.broadcasted_iota(jnp.int32, (BLOCK, kw), 1)
        dist = qpos - kpos
        distf = dist.astype(F32)
        for g in range(G):
            picked = jnp.dot(sel_ref[0, g], expand, preferred_element_type=F32) > 0.5
            maskadd = jnp.where(picked & (dist >= 0), 0.0, NEG)
            k = k_refs[g][0, :kw, :].astype(BF16)
            v = v_refs[g][0, :kw, :].astype(BF16)
            for r in range(R):
                q = q_refs[g][0, :, r * HEAD_DIM:(r + 1) * HEAD_DIM].astype(BF16)
                s2 = _dot_nt(q, k) * (ATTN_SCALE * LOG2E) - (slope_ref[g * R + r] * LOG2E) * distf + maskadd
                o_ref[0, :, (g * R + r) * HEAD_DIM:(g * R + r + 1) * HEAD_DIM] = _softmax2_pv(s2, v)

    _for_causal_class(i, nb, body)


def _nsa_slc(y3, sel, slopes):
    B, L, _ = y3.shape
    G = B_KV_GROUPS
    kv_specs = [pl.BlockSpec((1, L, HEAD_DIM), lambda b, i, c=CB_BKV + 4 + kv * G + g: (b, 0, c))
                for kv in range(2) for g in range(G)]
    return pl.pallas_call(
        _nsa_slc_kernel,
        grid=(B, L // BLOCK),
        in_specs=[pl.BlockSpec(memory_space=pltpu.SMEM)] + _nsa_q_specs() + kv_specs +
                 [pl.BlockSpec((1, G, BLOCK, LANES), lambda b, i: (b, 0, i, 0))],
        out_specs=pl.BlockSpec((1, BLOCK, B_HEADS * HEAD_DIM), lambda b, i: (b, i, 0)),
        out_shape=jax.ShapeDtypeStruct((B, L, B_HEADS * HEAD_DIM), F32),
        compiler_params=_cparams(("parallel", "arbitrary")),
        name="nsa_selected_attention",
    )(slopes, *([y3] * (3 * G)), sel)


def _b_gate_kernel(gl_ref, oc_ref, os_ref, ow_ref, out_ref):
    gate = jax.nn.sigmoid(gl_ref[...])
    for h in range(B_HEADS):
        sl = slice(h * HEAD_DIM, (h + 1) * HEAD_DIM)
        out = (gate[:, h:h + 1] * oc_ref[:, sl] + gate[:, B_HEADS + h:B_HEADS + h + 1] * os_ref[:, sl]
               + gate[:, 2 * B_HEADS + h:2 * B_HEADS + h + 1] * ow_ref[:, sl])
        out_ref[:, sl] = out.astype(out_ref.dtype)


def _b_gate(gl, o_cmp, o_slc, o_win):
    M, C = o_cmp.shape
    tm = _tile(M, 512)
    spec = pl.BlockSpec((tm, C), lambda i: (i, 0))
    return pl.pallas_call(
        _b_gate_kernel,
        grid=(M // tm,),
        in_specs=[pl.BlockSpec((tm, gl.shape[1]), lambda i: (i, 0)), spec, spec, spec],
        out_specs=spec,
        out_shape=jax.ShapeDtypeStruct((M, C), BF16),
        compiler_params=_cparams(("parallel",)),
        name="nsa_branch_gate",
    )(gl, o_cmp, o_slc, o_win)


def _native_sparse_attention(y3, gate_logits, cmp_w1, cmp_w2, cmp_pos, slopes_all):
    B, L, _ = y3.shape
    G, R = B_KV_GROUPS, B_GROUP_SIZE
    slopes = slopes_all[np.array(B_SLOPE_IDX)]
    n_chunk = L // B_CMP_STRIDE
    n_cmp = n_chunk - B_CMP_LEN // B_CMP_STRIDE + 1
    n_slc = L // B_SEL_LEN
    assert B_CMP_LEN == 2 * B_CMP_STRIDE and n_chunk % 8 == 0 and n_slc <= LANES

    xc = y3[:, :, CB_BKV * LANES:(CB_BKV + 2 * G) * LANES].reshape(B, L, 2 * G, HEAD_DIM)
    xc = xc.transpose(0, 2, 1, 3).reshape(B, 2 * G, n_chunk, B_CMP_STRIDE * HEAD_DIM)
    cmp_kv = _compress(xc, cmp_pos.reshape(2, 1, B_CMP_LEN * HEAD_DIM), cmp_w1.astype(BF16), cmp_w2.astype(BF16))

    o_cmp, sel = _nsa_cmp(y3, cmp_kv, slopes, n_cmp, n_slc)
    o_slc = _nsa_slc(y3, sel, slopes)
    w_steps = B_WINDOW - 1
    (o_win,) = _banded_attention(y3, y3, y3, slopes, n_groups=G, hp=R, shared_kv=True,
                                 q_blk=lambda g: CB_BQ // R + g,
                                 k_blk=lambda g: CB_BKV + 8 + g,
                                 v_blk=lambda g: CB_BKV + 10 + g,
                                 window=w_steps, n_prev=-(-w_steps // BLOCK), step=1, with_lse=False)
    C = B_HEADS * HEAD_DIM
    return _b_gate(gate_logits, o_cmp.reshape(B * L, C), o_slc.reshape(B * L, C), o_win.reshape(B * L, C))


def _merge_kernel(oa, ob, oc, od, wa, wb, wc, wd, g0, g1, g2, g3, b0, b1, b2, b3, out_ref):
    acc = jax.nn.sigmoid(g0[...] + b0[...]) * jnp.dot(oa[...], wa[...], preferred_element_type=F32)
    acc = acc + jax.nn.sigmoid(g1[...] + b1[...]) * jnp.dot(ob[...], wb[...], preferred_element_type=F32)
    acc = acc + jax.nn.sigmoid(g2[...] + b2[...]) * jnp.dot(oc[...], wc[...], preferred_element_type=F32)
    acc = acc + jax.nn.sigmoid(g3[...] + b3[...]) * jnp.dot(od[...], wd[...], preferred_element_type=F32)
    out_ref[...] = acc.astype(out_ref.dtype)


def _merge(branch_outs, branch_ws, gate_logits, b_gate, D):
    M = gate_logits.shape[0]
    tm, tn = _tile(M, 512), _tile(D, 512)
    nj = D // tn
    o_specs = [pl.BlockSpec((tm, o.shape[1]), lambda i, j: (i, 0)) for o in branch_outs]
    w_specs = [pl.BlockSpec((w.shape[0], tn), lambda i, j: (0, j)) for w in branch_ws]
    g_specs = [pl.BlockSpec((tm, tn), lambda i, j, c=c: (i, c * nj + j)) for c in range(N_BRANCH)]
    b_specs = [pl.BlockSpec((1, tn), lambda i, j, c=c: (0, c * nj + j)) for c in range(N_BRANCH)]
    return pl.pallas_call(
        _merge_kernel,
        grid=(M // tm, nj),
        in_specs=o_specs + w_specs + g_specs + b_specs,
        out_specs=pl.BlockSpec((tm, tn), lambda i, j: (i, j)),
        out_shape=jax.ShapeDtypeStruct((M, D), BF16),
        compiler_params=_cparams(("parallel", "arbitrary")),
        name="gated_merge",
    )(*branch_outs, *branch_ws, *([gate_logits] * N_BRANCH), *([b_gate.reshape(1, -1)] * N_BRANCH))


def _attn_weight(w_in):
    D = w_in.shape[0]
    sizes = (A_HEADS * HEAD_DIM,) * 3 + (B_HEADS * HEAD_DIM, 3 * 2 * B_KV_GROUPS * HEAD_DIM, 3 * B_HEADS) + \
            (C_HEADS * HEAD_DIM,) * 3 + (C_HEADS,) + (D_HEADS * HEAD_DIM, HEAD_DIM, HEAD_DIM) + \
            (D_IDX_HEADS * D_IDX_DIM, D_IDX_DIM, D_IDX_HEADS)
    offs = np.concatenate([[0], np.cumsum(sizes)])
    (a_q, a_k, a_v, b_q, b_kv, b_g, c_q, c_k, c_v, c_f, d_q, d_k, d_v, d_iq, d_ik, d_iw) = [
        w_in[:, offs[n]:offs[n + 1]] for n in range(len(sizes))]
    misc = jnp.zeros((D, 2 * LANES), w_in.dtype)
    misc = misc.at[:, MISC_BG:MISC_BG + 3 * B_HEADS].set(b_g)
    misc = misc.at[:, MISC_CF:MISC_CF + C_HEADS].set(c_f)
    misc = misc.at[:, MISC_IK:MISC_IK + D_IDX_DIM].set(d_ik)
    misc = misc.at[:, MISC_IW:MISC_IW + D_IDX_HEADS].set(d_iw)
    w = jnp.concatenate([a_q, a_k, a_v, b_q, b_kv, c_q, c_k, c_v, d_q, d_iq, d_k, d_v, misc], axis=1)
    assert w.shape[1] == N_CB * LANES
    return w.astype(BF16), int(offs[-1])


def _hybrid_mixer(xb, B, L, w_in, b_forget, b_gate, cmp_w1, cmp_w2, cmp_pos, w_branch, w_out, tiles):
    M, D = xb.shape
    slopes_all = jnp.exp2(-8.0 * jnp.arange(1, N_ALIBI + 1, dtype=F32) / N_ALIBI)
    w_attn, gate_off = _attn_weight(w_in)
    y3 = _matmul(xb, w_attn, BF16, tiles).reshape(B, L, N_CB * LANES)
    misc = _matmul(xb, w_attn[:, CB_MISC * LANES:(CB_MISC + 1) * LANES], F32)
    gates = _matmul(xb, w_in[:, gate_off:].astype(BF16), F32, tiles)

    bias = jnp.zeros((1, LANES), F32).at[0, MISC_CF:MISC_CF + C_HEADS].set(b_forget)
    c = _logsig_cumsum(misc.reshape(B, L, LANES), bias)[:, :, MISC_CF:MISC_CF + C_HEADS].transpose(0, 2, 1)

    o_a = _dilated_attention(y3, slopes_all)
    o_b = _native_sparse_attention(y3, misc[:, MISC_BG:MISC_BG + 3 * B_HEADS], cmp_w1, cmp_w2, cmp_pos, slopes_all)
    o_c = _forgetting_attention(y3, c[..., None], c[:, :, None, :])
    o_d = _indexed_sparse_attention(y3, misc[:, MISC_IK:MISC_IK + D_IDX_DIM].reshape(B, L, D_IDX_DIM),
                                    misc[:, MISC_IW:MISC_IW + D_IDX_HEADS].reshape(B, L, D_IDX_HEADS), slopes_all)

    sizes = (A_HEADS_PER_PAIR * HEAD_DIM, B_HEADS * HEAD_DIM, C_HEADS * HEAD_DIM, D_HEADS * HEAD_DIM)
    offs = np.concatenate([[0], np.cumsum(sizes)])
    ws = [w_branch[offs[n]:offs[n + 1]].astype(BF16) for n in range(N_BRANCH)]
    outs = [o_a, o_b, o_c.reshape(M, -1), o_d.reshape(M, -1)]
    merged = _merge(outs, ws, gates, b_gate, D)
    return _matmul(merged, w_out.astype(BF16), F32, tiles)


def kernel(x, ln_g, ln_b, ffn1_w_gate, ffn1_w_up, ffn1_w_down, w_in, b_forget, b_gate, cmp_w1, cmp_w2, cmp_pos,
           w_branch, w_out, ffn2_w_gate, ffn2_w_up, ffn2_w_down):
    B, L, D = x.shape
    assert L % BLOCK == 0 and D % LANES == 0
    xf = x.reshape(B * L, D)
    xb = xf.astype(BF16)
    tilings = ((1024, 1024, 1024), (1024, 1024, 2048), (2048, 1024, 1024), (1024, 2048, 1024))

    def ffn(xf, xb, wg, wu, wd, g, b, tiles):
        h = _ffn_up(xb, wg.astype(BF16), wu.astype(BF16), tiles)
        y = _matmul(h, wd.astype(BF16), F32, tiles)
        return _add_ln(xf, y, g, b, 0.5)

    for l in range(ln_g.shape[0]):
        xf, xb = ffn(xf, xb, ffn1_w_gate[l], ffn1_w_up[l], ffn1_w_down[l], ln_g[l, 0], ln_b[l, 0], tilings[2 * l])
        y = _hybrid_mixer(xb, B, L, w_in[l], b_forget[l], b_gate[l], cmp_w1[l], cmp_w2[l], cmp_pos[l],
                          w_branch[l], w_out[l], tilings[l])
        xf, xb = _add_ln(xf, y, ln_g[l, 1], ln_b[l, 1], 1.0)
        xf, xb = ffn(xf, xb, ffn2_w_gate[l], ffn2_w_up[l], ffn2_w_down[l], ln_g[l, 2], ln_b[l, 2],
                     tilings[2 * l + 1])
    return xf.reshape(B, L, D)
```

```python
import functools

import numpy as np
import jax
import jax.numpy as jnp
from jax import lax
from jax.experimental import pallas as pl
from jax.experimental.pallas import tpu as pltpu

F32 = jnp.float32
BF16 = jnp.bfloat16

HEAD_DIM = 128
BLOCK = 128
LANES = 128
NEG = -1e30
ATTN_SCALE = HEAD_DIM ** -0.5
LOG2E = 1.4426950408889634
LN2 = 0.6931471805599453
DEPTH = 2
ALPHA = (2 * DEPTH) ** 0.25
LN_EPS = 1e-5

A_PAIRS = ((128, 1), (512, 4), (2048, 16))
A_HEADS_PER_PAIR = 4
A_HEADS = A_HEADS_PER_PAIR * len(A_PAIRS)
B_HEADS = 8
B_KV_GROUPS = 2
B_GROUP_SIZE = B_HEADS // B_KV_GROUPS
B_CMP_LEN = 32
B_CMP_STRIDE = 16
B_SEL_LEN = 64
B_SEL_RATIO = B_SEL_LEN // B_CMP_STRIDE
B_N_SEL = 8
B_WINDOW = 512
B_CMP_HIDDEN = 512
C_HEADS = 8
D_HEADS = 8
D_IDX_HEADS = 8
D_IDX_DIM = 64
D_TOPK = 256
N_BRANCH = 4
N_ALIBI = A_HEADS + B_HEADS + D_HEADS
A_SLOPE_IDX = (0, 1, 2, 3, 12, 13, 14, 15, 24, 25, 26, 27)
B_SLOPE_IDX = (4, 5, 6, 7, 8, 9, 10, 11)
D_SLOPE_IDX = (16, 17, 18, 19, 20, 21, 22, 23)

CB_A = 0
CB_BQ = 36
CB_BKV = 44
CB_CQ, CB_CK, CB_CV = 56, 64, 72
CB_DQ = 80
CB_DIQ = 88
CB_DK, CB_DV = 92, 93
CB_MISC = 94
N_CB = 96
MISC_BG, MISC_CF, MISC_IK, MISC_IW = 0, 24, 32, 96

CAUSAL_CLASSES = 4
VMEM_LIMIT = 56 * 1024 * 1024


def _cparams(sem):
    return pltpu.CompilerParams(dimension_semantics=sem, vmem_limit_bytes=VMEM_LIMIT)


def _tile(dim, pref):
    return pref if dim % pref == 0 else dim


def _dot_nt(a, b):
    return lax.dot_general(a, b, (((1,), (1,)), ((), ())), preferred_element_type=F32)


def _softmax2_pv(s2, v):
    m = jnp.max(s2, axis=-1, keepdims=True)
    e = jnp.exp2(s2 - m)
    den = jnp.maximum(jnp.sum(e, axis=-1, keepdims=True), 1e-30)
    return jnp.dot(e.astype(BF16), v, preferred_element_type=F32) / den


def _for_causal_class(i, nb, body):
    n_cls = CAUSAL_CLASSES if nb % CAUSAL_CLASSES == 0 else 1
    per = nb // n_cls
    for c in range(n_cls):
        pl.when((i >= c * per) & (i < (c + 1) * per))(functools.partial(body, (c + 1) * per * BLOCK))


def _mm_kernel(x_ref, w_ref, o_ref, acc_ref):
    k = pl.program_id(2)

    @pl.when(k == 0)
    def _():
        acc_ref[...] = jnp.zeros_like(acc_ref)

    acc_ref[...] += jnp.dot(x_ref[...], w_ref[...], preferred_element_type=F32)

    @pl.when(k == pl.num_programs(2) - 1)
    def _():
        o_ref[...] = acc_ref[...].astype(o_ref.dtype)


def _matmul(x, w, out_dtype, tiles=(1024, 1024, 2048)):
    M, K = x.shape
    N = w.shape[1]
    tm, tn, tk = _tile(M, tiles[0]), _tile(N, tiles[1]), _tile(K, tiles[2])
    return pl.pallas_call(
        _mm_kernel,
        grid=(M // tm, N // tn, K // tk),
        in_specs=[pl.BlockSpec((tm, tk), lambda i, j, k: (i, k)),
                  pl.BlockSpec((tk, tn), lambda i, j, k: (k, j))],
        out_specs=pl.BlockSpec((tm, tn), lambda i, j, k: (i, j)),
        out_shape=jax.ShapeDtypeStruct((M, N), out_dtype),
        scratch_shapes=[pltpu.VMEM((tm, tn), F32)],
        compiler_params=_cparams(("parallel", "parallel", "arbitrary")),
        name="matmul",
    )(x, w)


def _ffn_up_kernel(x_ref, wg_ref, wu_ref, o_ref, accg_ref, accu_ref):
    k = pl.program_id(2)

    @pl.when(k == 0)
    def _():
        accg_ref[...] = jnp.zeros_like(accg_ref)
        accu_ref[...] = jnp.zeros_like(accu_ref)

    x = x_ref[...]
    accg_ref[...] += jnp.dot(x, wg_ref[...], preferred_element_type=F32)
    accu_ref[...] += jnp.dot(x, wu_ref[...], preferred_element_type=F32)

    @pl.when(k == pl.num_programs(2) - 1)
    def _():
        g = accg_ref[...]
        o_ref[...] = (g * jax.nn.sigmoid(g) * accu_ref[...]).astype(o_ref.dtype)


def _ffn_up(x, wg, wu, tiles=(1024, 1024, 2048)):
    M, K = x.shape
    N = wg.shape[1]
    tm, tn, tk = _tile(M, tiles[0]), _tile(N, tiles[1]), _tile(K, tiles[2])
    return pl.pallas_call(
        _ffn_up_kernel,
        grid=(M // tm, N // tn, K // tk),
        in_specs=[pl.BlockSpec((tm, tk), lambda i, j, k: (i, k)),
                  pl.BlockSpec((tk, tn), lambda i, j, k: (k, j)),
                  pl.BlockSpec((tk, tn), lambda i, j, k: (k, j))],
        out_specs=pl.BlockSpec((tm, tn), lambda i, j, k: (i, j)),
        out_shape=jax.ShapeDtypeStruct((M, N), BF16),
        scratch_shapes=[pltpu.VMEM((tm, tn), F32), pltpu.VMEM((tm, tn), F32)],
        compiler_params=_cparams(("parallel", "parallel", "arbitrary")),
        name="ffn_up",
    )(x, wg, wu)


def _add_ln_kernel(x_ref, y_ref, g_ref, b_ref, o_ref, ob_ref, *, cy):
    z = ALPHA * x_ref[...] + cy * y_ref[...]
    mu = jnp.mean(z, axis=-1, keepdims=True)
    zc = z - mu
    var = jnp.mean(zc * zc, axis=-1, keepdims=True)
    out = zc * lax.rsqrt(var + LN_EPS) * g_ref[...] + b_ref[...]
    o_ref[...] = out
    ob_ref[...] = out.astype(BF16)


def _add_ln(x, y, g, b, cy):
    M, D = x.shape
    tm = _tile(M, 256)
    row = pl.BlockSpec((tm, D), lambda i: (i, 0))
    vec = pl.BlockSpec((1, D), lambda i: (0, 0))
    return pl.pallas_call(
        functools.partial(_add_ln_kernel, cy=cy),
        grid=(M // tm,),
        in_specs=[row, row, vec, vec],
        out_specs=[row, row],
        out_shape=[jax.ShapeDtypeStruct((M, D), F32), jax.ShapeDtypeStruct((M, D), BF16)],
        compiler_params=_cparams(("parallel",)),
        name="add_ln",
    )(x, y, g.reshape(1, D), b.reshape(1, D))


def _banded_kernel(slope_ref, q_ref, k_ref, v_ref, *out_refs, window, n_prev, step, nb, hp, shared_kv):
    g = pl.program_id(1)
    i = pl.program_id(2)
    kwb = min(n_prev + 1, nb)
    kw = kwb * BLOCK
    start = pl.multiple_of(jnp.maximum(i - n_prev, 0) * BLOCK, BLOCK)
    qpos = i * BLOCK + lax.broadcasted_iota(jnp.int32, (BLOCK, kw), 0)
    kpos = start + lax.broadcasted_iota(jnp.int32, (BLOCK, kw), 1)
    dist = qpos - kpos
    maskadd = jnp.where((dist >= 0) & (dist <= window), 0.0, NEG)
    distf = (step * dist).astype(F32)
    for h in range(hp):
        cs = slice(h * HEAD_DIM, (h + 1) * HEAD_DIM)
        ks = slice(0, HEAD_DIM) if shared_kv else cs
        q = q_ref[0, :, cs].astype(BF16)
        kwin = k_ref[0, pl.ds(start, kw), ks].astype(BF16)
        vwin = v_ref[0, pl.ds(start, kw), ks].astype(BF16)
        s = _dot_nt(q, kwin) * ATTN_SCALE
        s = s - slope_ref[g * hp + h] * distf + maskadd
        m = jnp.max(s, axis=-1, keepdims=True)
        e = jnp.exp(s - m)
        den = jnp.maximum(jnp.sum(e, axis=-1, keepdims=True), 1e-30)
        out_refs[0][0, :, cs] = jnp.dot(e.astype(BF16), vwin, preferred_element_type=F32) / den
        if len(out_refs) > 1:
            out_refs[1][0, :, cs] = jnp.broadcast_to(m + jnp.log(den), (BLOCK, LANES))


def _banded_attention(qa, ka, va, slopes, *, n_groups, hp, shared_kv, q_blk, k_blk, v_blk, window, n_prev, step,
                      with_lse):
    B, N, _ = qa.shape
    nb = N // BLOCK
    kvw = HEAD_DIM if shared_kv else hp * HEAD_DIM
    qspec = pl.BlockSpec((1, BLOCK, hp * HEAD_DIM), lambda b, g, i: (b, i, q_blk(g)))
    kspec = pl.BlockSpec((1, N, kvw), lambda b, g, i: (b, 0, k_blk(g)))
    vspec = pl.BlockSpec((1, N, kvw), lambda b, g, i: (b, 0, v_blk(g)))
    ospec = pl.BlockSpec((1, BLOCK, hp * HEAD_DIM), lambda b, g, i: (b, i, g))
    oshape = jax.ShapeDtypeStruct((B, N, n_groups * hp * HEAD_DIM), F32)
    return pl.pallas_call(
        functools.partial(_banded_kernel, window=window, n_prev=n_prev, step=step, nb=nb, hp=hp, shared_kv=shared_kv),
        grid=(B, n_groups, nb),
        in_specs=[pl.BlockSpec(memory_space=pltpu.SMEM), qspec, kspec, vspec],
        out_specs=[ospec, ospec] if with_lse else [ospec],
        out_shape=[oshape, oshape] if with_lse else [oshape],
        compiler_params=_cparams(("parallel", "parallel", "arbitrary")),
        name="banded_attention",
    )(slopes, qa, ka, va)


def _a_combine_kernel(o0, o1, o2, l0, l1, l2, out_ref):
    a, b, c = l0[...], l1[...], l2[...]
    m = jnp.maximum(jnp.maximum(a, b), c)
    ea, eb, ec = jnp.exp(a - m), jnp.exp(b - m), jnp.exp(c - m)
    tot = ea + eb + ec
    out = (ea / tot) * o0[...] + (eb / tot) * o1[...] + (ec / tot) * o2[...]
    out_ref[...] = out.astype(out_ref.dtype)


def _a_combine(outs, lses):
    M, C = outs[0].shape
    tm = _tile(M, 512)
    spec = pl.BlockSpec((tm, C), lambda i: (i, 0))
    return pl.pallas_call(
        _a_combine_kernel,
        grid=(M // tm,),
        in_specs=[spec] * 6,
        out_specs=spec,
        out_shape=jax.ShapeDtypeStruct((M, C), BF16),
        compiler_params=_cparams(("parallel",)),
        name="a_combine",
    )(*outs, *lses)


def _dilated_attention(y3, slopes_all):
    B, L, _ = y3.shape
    slopes = slopes_all[np.array(A_SLOPE_IDX)]
    hpp = A_HEADS_PER_PAIR
    gw = 3 * hpp * HEAD_DIM
    outs, lses = [], []
    for g, (window, dil) in enumerate(A_PAIRS):
        n = L // dil
        assert n % BLOCK == 0
        steps = window // dil
        if dil == 1:
            ya, base = y3, CB_A // hpp + g
            blk = lambda part: (lambda r: base + part * (A_HEADS // hpp))
        else:
            cols = [y3[:, :, (CB_A + p * A_HEADS + g * hpp) * LANES:(CB_A + p * A_HEADS + (g + 1) * hpp) * LANES]
                    for p in range(3)]
            ya = jnp.concatenate(cols, axis=-1).reshape(B, n, dil * gw)
            blk = lambda part: (lambda r: r * 3 + part)
        o, lse = _banded_attention(ya, ya, ya, jnp.tile(slopes[g * hpp:(g + 1) * hpp], dil),
                                   n_groups=dil, hp=hpp, shared_kv=False, q_blk=blk(0), k_blk=blk(1), v_blk=blk(2),
                                   window=steps, n_prev=-(-steps // BLOCK), step=dil, with_lse=True)
        outs.append(o.reshape(B * L, hpp * HEAD_DIM))
        lses.append(lse.reshape(B * L, hpp * HEAD_DIM))
    return _a_combine(outs, lses)


def _logsig_cumsum_kernel(z_ref, bias_ref, o_ref):
    L = z_ref.shape[1]
    row = lax.broadcasted_iota(jnp.int32, (BLOCK, BLOCK), 0)
    col = lax.broadcasted_iota(jnp.int32, (BLOCK, BLOCK), 1)
    tri = jnp.where(row >= col, 1.0, 0.0).astype(F32)
    carry = jnp.zeros((1, LANES), F32)
    for j in range(L // BLOCK):
        z = z_ref[0, j * BLOCK:(j + 1) * BLOCK, :] + bias_ref[...]
        ls = jnp.minimum(z, 0.0) - jnp.log(1.0 + jnp.exp(-jnp.abs(z)))
        c = jnp.dot(tri, ls, preferred_element_type=F32, precision=lax.Precision.HIGHEST) + carry
        o_ref[0, j * BLOCK:(j + 1) * BLOCK, :] = c
        carry = c[BLOCK - 1:BLOCK, :]


def _logsig_cumsum(z, bias):
    B, L, _ = z.shape
    spec = pl.BlockSpec((1, L, LANES), lambda b: (b, 0, 0))
    return pl.pallas_call(
        _logsig_cumsum_kernel,
        grid=(B,),
        in_specs=[spec, pl.BlockSpec((1, LANES), lambda b: (0, 0))],
        out_specs=spec,
        out_shape=jax.ShapeDtypeStruct((B, L, LANES), F32),
        compiler_params=_cparams(("parallel",)),
        name="logsig_cumsum",
    )(z, bias)


def _fox_kernel(q_ref, k_ref, v_ref, cq_ref, ck_ref, o_ref):
    i = pl.program_id(1)
    nb = k_ref.shape[1] // BLOCK

    def body(kw):
        qpos = i * BLOCK + lax.broadcasted_iota(jnp.int32, (BLOCK, kw), 0)
        kpos = lax.broadcasted_iota(jnp.int32, (BLOCK, kw), 1)
        maskadd = jnp.where(qpos >= kpos, 0.0, NEG)
        for h in range(C_HEADS):
            cs = slice(h * HEAD_DIM, (h + 1) * HEAD_DIM)
            q = q_ref[0, :, cs].astype(BF16)
            k = k_ref[0, :kw, cs].astype(BF16)
            v = v_ref[0, :kw, cs].astype(BF16)
            s2 = _dot_nt(q, k) * (ATTN_SCALE * LOG2E) + cq_ref[0, h] * LOG2E - ck_ref[0, h, :, :kw] * LOG2E
            o_ref[0, :, cs] = _softmax2_pv(s2 + maskadd, v).astype(o_ref.dtype)

    _for_causal_class(i, nb, body)


def _forgetting_attention(y3, c_col, c_row):
    B, L, _ = y3.shape
    nb = L // BLOCK
    cw = C_HEADS * HEAD_DIM
    return pl.pallas_call(
        _fox_kernel,
        grid=(B, nb),
        in_specs=[pl.BlockSpec((1, BLOCK, cw), lambda b, i: (b, i, CB_CQ // C_HEADS)),
                  pl.BlockSpec((1, L, cw), lambda b, i: (b, 0, CB_CK // C_HEADS)),
                  pl.BlockSpec((1, L, cw), lambda b, i: (b, 0, CB_CV // C_HEADS)),
                  pl.BlockSpec((1, C_HEADS, BLOCK, 1), lambda b, i: (b, 0, i, 0)),
                  pl.BlockSpec((1, C_HEADS, 1, L), lambda b, i: (b, 0, 0, 0))],
        out_specs=pl.BlockSpec((1, BLOCK, cw), lambda b, i: (b, i, 0)),
        out_shape=jax.ShapeDtypeStruct((B, L, cw), BF16),
        compiler_params=_cparams(("parallel", "arbitrary")),
        name="forgetting_attention",
    )(y3, y3, y3, c_col, c_row)


def _order_key(x):
    bits = lax.bitcast_convert_type(x, jnp.int32)
    return bits ^ ((bits >> 31) & jnp.int32(0x7FFFFFFF))


def _kth_largest_key(key, k):
    rows = key.shape[0]

    def count_ge(t):
        return jnp.sum(jnp.where(key >= t, 1.0, 0.0), axis=-1, keepdims=True)

    t0 = jnp.where(count_ge(jnp.zeros((rows, 1), jnp.int32)) >= k,
                   jnp.int32(0), jnp.int32(-2 ** 31)) + jnp.zeros((rows, 1), jnp.int32)

    def two_bits(it, t):
        lo = jnp.left_shift(jnp.int32(1), 29 - 2 * it)
        c1, c2 = t | lo, t | (lo + lo)
        c3 = c2 | lo
        n1, n2, n3 = count_ge(c1), count_ge(c2), count_ge(c3)
        return jnp.where(n3 >= k, c3, jnp.where(n2 >= k, c2, jnp.where(n1 >= k, c1, t)))

    t = lax.fori_loop(0, 15, two_bits, t0)
    last = t | jnp.int32(1)
    return jnp.where(count_ge(last) >= k, last, t)


def _dsa_kernel(slope_ref, iq_ref, ik_ref, iw_ref, q_ref, k_ref, v_ref, o_ref, *, n_top):
    i = pl.program_id(1)
    nb = k_ref.shape[1] // BLOCK
    iw = iw_ref[0]

    def body(kw):
        ik = ik_ref[0, :kw, :].astype(BF16)
        score = jnp.zeros((BLOCK, kw), F32)
        for h in range(D_IDX_HEADS):
            iq = iq_ref[0, :, h * D_IDX_DIM:(h + 1) * D_IDX_DIM].astype(BF16)
            rel = jnp.maximum(_dot_nt(iq, ik), 0.0)
            score = score + iw[:, h:h + 1] * rel
        qpos = i * BLOCK + lax.broadcasted_iota(jnp.int32, (BLOCK, kw), 0)
        kpos = lax.broadcasted_iota(jnp.int32, (BLOCK, kw), 1)
        dist = qpos - kpos
        causal = dist >= 0
        key = _order_key(jnp.where(causal, score, NEG))

        thr = _kth_largest_key(key, float(n_top))
        gt = key > thr
        eq = key == thr
        need = float(n_top) - jnp.sum(jnp.where(gt, 1.0, 0.0), axis=-1, keepdims=True)
        row = lax.broadcasted_iota(jnp.int32, (BLOCK, BLOCK), 0)
        col = lax.broadcasted_iota(jnp.int32, (BLOCK, BLOCK), 1)
        upper = jnp.where(row <= col, 1.0, 0.0).astype(BF16)
        eqf = jnp.where(eq, 1.0, 0.0).astype(BF16)
        carry = jnp.zeros((BLOCK, 1), F32)
        rank_tiles = []
        for j in range(kw // BLOCK):
            rank = carry + jnp.dot(eqf[:, j * BLOCK:(j + 1) * BLOCK], upper, preferred_element_type=F32)
            rank_tiles.append(rank)
            carry = rank[:, BLOCK - 1:BLOCK]
        rank = jnp.concatenate(rank_tiles, axis=1)
        mask = (gt | (eq & (rank <= need))) & causal
        maskadd = jnp.where(mask, 0.0, NEG)

        k = k_ref[0, :kw, :].astype(BF16)
        v = v_ref[0, :kw, :].astype(BF16)
        distf = dist.astype(F32)
        for h in range(D_HEADS):
            q = q_ref[0, :, h * HEAD_DIM:(h + 1) * HEAD_DIM].astype(BF16)
            s2 = _dot_nt(q, k) * (ATTN_SCALE * LOG2E) - (slope_ref[h] * LOG2E) * distf + maskadd
            o_ref[0, :, h * HEAD_DIM:(h + 1) * HEAD_DIM] = _softmax2_pv(s2, v).astype(o_ref.dtype)

    _for_causal_class(i, nb, body)


def _indexed_sparse_attention(y3, ik, iw, slopes_all):
    B, L, _ = y3.shape
    nb = L // BLOCK
    n_top = min(D_TOPK, L // 4)
    slopes = slopes_all[np.array(D_SLOPE_IDX)]
    qw, iqw = D_HEADS * HEAD_DIM, D_IDX_HEADS * D_IDX_DIM
    return pl.pallas_call(
        functools.partial(_dsa_kernel, n_top=n_top),
        grid=(B, nb),
        in_specs=[pl.BlockSpec(memory_space=pltpu.SMEM),
                  pl.BlockSpec((1, BLOCK, iqw), lambda b, i: (b, i, CB_DIQ * LANES // iqw)),
                  pl.BlockSpec((1, L, D_IDX_DIM), lambda b, i: (b, 0, 0)),
                  pl.BlockSpec((1, BLOCK, D_IDX_HEADS), lambda b, i: (b, i, 0)),
                  pl.BlockSpec((1, BLOCK, qw), lambda b, i: (b, i, CB_DQ * LANES // qw)),
                  pl.BlockSpec((1, L, HEAD_DIM), lambda b, i: (b, 0, CB_DK)),
                  pl.BlockSpec((1, L, HEAD_DIM), lambda b, i: (b, 0, CB_DV))],
        out_specs=pl.BlockSpec((1, BLOCK, qw), lambda b, i: (b, i, 0)),
        out_shape=jax.ShapeDtypeStruct((B, L, qw), BF16),
        compiler_params=_cparams(("parallel", "arbitrary")),
        name="indexed_sparse_attention",
    )(slopes, y3, ik, iw, y3, y3, y3)


def _compress_kernel(x_ref, pos_ref, w1_ref, w2_ref, o_ref):
    x = x_ref[0, 0].astype(F32)
    nxt = pltpu.roll(x, x.shape[0] - 1, 0)
    blk = jnp.concatenate([x, nxt], axis=1) + pos_ref[0]
    h = jnp.dot(blk.astype(BF16), w1_ref[0], preferred_element_type=F32)
    h = jax.nn.gelu(h, approximate=True)
    o_ref[0, 0] = jnp.dot(h.astype(BF16), w2_ref[0], preferred_element_type=F32).astype(o_ref.dtype)


def _compress(xc, pos, w1, w2):
    B, _, n_chunk, cw = xc.shape
    G = B_KV_GROUPS
    return pl.pallas_call(
        _compress_kernel,
        grid=(2 * G, B),
        in_specs=[pl.BlockSpec((1, 1, n_chunk, cw), lambda a, b: (b, a, 0, 0)),
                  pl.BlockSpec((1, 1, 2 * cw), lambda a, b: (a // G, 0, 0)),
                  pl.BlockSpec((1, 2 * cw, B_CMP_HIDDEN), lambda a, b: (a // G, 0, 0)),
                  pl.BlockSpec((1, B_CMP_HIDDEN, HEAD_DIM), lambda a, b: (a // G, 0, 0))],
        out_specs=pl.BlockSpec((1, 1, n_chunk, HEAD_DIM), lambda a, b: (b, a, 0, 0)),
        out_shape=jax.ShapeDtypeStruct((B, 2 * G, n_chunk, HEAD_DIM), BF16),
        compiler_params=_cparams(("arbitrary", "arbitrary")),
        name="nsa_compress",
    )(xc, pos, w1, w2)


def _nsa_cmp_kernel(slope_ref, *refs, n_cmp, n_slc):
    G, R = B_KV_GROUPS, B_GROUP_SIZE
    q_refs, (ckv_ref, o_ref, sel_ref) = refs[:G], refs[G:]
    i = pl.program_id(1)
    nc = ckv_ref.shape[2]
    t = i * BLOCK + lax.broadcasted_iota(jnp.int32, (BLOCK, nc), 0)
    n = lax.broadcasted_iota(jnp.int32, (BLOCK, nc), 1)
    dist_c = t - (n * B_CMP_STRIDE + B_CMP_LEN - 1)
    mask = (dist_c >= 0) & (n < n_cmp)
    distf = dist_c.astype(F32)

    nn = lax.broadcasted_iota(jnp.int32, (nc, LANES), 0)
    jj = lax.broadcasted_iota(jnp.int32, (nc, LANES), 1)
    off = nn - B_SEL_RATIO * jj + 1
    w = jnp.where((off == 0) | (off == B_SEL_RATIO), 1.0, jnp.where((off > 0) & (off < B_SEL_RATIO), 2.0, 0.0))
    w = jnp.where((nn < n_cmp) & (jj < n_slc), w, 0.0).astype(F32)
    tq = i * BLOCK + lax.broadcasted_iota(jnp.int32, (BLOCK, LANES), 0)
    j = lax.broadcasted_iota(jnp.int32, (BLOCK, LANES), 1)
    cur = tq >> int(np.log2(B_SEL_LEN))
    forced = (j == 0) | (j == cur) | (j == cur - 1)
    jf = j.astype(F32)

    for g in range(G):
        kc = ckv_ref[0, g]
        vc = ckv_ref[0, G + g]
        imp = jnp.zeros((BLOCK, nc), F32)
        for r in range(R):
            q = q_refs[g][0, :, r * HEAD_DIM:(r + 1) * HEAD_DIM].astype(BF16)
            s = _dot_nt(q, kc) * ATTN_SCALE
            s = jnp.where(mask, s - slope_ref[g * R + r] * distf, NEG)
            m = jnp.max(s, axis=-1, keepdims=True)
            e = jnp.where(mask, jnp.exp(s - m), 0.0)
            p = e / jnp.maximum(jnp.sum(e, axis=-1, keepdims=True), 1e-30)
            o_ref[0, :, (g * R + r) * HEAD_DIM:(g * R + r + 1) * HEAD_DIM] = jnp.dot(
                p.astype(BF16), vc, preferred_element_type=F32)
            imp = imp + p
        p_slc = jnp.dot(imp, w, preferred_element_type=F32, precision=lax.Precision.HIGHEST)
        score = jnp.where(forced, 1e9, jnp.where(j <= cur, p_slc, -1e9))
        score = jnp.where(j < n_slc, score, -3e38)
        sel = jnp.zeros((BLOCK, LANES), F32)
        for _ in range(min(B_N_SEL, n_slc)):
            m = jnp.max(score, axis=-1, keepdims=True)
            first = jnp.min(jnp.where(score == m, jf, float(LANES)), axis=-1, keepdims=True)
            hit = jf == first
            sel = jnp.where(hit, 1.0, sel)
            score = jnp.where(hit, -3e38, score)
        sel_ref[0, g] = sel.astype(sel_ref.dtype)


def _nsa_q_specs():
    return [pl.BlockSpec((1, BLOCK, B_GROUP_SIZE * HEAD_DIM), lambda b, i, g=g: (b, i, CB_BQ // B_GROUP_SIZE + g))
            for g in range(B_KV_GROUPS)]


def _nsa_cmp(y3, cmp_kv, slopes, n_cmp, n_slc):
    B, L, _ = y3.shape
    G = B_KV_GROUPS
    nc = cmp_kv.shape[2]
    return pl.pallas_call(
        functools.partial(_nsa_cmp_kernel, n_cmp=n_cmp, n_slc=n_slc),
        grid=(B, L // BLOCK),
        in_specs=[pl.BlockSpec(memory_space=pltpu.SMEM)] + _nsa_q_specs() +
                 [pl.BlockSpec((1, 2 * G, nc, HEAD_DIM), lambda b, i: (b, 0, 0, 0))],
        out_specs=[pl.BlockSpec((1, BLOCK, B_HEADS * HEAD_DIM), lambda b, i: (b, i, 0)),
                   pl.BlockSpec((1, G, BLOCK, LANES), lambda b, i: (b, 0, i, 0))],
        out_shape=[jax.ShapeDtypeStruct((B, L, B_HEADS * HEAD_DIM), F32),
                   jax.ShapeDtypeStruct((B, G, L, LANES), BF16)],
        compiler_params=_cparams(("parallel", "arbitrary")),
        name="nsa_compressed_attention",
    )(slopes, *([y3] * G), cmp_kv)


def _nsa_slc_kernel(slope_ref, *refs):
    G, R = B_KV_GROUPS, B_GROUP_SIZE
    q_refs, k_refs, v_refs, (sel_ref, o_ref) = refs[:G], refs[G:2 * G], refs[2 * G:3 * G], refs[3 * G:]
    i = pl.program_id(1)
    nb = k_refs[0].shape[1] // BLOCK

    def body(kw):
        jj = lax.broadcasted_iota(jnp.int32, (LANES, kw), 0)
        ss = lax.broadcasted_iota(jnp.int32, (LANES, kw), 1)
        expand = jnp.where((ss >> int(np.log2(B_SEL_LEN))) == jj, 1.0, 0.0).astype(BF16)
        qpos = i * BLOCK + lax.broadcasted_iota(jnp.int32, (BLOCK, kw), 0)
        kpos = lax.broadcasted_iota(jnp.int32, (BLOCK, kw), 1)
        dist = qpos - kpos
        distf = dist.astype(F32)
        for g in range(G):
            picked = jnp.dot(sel_ref[0, g], expand, preferred_element_type=F32) > 0.5
            maskadd = jnp.where(picked & (dist >= 0), 0.0, NEG)
            k = k_refs[g][0, :kw, :].astype(BF16)
            v = v_refs[g][0, :kw, :].astype(BF16)
            for r in range(R):
                q = q_refs[g][0, :, r * HEAD_DIM:(r + 1) * HEAD_DIM].astype(BF16)
                s2 = _dot_nt(q, k) * (ATTN_SCALE * LOG2E) - (slope_ref[g * R + r] * LOG2E) * distf + maskadd
                o_ref[0, :, (g * R + r) * HEAD_DIM:(g * R + r + 1) * HEAD_DIM] = _softmax2_pv(s2, v)

    _for_causal_class(i, nb, body)


def _nsa_slc(y3, sel, slopes):
    B, L, _ = y3.shape
    G = B_KV_GROUPS
    kv_specs = [pl.BlockSpec((1, L, HEAD_DIM), lambda b, i, c=CB_BKV + 4 + kv * G + g: (b, 0, c))
                for kv in range(2) for g in range(G)]
    return pl.pallas_call(
        _nsa_slc_kernel,
        grid=(B, L // BLOCK),
        in_specs=[pl.BlockSpec(memory_space=pltpu.SMEM)] + _nsa_q_specs() + kv_specs +
                 [pl.BlockSpec((1, G, BLOCK, LANES), lambda b, i: (b, 0, i, 0))],
        out_specs=pl.BlockSpec((1, BLOCK, B_HEADS * HEAD_DIM), lambda b, i: (b, i, 0)),
        out_shape=jax.ShapeDtypeStruct((B, L, B_HEADS * HEAD_DIM), F32),
        compiler_params=_cparams(("parallel", "arbitrary")),
        name="nsa_selected_attention",
    )(slopes, *([y3] * (3 * G)), sel)


def _b_gate_kernel(gl_ref, oc_ref, os_ref, ow_ref, out_ref):
    gate = jax.nn.sigmoid(gl_ref[...])
    for h in range(B_HEADS):
        sl = slice(h * HEAD_DIM, (h + 1) * HEAD_DIM)
        out = (gate[:, h:h + 1] * oc_ref[:, sl] + gate[:, B_HEADS + h:B_HEADS + h + 1] * os_ref[:, sl]
               + gate[:, 2 * B_HEADS + h:2 * B_HEADS + h + 1] * ow_ref[:, sl])
        out_ref[:, sl] = out.astype(out_ref.dtype)


def _b_gate(gl, o_cmp, o_slc, o_win):
    M, C = o_cmp.shape
    tm = _tile(M, 512)
    spec = pl.BlockSpec((tm, C), lambda i: (i, 0))
    return pl.pallas_call(
        _b_gate_kernel,
        grid=(M // tm,),
        in_specs=[pl.BlockSpec((tm, gl.shape[1]), lambda i: (i, 0)), spec, spec, spec],
        out_specs=spec,
        out_shape=jax.ShapeDtypeStruct((M, C), BF16),
        compiler_params=_cparams(("parallel",)),
        name="nsa_branch_gate",
    )(gl, o_cmp, o_slc, o_win)


def _native_sparse_attention(y3, gate_logits, cmp_w1, cmp_w2, cmp_pos, slopes_all):
    B, L, _ = y3.shape
    G, R = B_KV_GROUPS, B_GROUP_SIZE
    slopes = slopes_all[np.array(B_SLOPE_IDX)]
    n_chunk = L // B_CMP_STRIDE
    n_cmp = n_chunk - B_CMP_LEN // B_CMP_STRIDE + 1
    n_slc = L // B_SEL_LEN
    assert B_CMP_LEN == 2 * B_CMP_STRIDE and n_chunk % 8 == 0 and n_slc <= LANES

    xc = y3[:, :, CB_BKV * LANES:(CB_BKV + 2 * G) * LANES].reshape(B, L, 2 * G, HEAD_DIM)
    xc = xc.transpose(0, 2, 1, 3).reshape(B, 2 * G, n_chunk, B_CMP_STRIDE * HEAD_DIM)
    cmp_kv = _compress(xc, cmp_pos.reshape(2, 1, B_CMP_LEN * HEAD_DIM), cmp_w1.astype(BF16), cmp_w2.astype(BF16))

    o_cmp, sel = _nsa_cmp(y3, cmp_kv, slopes, n_cmp, n_slc)
    o_slc = _nsa_slc(y3, sel, slopes)
    w_steps = B_WINDOW - 1
    (o_win,) = _banded_attention(y3, y3, y3, slopes, n_groups=G, hp=R, shared_kv=True,
                                 q_blk=lambda g: CB_BQ // R + g,
                                 k_blk=lambda g: CB_BKV + 8 + g,
                                 v_blk=lambda g: CB_BKV + 10 + g,
                                 window=w_steps, n_prev=-(-w_steps // BLOCK), step=1, with_lse=False)
    C = B_HEADS * HEAD_DIM
    return _b_gate(gate_logits, o_cmp.reshape(B * L, C), o_slc.reshape(B * L, C), o_win.reshape(B * L, C))


def _gated_merge_kernel(x_ref, *refs):
    nbr = N_BRANCH
    wg, o, w, bias = refs[:nbr], refs[nbr:2 * nbr], refs[2 * nbr:3 * nbr], refs[3 * nbr:4 * nbr]
    out_ref, acc_ref = refs[4 * nbr], refs[4 * nbr + 1]
    k = pl.program_id(2)

    @pl.when(k == 0)
    def _():
        acc_ref[...] = jnp.zeros_like(acc_ref)

    x = x_ref[...]
    for c in range(nbr):
        acc_ref[c] += jnp.dot(x, wg[c][...], preferred_element_type=F32)

    @pl.when(k == pl.num_programs(2) - 1)
    def _():
        merged = None
        for c in range(nbr):
            term = jax.nn.sigmoid(acc_ref[c] + bias[c][...]) * jnp.dot(o[c][...], w[c][...],
                                                                      preferred_element_type=F32)
            merged = term if merged is None else merged + term
        out_ref[...] = merged.astype(out_ref.dtype)


def _gated_merge(xb, w_gate, b_gate, branch_outs, branch_ws):
    M, K = xb.shape
    D = branch_ws[0].shape[1]
    tm, tn, tk = _tile(M, 1024), _tile(D, 512), _tile(K, 1024)
    nj = D // tn
    wg_specs = [pl.BlockSpec((tk, tn), lambda i, j, k, c=c: (k, c * nj + j)) for c in range(N_BRANCH)]
    o_specs = [pl.BlockSpec((tm, o.shape[1]), lambda i, j, k: (i, 0)) for o in branch_outs]
    w_specs = [pl.BlockSpec((w.shape[0], tn), lambda i, j, k: (0, j)) for w in branch_ws]
    b_specs = [pl.BlockSpec((1, tn), lambda i, j, k, c=c: (0, c * nj + j)) for c in range(N_BRANCH)]
    return pl.pallas_call(
        _gated_merge_kernel,
        grid=(M // tm, nj, K // tk),
        in_specs=[pl.BlockSpec((tm, tk), lambda i, j, k: (i, k))] + wg_specs + o_specs + w_specs + b_specs,
        out_specs=pl.BlockSpec((tm, tn), lambda i, j, k: (i, j)),
        out_shape=jax.ShapeDtypeStruct((M, D), BF16),
        scratch_shapes=[pltpu.VMEM((N_BRANCH, tm, tn), F32)],
        compiler_params=_cparams(("parallel", "parallel", "arbitrary")),
        name="gated_merge",
    )(xb, *([w_gate] * N_BRANCH), *branch_outs, *branch_ws, *([b_gate.reshape(1, -1)] * N_BRANCH))


def _attn_weight(w_in):
    D = w_in.shape[0]
    sizes = (A_HEADS * HEAD_DIM,) * 3 + (B_HEADS * HEAD_DIM, 3 * 2 * B_KV_GROUPS * HEAD_DIM, 3 * B_HEADS) + \
            (C_HEADS * HEAD_DIM,) * 3 + (C_HEADS,) + (D_HEADS * HEAD_DIM, HEAD_DIM, HEAD_DIM) + \
            (D_IDX_HEADS * D_IDX_DIM, D_IDX_DIM, D_IDX_HEADS)
    offs = np.concatenate([[0], np.cumsum(sizes)])
    (a_q, a_k, a_v, b_q, b_kv, b_g, c_q, c_k, c_v, c_f, d_q, d_k, d_v, d_iq, d_ik, d_iw) = [
        w_in[:, offs[n]:offs[n + 1]] for n in range(len(sizes))]
    misc = jnp.zeros((D, 2 * LANES), w_in.dtype)
    misc = misc.at[:, MISC_BG:MISC_BG + 3 * B_HEADS].set(b_g)
    misc = misc.at[:, MISC_CF:MISC_CF + C_HEADS].set(c_f)
    misc = misc.at[:, MISC_IK:MISC_IK + D_IDX_DIM].set(d_ik)
    misc = misc.at[:, MISC_IW:MISC_IW + D_IDX_HEADS].set(d_iw)
    parts = [a_q, a_k, a_v, b_q, b_kv, c_q, c_k, c_v, d_q, d_iq, d_k, d_v, misc]
    w = jnp.concatenate([p.astype(BF16) for p in parts], axis=1)
    assert w.shape[1] == N_CB * LANES
    return w, int(offs[-1])


def _hybrid_mixer(xb, B, L, w_in, b_forget, b_gate, cmp_w1, cmp_w2, cmp_pos, w_branch, w_out, tiles):
    M, D = xb.shape
    slopes_all = jnp.exp2(-8.0 * jnp.arange(1, N_ALIBI + 1, dtype=F32) / N_ALIBI)
    w_attn, gate_off = _attn_weight(w_in)
    y3 = _matmul(xb, w_attn, BF16, tiles).reshape(B, L, N_CB * LANES)
    misc = _matmul(xb, w_attn[:, CB_MISC * LANES:(CB_MISC + 1) * LANES], F32)

    bias = jnp.zeros((1, LANES), F32).at[0, MISC_CF:MISC_CF + C_HEADS].set(b_forget)
    c = _logsig_cumsum(misc.reshape(B, L, LANES), bias)[:, :, MISC_CF:MISC_CF + C_HEADS].transpose(0, 2, 1)

    o_a = _dilated_attention(y3, slopes_all)
    o_b = _native_sparse_attention(y3, misc[:, MISC_BG:MISC_BG + 3 * B_HEADS], cmp_w1, cmp_w2, cmp_pos, slopes_all)
    o_c = _forgetting_attention(y3, c[..., None], c[:, :, None, :])
    o_d = _indexed_sparse_attention(y3, misc[:, MISC_IK:MISC_IK + D_IDX_DIM].reshape(B, L, D_IDX_DIM),
                                    misc[:, MISC_IW:MISC_IW + D_IDX_HEADS].reshape(B, L, D_IDX_HEADS), slopes_all)

    sizes = (A_HEADS_PER_PAIR * HEAD_DIM, B_HEADS * HEAD_DIM, C_HEADS * HEAD_DIM, D_HEADS * HEAD_DIM)
    offs = np.concatenate([[0], np.cumsum(sizes)])
    ws = [w_branch[offs[n]:offs[n + 1]].astype(BF16) for n in range(N_BRANCH)]
    outs = [o_a, o_b, o_c.reshape(M, -1), o_d.reshape(M, -1)]
    merged = _gated_merge(xb, w_in[:, gate_off:].astype(BF16), b_gate, outs, ws)
    return _matmul(merged, w_out.astype(BF16), F32, tiles)


def kernel(x, ln_g, ln_b, ffn1_w_gate, ffn1_w_up, ffn1_w_down, w_in, b_forget, b_gate, cmp_w1, cmp_w2, cmp_pos,
           w_branch, w_out, ffn2_w_gate, ffn2_w_up, ffn2_w_down):
    B, L, D = x.shape
    assert L % BLOCK == 0 and D % LANES == 0
    xf = x.reshape(B * L, D)
    xb = xf.astype(BF16)
    tilings = ((1024, 1024, 2048), (1024, 1024, 2048))

    def ffn(xf, xb, wg, wu, wd, g, b, tiles):
        h = _ffn_up(xb, wg.astype(BF16), wu.astype(BF16))
        y = _matmul(h, wd.astype(BF16), F32, tiles)
        return _add_ln(xf, y, g, b, 0.5)

    for l in range(ln_g.shape[0]):
        xf, xb = ffn(xf, xb, ffn1_w_gate[l], ffn1_w_up[l], ffn1_w_down[l], ln_g[l, 0], ln_b[l, 0], tilings[l])
        y = _hybrid_mixer(xb, B, L, w_in[l], b_forget[l], b_gate[l], cmp_w1[l], cmp_w2[l], cmp_pos[l],
                          w_branch[l], w_out[l], tilings[l])
        xf, xb = _add_ln(xf, y, ln_g[l, 1], ln_b[l, 1], 1.0)
        xf, xb = ffn(xf, xb, ffn2_w_gate[l], ffn2_w_up[l], ffn2_w_down[l], ln_g[l, 2], ln_b[l, 2], tilings[l])
    return xf.reshape(B, L, D)
```

```python
import functools

import numpy as np
import jax
import jax.numpy as jnp
from jax import lax
from jax.experimental import pallas as pl
from jax.experimental.pallas import tpu as pltpu

F32 = jnp.float32
BF16 = jnp.bfloat16

HEAD_DIM = 128
BLOCK = 128
LANES = 128
NEG = -1e30
ATTN_SCALE = HEAD_DIM ** -0.5
LOG2E = 1.4426950408889634
LN2 = 0.6931471805599453
DEPTH = 2
ALPHA = (2 * DEPTH) ** 0.25
LN_EPS = 1e-5

A_PAIRS = ((128, 1), (512, 4), (2048, 16))
A_HEADS_PER_PAIR = 4
A_HEADS = A_HEADS_PER_PAIR * len(A_PAIRS)
B_HEADS = 8
B_KV_GROUPS = 2
B_GROUP_SIZE = B_HEADS // B_KV_GROUPS
B_CMP_LEN = 32
B_CMP_STRIDE = 16
B_SEL_LEN = 64
B_SEL_RATIO = B_SEL_LEN // B_CMP_STRIDE
B_N_SEL = 8
B_WINDOW = 512
B_CMP_HIDDEN = 512
C_HEADS = 8
D_HEADS = 8
D_IDX_HEADS = 8
D_IDX_DIM = 64
D_TOPK = 256
N_BRANCH = 4
N_ALIBI = A_HEADS + B_HEADS + D_HEADS
A_SLOPE_IDX = (0, 1, 2, 3, 12, 13, 14, 15, 24, 25, 26, 27)
B_SLOPE_IDX = (4, 5, 6, 7, 8, 9, 10, 11)
D_SLOPE_IDX = (16, 17, 18, 19, 20, 21, 22, 23)

CB_A = 0
CB_BQ = 36
CB_BKV = 44
CB_CQ, CB_CK, CB_CV = 56, 64, 72
CB_DQ = 80
CB_DIQ = 88
CB_DK, CB_DV = 92, 93
CB_MISC = 94
N_CB = 96
MISC_BG, MISC_CF, MISC_IK, MISC_IW = 0, 24, 32, 96

CAUSAL_CLASSES = 8
VMEM_LIMIT = 56 * 1024 * 1024


def _cparams(sem):
    return pltpu.CompilerParams(dimension_semantics=sem, vmem_limit_bytes=VMEM_LIMIT)


def _tile(dim, pref):
    return pref if dim % pref == 0 else dim


def _dot_nt(a, b):
    return lax.dot_general(a, b, (((1,), (1,)), ((), ())), preferred_element_type=F32)


def _softmax2_pv(s2, v):
    m = jnp.max(s2, axis=-1, keepdims=True)
    e = jnp.exp2(s2 - m)
    den = jnp.maximum(jnp.sum(e, axis=-1, keepdims=True), 1e-30)
    return jnp.dot(e.astype(BF16), v, preferred_element_type=F32) / den


def _for_causal_class(i, nb, body):
    n_cls = CAUSAL_CLASSES if nb % CAUSAL_CLASSES == 0 else 1
    per = nb // n_cls
    for c in range(n_cls):
        pl.when((i >= c * per) & (i < (c + 1) * per))(functools.partial(body, (c + 1) * per * BLOCK))


def _sigmoid(z):
    return 0.5 * jnp.tanh(0.5 * z) + 0.5


def _k_steps(nk, first, middle, last):
    k = pl.program_id(2)
    if nk == 1:
        last(False)
        return
    pl.when(k == 0)(first)
    if nk > 2:
        pl.when((k > 0) & (k < nk - 1))(middle)
    pl.when(k == nk - 1)(functools.partial(last, True))


def _mm_kernel(x_ref, w_ref, o_ref, *acc, nk):
    def prod():
        return jnp.dot(x_ref[...], w_ref[...], preferred_element_type=F32)

    def first():
        acc[0][...] = prod()

    def middle():
        acc[0][...] += prod()

    def last(has_acc):
        o_ref[...] = ((acc[0][...] + prod()) if has_acc else prod()).astype(o_ref.dtype)

    _k_steps(nk, first, middle, last)


def _matmul(x, w, out_dtype, tiles=(1024, 1024, 4096)):
    M, K = x.shape
    N = w.shape[1]
    tm, tn, tk = _tile(M, tiles[0]), _tile(N, tiles[1]), _tile(K, tiles[2])
    nk = K // tk
    return pl.pallas_call(
        functools.partial(_mm_kernel, nk=nk),
        grid=(M // tm, N // tn, nk),
        in_specs=[pl.BlockSpec((tm, tk), lambda i, j, k: (i, k)),
                  pl.BlockSpec((tk, tn), lambda i, j, k: (k, j))],
        out_specs=pl.BlockSpec((tm, tn), lambda i, j, k: (i, j)),
        out_shape=jax.ShapeDtypeStruct((M, N), out_dtype),
        scratch_shapes=[pltpu.VMEM((tm, tn), F32)] if nk > 1 else [],
        compiler_params=_cparams(("parallel", "parallel", "arbitrary")),
        name="matmul",
    )(x, w)


def _ffn_up_kernel(x_ref, wg_ref, wu_ref, o_ref, *acc, nk):
    def prods():
        x = x_ref[...]
        return (jnp.dot(x, wg_ref[...], preferred_element_type=F32),
                jnp.dot(x, wu_ref[...], preferred_element_type=F32))

    def first():
        acc[0][...], acc[1][...] = prods()

    def middle():
        g, u = prods()
        acc[0][...] += g
        acc[1][...] += u

    def last(has_acc):
        g, u = prods()
        if has_acc:
            g, u = acc[0][...] + g, acc[1][...] + u
        o_ref[...] = (g * _sigmoid(g) * u).astype(o_ref.dtype)

    _k_steps(nk, first, middle, last)


def _ffn_up(x, wg, wu, tiles=(1024, 1024, 2048)):
    M, K = x.shape
    N = wg.shape[1]
    tm, tn, tk = _tile(M, tiles[0]), _tile(N, tiles[1]), _tile(K, tiles[2])
    nk = K // tk
    return pl.pallas_call(
        functools.partial(_ffn_up_kernel, nk=nk),
        grid=(M // tm, N // tn, nk),
        in_specs=[pl.BlockSpec((tm, tk), lambda i, j, k: (i, k)),
                  pl.BlockSpec((tk, tn), lambda i, j, k: (k, j)),
                  pl.BlockSpec((tk, tn), lambda i, j, k: (k, j))],
        out_specs=pl.BlockSpec((tm, tn), lambda i, j, k: (i, j)),
        out_shape=jax.ShapeDtypeStruct((M, N), BF16),
        scratch_shapes=[pltpu.VMEM((tm, tn), F32), pltpu.VMEM((tm, tn), F32)] if nk > 1 else [],
        compiler_params=_cparams(("parallel", "parallel", "arbitrary")),
        name="ffn_up",
    )(x, wg, wu)


def _add_ln_kernel(x_ref, y_ref, g_ref, b_ref, o_ref, ob_ref, *, cy):
    z = ALPHA * x_ref[...] + cy * y_ref[...]
    mu = jnp.mean(z, axis=-1, keepdims=True)
    zc = z - mu
    var = jnp.mean(zc * zc, axis=-1, keepdims=True)
    out = zc * lax.rsqrt(var + LN_EPS) * g_ref[...] + b_ref[...]
    o_ref[...] = out
    ob_ref[...] = out.astype(BF16)


def _add_ln(x, y, g, b, cy):
    M, D = x.shape
    tm = _tile(M, 256)
    row = pl.BlockSpec((tm, D), lambda i: (i, 0))
    vec = pl.BlockSpec((1, D), lambda i: (0, 0))
    return pl.pallas_call(
        functools.partial(_add_ln_kernel, cy=cy),
        grid=(M // tm,),
        in_specs=[row, row, vec, vec],
        out_specs=[row, row],
        out_shape=[jax.ShapeDtypeStruct((M, D), F32), jax.ShapeDtypeStruct((M, D), BF16)],
        compiler_params=_cparams(("parallel",)),
        name="add_ln",
    )(x, y, g.reshape(1, D), b.reshape(1, D))


def _banded_kernel(slope_ref, q_ref, k_ref, v_ref, *out_refs, window, n_prev, step, nb, hp, shared_kv):
    g = pl.program_id(1)
    i = pl.program_id(2)
    kwb = min(n_prev + 1, nb)
    kw = kwb * BLOCK
    start = pl.multiple_of(jnp.maximum(i - n_prev, 0) * BLOCK, BLOCK)
    qpos = i * BLOCK + lax.broadcasted_iota(jnp.int32, (BLOCK, kw), 0)
    kpos = start + lax.broadcasted_iota(jnp.int32, (BLOCK, kw), 1)
    dist = qpos - kpos
    maskadd = jnp.where((dist >= 0) & (dist <= window), 0.0, NEG)
    distf = (step * dist).astype(F32)
    for h in range(hp):
        cs = slice(h * HEAD_DIM, (h + 1) * HEAD_DIM)
        ks = slice(0, HEAD_DIM) if shared_kv else cs
        q = q_ref[0, :, cs].astype(BF16)
        kwin = k_ref[0, pl.ds(start, kw), ks].astype(BF16)
        vwin = v_ref[0, pl.ds(start, kw), ks].astype(BF16)
        s = _dot_nt(q, kwin) * ATTN_SCALE
        s = s - slope_ref[g * hp + h] * distf + maskadd
        m = jnp.max(s, axis=-1, keepdims=True)
        e = jnp.exp(s - m)
        den = jnp.maximum(jnp.sum(e, axis=-1, keepdims=True), 1e-30)
        out_refs[0][0, :, cs] = jnp.dot(e.astype(BF16), vwin, preferred_element_type=F32) / den
        if len(out_refs) > 1:
            out_refs[1][0, :, cs] = jnp.broadcast_to(m + jnp.log(den), (BLOCK, LANES))


def _banded_attention(qa, ka, va, slopes, *, n_groups, hp, shared_kv, q_blk, k_blk, v_blk, window, n_prev, step,
                      with_lse):
    B, N, _ = qa.shape
    nb = N // BLOCK
    kvw = HEAD_DIM if shared_kv else hp * HEAD_DIM
    qspec = pl.BlockSpec((1, BLOCK, hp * HEAD_DIM), lambda b, g, i: (b, i, q_blk(g)))
    kspec = pl.BlockSpec((1, N, kvw), lambda b, g, i: (b, 0, k_blk(g)))
    vspec = pl.BlockSpec((1, N, kvw), lambda b, g, i: (b, 0, v_blk(g)))
    ospec = pl.BlockSpec((1, BLOCK, hp * HEAD_DIM), lambda b, g, i: (b, i, g))
    oshape = jax.ShapeDtypeStruct((B, N, n_groups * hp * HEAD_DIM), F32)
    return pl.pallas_call(
        functools.partial(_banded_kernel, window=window, n_prev=n_prev, step=step, nb=nb, hp=hp, shared_kv=shared_kv),
        grid=(B, n_groups, nb),
        in_specs=[pl.BlockSpec(memory_space=pltpu.SMEM), qspec, kspec, vspec],
        out_specs=[ospec, ospec] if with_lse else [ospec],
        out_shape=[oshape, oshape] if with_lse else [oshape],
        compiler_params=_cparams(("parallel", "parallel", "arbitrary")),
        name="banded_attention",
    )(slopes, qa, ka, va)


def _a_combine_kernel(o0, o1, o2, l0, l1, l2, out_ref):
    a, b, c = l0[...], l1[...], l2[...]
    m = jnp.maximum(jnp.maximum(a, b), c)
    ea, eb, ec = jnp.exp(a - m), jnp.exp(b - m), jnp.exp(c - m)
    tot = ea + eb + ec
    out = (ea / tot) * o0[...] + (eb / tot) * o1[...] + (ec / tot) * o2[...]
    out_ref[...] = out.astype(out_ref.dtype)


def _a_combine(outs, lses):
    M, C = outs[0].shape
    tm = _tile(M, 512)
    spec = pl.BlockSpec((tm, C), lambda i: (i, 0))
    return pl.pallas_call(
        _a_combine_kernel,
        grid=(M // tm,),
        in_specs=[spec] * 6,
        out_specs=spec,
        out_shape=jax.ShapeDtypeStruct((M, C), BF16),
        compiler_params=_cparams(("parallel",)),
        name="a_combine",
    )(*outs, *lses)


def _dilated_attention(y3, slopes_all):
    B, L, _ = y3.shape
    slopes = slopes_all[np.array(A_SLOPE_IDX)]
    hpp = A_HEADS_PER_PAIR
    gw = 3 * hpp * HEAD_DIM
    outs, lses = [], []
    for g, (window, dil) in enumerate(A_PAIRS):
        n = L // dil
        assert n % BLOCK == 0
        steps = window // dil
        if dil == 1:
            ya, base = y3, CB_A // hpp + g
            blk = lambda part: (lambda r: base + part * (A_HEADS // hpp))
        else:
            cols = [y3[:, :, (CB_A + p * A_HEADS + g * hpp) * LANES:(CB_A + p * A_HEADS + (g + 1) * hpp) * LANES]
                    for p in range(3)]
            ya = jnp.concatenate(cols, axis=-1).reshape(B, n, dil * gw)
            blk = lambda part: (lambda r: r * 3 + part)
        o, lse = _banded_attention(ya, ya, ya, jnp.tile(slopes[g * hpp:(g + 1) * hpp], dil),
                                   n_groups=dil, hp=hpp, shared_kv=False, q_blk=blk(0), k_blk=blk(1), v_blk=blk(2),
                                   window=steps, n_prev=-(-steps // BLOCK), step=dil, with_lse=True)
        outs.append(o.reshape(B * L, hpp * HEAD_DIM))
        lses.append(lse.reshape(B * L, hpp * HEAD_DIM))
    return _a_combine(outs, lses)


def _logsig_cumsum_kernel(z_ref, bias_ref, o_ref):
    L = z_ref.shape[1]
    row = lax.broadcasted_iota(jnp.int32, (BLOCK, BLOCK), 0)
    col = lax.broadcasted_iota(jnp.int32, (BLOCK, BLOCK), 1)
    tri = jnp.where(row >= col, 1.0, 0.0).astype(F32)
    carry = jnp.zeros((1, LANES), F32)
    for j in range(L // BLOCK):
        z = z_ref[0, j * BLOCK:(j + 1) * BLOCK, :] + bias_ref[...]
        ls = jnp.minimum(z, 0.0) - jnp.log(1.0 + jnp.exp(-jnp.abs(z)))
        c = jnp.dot(tri, ls, preferred_element_type=F32, precision=lax.Precision.HIGHEST) + carry
        o_ref[0, j * BLOCK:(j + 1) * BLOCK, :] = c
        carry = c[BLOCK - 1:BLOCK, :]


def _logsig_cumsum(z, bias):
    B, L, _ = z.shape
    spec = pl.BlockSpec((1, L, LANES), lambda b: (b, 0, 0))
    return pl.pallas_call(
        _logsig_cumsum_kernel,
        grid=(B,),
        in_specs=[spec, pl.BlockSpec((1, LANES), lambda b: (0, 0))],
        out_specs=spec,
        out_shape=jax.ShapeDtypeStruct((B, L, LANES), F32),
        compiler_params=_cparams(("parallel",)),
        name="logsig_cumsum",
    )(z, bias)


def _fox_kernel(q_ref, k_ref, v_ref, cq_ref, ck_ref, o_ref):
    i = pl.program_id(1)
    nb = k_ref.shape[1] // BLOCK

    def body(kw):
        qpos = i * BLOCK + lax.broadcasted_iota(jnp.int32, (BLOCK, kw), 0)
        kpos = lax.broadcasted_iota(jnp.int32, (BLOCK, kw), 1)
        maskadd = jnp.where(qpos >= kpos, 0.0, NEG)
        for h in range(C_HEADS):
            cs = slice(h * HEAD_DIM, (h + 1) * HEAD_DIM)
            q = q_ref[0, :, cs].astype(BF16)
            k = k_ref[0, :kw, cs].astype(BF16)
            v = v_ref[0, :kw, cs].astype(BF16)
            s2 = _dot_nt(q, k) * (ATTN_SCALE * LOG2E) + cq_ref[0, h] * LOG2E - ck_ref[0, h, :, :kw] * LOG2E
            o_ref[0, :, cs] = _softmax2_pv(s2 + maskadd, v).astype(o_ref.dtype)

    _for_causal_class(i, nb, body)


def _forgetting_attention(y3, c_col, c_row):
    B, L, _ = y3.shape
    nb = L // BLOCK
    cw = C_HEADS * HEAD_DIM
    return pl.pallas_call(
        _fox_kernel,
        grid=(B, nb),
        in_specs=[pl.BlockSpec((1, BLOCK, cw), lambda b, i: (b, i, CB_CQ // C_HEADS)),
                  pl.BlockSpec((1, L, cw), lambda b, i: (b, 0, CB_CK // C_HEADS)),
                  pl.BlockSpec((1, L, cw), lambda b, i: (b, 0, CB_CV // C_HEADS)),
                  pl.BlockSpec((1, C_HEADS, BLOCK, 1), lambda b, i: (b, 0, i, 0)),
                  pl.BlockSpec((1, C_HEADS, 1, L), lambda b, i: (b, 0, 0, 0))],
        out_specs=pl.BlockSpec((1, BLOCK, cw), lambda b, i: (b, i, 0)),
        out_shape=jax.ShapeDtypeStruct((B, L, cw), BF16),
        compiler_params=_cparams(("parallel", "arbitrary")),
        name="forgetting_attention",
    )(y3, y3, y3, c_col, c_row)


def _order_key(x):
    bits = lax.bitcast_convert_type(x, jnp.int32)
    return bits ^ ((bits >> 31) & jnp.int32(0x7FFFFFFF))


def _kth_largest_key(key, k):
    rows = key.shape[0]

    def count_ge(t):
        return jnp.sum(jnp.where(key >= t, 1.0, 0.0), axis=-1, keepdims=True)

    t0 = jnp.where(count_ge(jnp.zeros((rows, 1), jnp.int32)) >= k,
                   jnp.int32(0), jnp.int32(-2 ** 31)) + jnp.zeros((rows, 1), jnp.int32)

    def two_bits(it, t):
        lo = jnp.left_shift(jnp.int32(1), 29 - 2 * it)
        c1, c2 = t | lo, t | (lo + lo)
        c3 = c2 | lo
        n1, n2, n3 = count_ge(c1), count_ge(c2), count_ge(c3)
        return jnp.where(n3 >= k, c3, jnp.where(n2 >= k, c2, jnp.where(n1 >= k, c1, t)))

    t = lax.fori_loop(0, 15, two_bits, t0)
    last = t | jnp.int32(1)
    return jnp.where(count_ge(last) >= k, last, t)


def _dsa_kernel(slope_ref, iq_ref, ik_ref, iw_ref, q_ref, k_ref, v_ref, o_ref, *, n_top):
    i = pl.program_id(1)
    nb = k_ref.shape[1] // BLOCK
    iw = iw_ref[0]

    def body(kw):
        ik = ik_ref[0, :kw, :].astype(BF16)
        score = jnp.zeros((BLOCK, kw), F32)
        for h in range(D_IDX_HEADS):
            iq = iq_ref[0, :, h * D_IDX_DIM:(h + 1) * D_IDX_DIM].astype(BF16)
            rel = jnp.maximum(_dot_nt(iq, ik), 0.0)
            score = score + iw[:, h:h + 1] * rel
        qpos = i * BLOCK + lax.broadcasted_iota(jnp.int32, (BLOCK, kw), 0)
        kpos = lax.broadcasted_iota(jnp.int32, (BLOCK, kw), 1)
        dist = qpos - kpos
        causal = dist >= 0
        key = _order_key(jnp.where(causal, score, NEG))

        thr = _kth_largest_key(key, float(n_top))
        gt = key > thr
        eq = key == thr
        need = float(n_top) - jnp.sum(jnp.where(gt, 1.0, 0.0), axis=-1, keepdims=True)
        row = lax.broadcasted_iota(jnp.int32, (BLOCK, BLOCK), 0)
        col = lax.broadcasted_iota(jnp.int32, (BLOCK, BLOCK), 1)
        upper = jnp.where(row <= col, 1.0, 0.0).astype(BF16)
        eqf = jnp.where(eq, 1.0, 0.0).astype(BF16)
        carry = jnp.zeros((BLOCK, 1), F32)
        rank_tiles = []
        for j in range(kw // BLOCK):
            rank = carry + jnp.dot(eqf[:, j * BLOCK:(j + 1) * BLOCK], upper, preferred_element_type=F32)
            rank_tiles.append(rank)
            carry = rank[:, BLOCK - 1:BLOCK]
        rank = jnp.concatenate(rank_tiles, axis=1)
        mask = (gt | (eq & (rank <= need))) & causal
        maskadd = jnp.where(mask, 0.0, NEG)

        k = k_ref[0, :kw, :].astype(BF16)
        v = v_ref[0, :kw, :].astype(BF16)
        distf = dist.astype(F32)
        for h in range(D_HEADS):
            q = q_ref[0, :, h * HEAD_DIM:(h + 1) * HEAD_DIM].astype(BF16)
            s2 = _dot_nt(q, k) * (ATTN_SCALE * LOG2E) - (slope_ref[h] * LOG2E) * distf + maskadd
            o_ref[0, :, h * HEAD_DIM:(h + 1) * HEAD_DIM] = _softmax2_pv(s2, v).astype(o_ref.dtype)

    _for_causal_class(i, nb, body)


def _indexed_sparse_attention(y3, ik, iw, slopes_all):
    B, L, _ = y3.shape
    nb = L // BLOCK
    n_top = min(D_TOPK, L // 4)
    slopes = slopes_all[np.array(D_SLOPE_IDX)]
    qw, iqw = D_HEADS * HEAD_DIM, D_IDX_HEADS * D_IDX_DIM
    return pl.pallas_call(
        functools.partial(_dsa_kernel, n_top=n_top),
        grid=(B, nb),
        in_specs=[pl.BlockSpec(memory_space=pltpu.SMEM),
                  pl.BlockSpec((1, BLOCK, iqw), lambda b, i: (b, i, CB_DIQ * LANES // iqw)),
                  pl.BlockSpec((1, L, D_IDX_DIM), lambda b, i: (b, 0, 0)),
                  pl.BlockSpec((1, BLOCK, D_IDX_HEADS), lambda b, i: (b, i, 0)),
                  pl.BlockSpec((1, BLOCK, qw), lambda b, i: (b, i, CB_DQ * LANES // qw)),
                  pl.BlockSpec((1, L, HEAD_DIM), lambda b, i: (b, 0, CB_DK)),
                  pl.BlockSpec((1, L, HEAD_DIM), lambda b, i: (b, 0, CB_DV))],
        out_specs=pl.BlockSpec((1, BLOCK, qw), lambda b, i: (b, i, 0)),
        out_shape=jax.ShapeDtypeStruct((B, L, qw), BF16),
        compiler_params=_cparams(("parallel", "arbitrary")),
        name="indexed_sparse_attention",
    )(slopes, y3, ik, iw, y3, y3, y3)


def _compress_kernel(x_ref, pos_ref, w1_ref, w2_ref, o_ref):
    x = x_ref[0, 0].astype(F32)
    nxt = pltpu.roll(x, x.shape[0] - 1, 0)
    blk = jnp.concatenate([x, nxt], axis=1) + pos_ref[0]
    h = jnp.dot(blk.astype(BF16), w1_ref[0], preferred_element_type=F32)
    h = jax.nn.gelu(h, approximate=True)
    o_ref[0, 0] = jnp.dot(h.astype(BF16), w2_ref[0], preferred_element_type=F32).astype(o_ref.dtype)


def _compress(xc, pos, w1, w2):
    B, _, n_chunk, cw = xc.shape
    G = B_KV_GROUPS
    return pl.pallas_call(
        _compress_kernel,
        grid=(2 * G, B),
        in_specs=[pl.BlockSpec((1, 1, n_chunk, cw), lambda a, b: (b, a, 0, 0)),
                  pl.BlockSpec((1, 1, 2 * cw), lambda a, b: (a // G, 0, 0)),
                  pl.BlockSpec((1, 2 * cw, B_CMP_HIDDEN), lambda a, b: (a // G, 0, 0)),
                  pl.BlockSpec((1, B_CMP_HIDDEN, HEAD_DIM), lambda a, b: (a // G, 0, 0))],
        out_specs=pl.BlockSpec((1, 1, n_chunk, HEAD_DIM), lambda a, b: (b, a, 0, 0)),
        out_shape=jax.ShapeDtypeStruct((B, 2 * G, n_chunk, HEAD_DIM), BF16),
        compiler_params=_cparams(("arbitrary", "arbitrary")),
        name="nsa_compress",
    )(xc, pos, w1, w2)


def _nsa_cmp_kernel(slope_ref, *refs, n_cmp, n_slc):
    G, R = B_KV_GROUPS, B_GROUP_SIZE
    q_refs, (ckv_ref, o_ref, sel_ref) = refs[:G], refs[G:]
    i = pl.program_id(1)
    nc = ckv_ref.shape[2]
    t = i * BLOCK + lax.broadcasted_iota(jnp.int32, (BLOCK, nc), 0)
    n = lax.broadcasted_iota(jnp.int32, (BLOCK, nc), 1)
    dist_c = t - (n * B_CMP_STRIDE + B_CMP_LEN - 1)
    mask = (dist_c >= 0) & (n < n_cmp)
    distf = dist_c.astype(F32)

    nn = lax.broadcasted_iota(jnp.int32, (nc, LANES), 0)
    jj = lax.broadcasted_iota(jnp.int32, (nc, LANES), 1)
    off = nn - B_SEL_RATIO * jj + 1
    w = jnp.where((off == 0) | (off == B_SEL_RATIO), 1.0, jnp.where((off > 0) & (off < B_SEL_RATIO), 2.0, 0.0))
    w = jnp.where((nn < n_cmp) & (jj < n_slc), w, 0.0).astype(F32)
    tq = i * BLOCK + lax.broadcasted_iota(jnp.int32, (BLOCK, LANES), 0)
    j = lax.broadcasted_iota(jnp.int32, (BLOCK, LANES), 1)
    cur = tq >> int(np.log2(B_SEL_LEN))
    forced = (j == 0) | (j == cur) | (j == cur - 1)
    jf = j.astype(F32)

    for g in range(G):
        kc = ckv_ref[0, g]
        vc = ckv_ref[0, G + g]
        imp = jnp.zeros((BLOCK, nc), F32)
        for r in range(R):
            q = q_refs[g][0, :, r * HEAD_DIM:(r + 1) * HEAD_DIM].astype(BF16)
            s = _dot_nt(q, kc) * ATTN_SCALE
            s = jnp.where(mask, s - slope_ref[g * R + r] * distf, NEG)
            m = jnp.max(s, axis=-1, keepdims=True)
            e = jnp.where(mask, jnp.exp(s - m), 0.0)
            p = e / jnp.maximum(jnp.sum(e, axis=-1, keepdims=True), 1e-30)
            o_ref[0, :, (g * R + r) * HEAD_DIM:(g * R + r + 1) * HEAD_DIM] = jnp.dot(
                p.astype(BF16), vc, preferred_element_type=F32)
            imp = imp + p
        p_slc = jnp.dot(imp, w, preferred_element_type=F32, precision=lax.Precision.HIGHEST)
        score = jnp.where(forced, 1e9, jnp.where(j <= cur, p_slc, -1e9))
        score = jnp.where(j < n_slc, score, -3e38)
        sel = jnp.zeros((BLOCK, LANES), F32)
        for _ in range(min(B_N_SEL, n_slc)):
            m = jnp.max(score, axis=-1, keepdims=True)
            first = jnp.min(jnp.where(score == m, jf, float(LANES)), axis=-1, keepdims=True)
            hit = jf == first
            sel = jnp.where(hit, 1.0, sel)
            score = jnp.where(hit, -3e38, score)
        sel_ref[0, g] = sel.astype(sel_ref.dtype)


def _nsa_q_specs():
    return [pl.BlockSpec((1, BLOCK, B_GROUP_SIZE * HEAD_DIM), lambda b, i, g=g: (b, i, CB_BQ // B_GROUP_SIZE + g))
            for g in range(B_KV_GROUPS)]


def _nsa_cmp(y3, cmp_kv, slopes, n_cmp, n_slc):
    B, L, _ = y3.shape
    G = B_KV_GROUPS
    nc = cmp_kv.shape[2]
    return pl.pallas_call(
        functools.partial(_nsa_cmp_kernel, n_cmp=n_cmp, n_slc=n_slc),
        grid=(B, L // BLOCK),
        in_specs=[pl.BlockSpec(memory_space=pltpu.SMEM)] + _nsa_q_specs() +
                 [pl.BlockSpec((1, 2 * G, nc, HEAD_DIM), lambda b, i: (b, 0, 0, 0))],
        out_specs=[pl.BlockSpec((1, BLOCK, B_HEADS * HEAD_DIM), lambda b, i: (b, i, 0)),
                   pl.BlockSpec((1, G, BLOCK, LANES), lambda b, i: (b, 0, i, 0))],
        out_shape=[jax.ShapeDtypeStruct((B, L, B_HEADS * HEAD_DIM), F32),
                   jax.ShapeDtypeStruct((B, G, L, LANES), BF16)],
        compiler_params=_cparams(("parallel", "arbitrary")),
        name="nsa_compressed_attention",
    )(slopes, *([y3] * G), cmp_kv)


def _nsa_slc_kernel(slope_ref, *refs):
    G, R = B_KV_GROUPS, B_GROUP_SIZE
    q_refs, k_refs, v_refs, (sel_ref, o_ref) = refs[:G], refs[G:2 * G], refs[2 * G:3 * G], refs[3 * G:]
    i = pl.program_id(1)
    nb = k_refs[0].shape[1] // BLOCK

    def body(kw):
        jj = lax.broadcasted_iota(jnp.int32, (LANES, kw), 0)
        ss = lax.broadcasted_iota(jnp.int32, (LANES, kw), 1)
        expand = jnp.where((ss >> int(np.log2(B_SEL_LEN))) == jj, 1.0, 0.0).astype(BF16)
        qpos = i * BLOCK + lax.broadcasted_iota(jnp.int32, (BLOCK, kw), 0)
        kpos = lax.broadcasted_iota(jnp.int32, (BLOCK, kw), 1)
        dist = qpos - kpos
        distf = dist.astype(F32)
        for g in range(G):
            picked = jnp.dot(sel_ref[0, g], expand, preferred_element_type=F32) > 0.5
            maskadd = jnp.where(picked & (dist >= 0), 0.0, NEG)
            k = k_refs[g][0, :kw, :].astype(BF16)
            v = v_refs[g][0, :kw, :].astype(BF16)
            for r in range(R):
                q = q_refs[g][0, :, r * HEAD_DIM:(r + 1) * HEAD_DIM].astype(BF16)
                s2 = _dot_nt(q, k) * (ATTN_SCALE * LOG2E) - (slope_ref[g * R + r] * LOG2E) * distf + maskadd
                o_ref[0, :, (g * R + r) * HEAD_DIM:(g * R + r + 1) * HEAD_DIM] = _softmax2_pv(s2, v)

    _for_causal_class(i, nb, body)


def _nsa_slc(y3, sel, slopes):
    B, L, _ = y3.shape
    G = B_KV_GROUPS
    kv_specs = [pl.BlockSpec((1, L, HEAD_DIM), lambda b, i, c=CB_BKV + 4 + kv * G + g: (b, 0, c))
                for kv in range(2) for g in range(G)]
    return pl.pallas_call(
        _nsa_slc_kernel,
        grid=(B, L // BLOCK),
        in_specs=[pl.BlockSpec(memory_space=pltpu.SMEM)] + _nsa_q_specs() + kv_specs +
                 [pl.BlockSpec((1, G, BLOCK, LANES), lambda b, i: (b, 0, i, 0))],
        out_specs=pl.BlockSpec((1, BLOCK, B_HEADS * HEAD_DIM), lambda b, i: (b, i, 0)),
        out_shape=jax.ShapeDtypeStruct((B, L, B_HEADS * HEAD_DIM), F32),
        compiler_params=_cparams(("parallel", "arbitrary")),
        name="nsa_selected_attention",
    )(slopes, *([y3] * (3 * G)), sel)


def _b_gate_kernel(gl_ref, oc_ref, os_ref, ow_ref, out_ref):
    gate = jax.nn.sigmoid(gl_ref[...])
    for h in range(B_HEADS):
        sl = slice(h * HEAD_DIM, (h + 1) * HEAD_DIM)
        out = (gate[:, h:h + 1] * oc_ref[:, sl] + gate[:, B_HEADS + h:B_HEADS + h + 1] * os_ref[:, sl]
               + gate[:, 2 * B_HEADS + h:2 * B_HEADS + h + 1] * ow_ref[:, sl])
        out_ref[:, sl] = out.astype(out_ref.dtype)


def _b_gate(gl, o_cmp, o_slc, o_win):
    M, C = o_cmp.shape
    tm = _tile(M, 512)
    spec = pl.BlockSpec((tm, C), lambda i: (i, 0))
    return pl.pallas_call(
        _b_gate_kernel,
        grid=(M // tm,),
        in_specs=[pl.BlockSpec((tm, gl.shape[1]), lambda i: (i, 0)), spec, spec, spec],
        out_specs=spec,
        out_shape=jax.ShapeDtypeStruct((M, C), BF16),
        compiler_params=_cparams(("parallel",)),
        name="nsa_branch_gate",
    )(gl, o_cmp, o_slc, o_win)


def _native_sparse_attention(y3, gate_logits, cmp_w1, cmp_w2, cmp_pos, slopes_all):
    B, L, _ = y3.shape
    G, R = B_KV_GROUPS, B_GROUP_SIZE
    slopes = slopes_all[np.array(B_SLOPE_IDX)]
    n_chunk = L // B_CMP_STRIDE
    n_cmp = n_chunk - B_CMP_LEN // B_CMP_STRIDE + 1
    n_slc = L // B_SEL_LEN
    assert B_CMP_LEN == 2 * B_CMP_STRIDE and n_chunk % 8 == 0 and n_slc <= LANES

    xc = y3[:, :, CB_BKV * LANES:(CB_BKV + 2 * G) * LANES].reshape(B, L, 2 * G, HEAD_DIM)
    xc = xc.transpose(0, 2, 1, 3).reshape(B, 2 * G, n_chunk, B_CMP_STRIDE * HEAD_DIM)
    cmp_kv = _compress(xc, cmp_pos.reshape(2, 1, B_CMP_LEN * HEAD_DIM), cmp_w1.astype(BF16), cmp_w2.astype(BF16))

    o_cmp, sel = _nsa_cmp(y3, cmp_kv, slopes, n_cmp, n_slc)
    o_slc = _nsa_slc(y3, sel, slopes)
    w_steps = B_WINDOW - 1
    (o_win,) = _banded_attention(y3, y3, y3, slopes, n_groups=G, hp=R, shared_kv=True,
                                 q_blk=lambda g: CB_BQ // R + g,
                                 k_blk=lambda g: CB_BKV + 8 + g,
                                 v_blk=lambda g: CB_BKV + 10 + g,
                                 window=w_steps, n_prev=-(-w_steps // BLOCK), step=1, with_lse=False)
    C = B_HEADS * HEAD_DIM
    return _b_gate(gate_logits, o_cmp.reshape(B * L, C), o_slc.reshape(B * L, C), o_win.reshape(B * L, C))


def _gated_merge_kernel(x_ref, *refs, nk):
    nbr = N_BRANCH
    wg, o, w, bias = refs[:nbr], refs[nbr:2 * nbr], refs[2 * nbr:3 * nbr], refs[3 * nbr:4 * nbr]
    out_ref, acc = refs[4 * nbr], refs[4 * nbr + 1:]

    def logits(c):
        return jnp.dot(x_ref[...], wg[c][...], preferred_element_type=F32)

    def first():
        for c in range(nbr):
            acc[0][c] = logits(c)

    def middle():
        for c in range(nbr):
            acc[0][c] += logits(c)

    def last(has_acc):
        merged = None
        for c in range(nbr):
            z = (acc[0][c] + logits(c)) if has_acc else logits(c)
            term = _sigmoid(z + bias[c][...]) * jnp.dot(o[c][...], w[c][...], preferred_element_type=F32)
            merged = term if merged is None else merged + term
        out_ref[...] = merged.astype(out_ref.dtype)

    _k_steps(nk, first, middle, last)


def _gated_merge(xb, w_gate, b_gate, branch_outs, branch_ws):
    M, K = xb.shape
    D = branch_ws[0].shape[1]
    tm, tn, tk = _tile(M, 1024), _tile(D, 512), _tile(K, 1024)
    nj, nk = D // tn, K // tk
    wg_specs = [pl.BlockSpec((tk, tn), lambda i, j, k, c=c: (k, c * nj + j)) for c in range(N_BRANCH)]
    o_specs = [pl.BlockSpec((tm, o.shape[1]), lambda i, j, k: (i, 0)) for o in branch_outs]
    w_specs = [pl.BlockSpec((w.shape[0], tn), lambda i, j, k: (0, j)) for w in branch_ws]
    b_specs = [pl.BlockSpec((1, tn), lambda i, j, k, c=c: (0, c * nj + j)) for c in range(N_BRANCH)]
    return pl.pallas_call(
        functools.partial(_gated_merge_kernel, nk=nk),
        grid=(M // tm, nj, nk),
        in_specs=[pl.BlockSpec((tm, tk), lambda i, j, k: (i, k))] + wg_specs + o_specs + w_specs + b_specs,
        out_specs=pl.BlockSpec((tm, tn), lambda i, j, k: (i, j)),
        out_shape=jax.ShapeDtypeStruct((M, D), BF16),
        scratch_shapes=[pltpu.VMEM((N_BRANCH, tm, tn), F32)] if nk > 1 else [],
        compiler_params=_cparams(("parallel", "parallel", "arbitrary")),
        name="gated_merge",
    )(xb, *([w_gate] * N_BRANCH), *branch_outs, *branch_ws, *([b_gate.reshape(1, -1)] * N_BRANCH))


def _attn_weight(w_in):
    D = w_in.shape[0]
    sizes = (A_HEADS * HEAD_DIM,) * 3 + (B_HEADS * HEAD_DIM, 3 * 2 * B_KV_GROUPS * HEAD_DIM, 3 * B_HEADS) + \
            (C_HEADS * HEAD_DIM,) * 3 + (C_HEADS,) + (D_HEADS * HEAD_DIM, HEAD_DIM, HEAD_DIM) + \
            (D_IDX_HEADS * D_IDX_DIM, D_IDX_DIM, D_IDX_HEADS)
    offs = np.concatenate([[0], np.cumsum(sizes)])
    (a_q, a_k, a_v, b_q, b_kv, b_g, c_q, c_k, c_v, c_f, d_q, d_k, d_v, d_iq, d_ik, d_iw) = [
        w_in[:, offs[n]:offs[n + 1]] for n in range(len(sizes))]
    misc = jnp.zeros((D, 2 * LANES), w_in.dtype)
    misc = misc.at[:, MISC_BG:MISC_BG + 3 * B_HEADS].set(b_g)
    misc = misc.at[:, MISC_CF:MISC_CF + C_HEADS].set(c_f)
    misc = misc.at[:, MISC_IK:MISC_IK + D_IDX_DIM].set(d_ik)
    misc = misc.at[:, MISC_IW:MISC_IW + D_IDX_HEADS].set(d_iw)
    parts = [a_q, a_k, a_v, b_q, b_kv, c_q, c_k, c_v, d_q, d_iq, d_k, d_v, misc]
    w = jnp.concatenate([p.astype(BF16) for p in parts], axis=1)
    assert w.shape[1] == N_CB * LANES
    return w, int(offs[-1])


def _hybrid_mixer(xb, B, L, w_in, b_forget, b_gate, cmp_w1, cmp_w2, cmp_pos, w_branch, w_out):
    M, D = xb.shape
    slopes_all = jnp.exp2(-8.0 * jnp.arange(1, N_ALIBI + 1, dtype=F32) / N_ALIBI)
    w_attn, gate_off = _attn_weight(w_in)
    y3 = _matmul(xb, w_attn, BF16).reshape(B, L, N_CB * LANES)
    misc = _matmul(xb, w_attn[:, CB_MISC * LANES:(CB_MISC + 1) * LANES], F32)

    bias = jnp.zeros((1, LANES), F32).at[0, MISC_CF:MISC_CF + C_HEADS].set(b_forget)
    c = _logsig_cumsum(misc.reshape(B, L, LANES), bias)[:, :, MISC_CF:MISC_CF + C_HEADS].transpose(0, 2, 1)

    o_a = _dilated_attention(y3, slopes_all)
    o_b = _native_sparse_attention(y3, misc[:, MISC_BG:MISC_BG + 3 * B_HEADS], cmp_w1, cmp_w2, cmp_pos, slopes_all)
    o_c = _forgetting_attention(y3, c[..., None], c[:, :, None, :])
    o_d = _indexed_sparse_attention(y3, misc[:, MISC_IK:MISC_IK + D_IDX_DIM].reshape(B, L, D_IDX_DIM),
                                    misc[:, MISC_IW:MISC_IW + D_IDX_HEADS].reshape(B, L, D_IDX_HEADS), slopes_all)

    sizes = (A_HEADS_PER_PAIR * HEAD_DIM, B_HEADS * HEAD_DIM, C_HEADS * HEAD_DIM, D_HEADS * HEAD_DIM)
    offs = np.concatenate([[0], np.cumsum(sizes)])
    ws = [w_branch[offs[n]:offs[n + 1]].astype(BF16) for n in range(N_BRANCH)]
    outs = [o_a, o_b, o_c.reshape(M, -1), o_d.reshape(M, -1)]
    merged = _gated_merge(xb, w_in[:, gate_off:].astype(BF16), b_gate, outs, ws)
    return _matmul(merged, w_out.astype(BF16), F32)


def kernel(x, ln_g, ln_b, ffn1_w_gate, ffn1_w_up, ffn1_w_down, w_in, b_forget, b_gate, cmp_w1, cmp_w2, cmp_pos,
           w_branch, w_out, ffn2_w_gate, ffn2_w_up, ffn2_w_down):
    B, L, D = x.shape
    assert L % BLOCK == 0 and D % LANES == 0
    xf = x.reshape(B * L, D)
    xb = xf.astype(BF16)

    def ffn(xf, xb, wg, wu, wd, g, b):
        h = _ffn_up(xb, wg.astype(BF16), wu.astype(BF16))
        y = _matmul(h, wd.astype(BF16), F32)
        return _add_ln(xf, y, g, b, 0.5)

    for l in range(ln_g.shape[0]):
        xf, xb = ffn(xf, xb, ffn1_w_gate[l], ffn1_w_up[l], ffn1_w_down[l], ln_g[l, 0], ln_b[l, 0])
        y = _hybrid_mixer(xb, B, L, w_in[l], b_forget[l], b_gate[l], cmp_w1[l], cmp_w2[l], cmp_pos[l],
                          w_branch[l], w_out[l])
        xf, xb = _add_ln(xf, y, ln_g[l, 1], ln_b[l, 1], 1.0)
        xf, xb = ffn(xf, xb, ffn2_w_gate[l], ffn2_w_up[l], ffn2_w_down[l], ln_g[l, 2], ln_b[l, 2])
    return xf.reshape(B, L, D)
```

```python
import functools

import numpy as np
import jax
import jax.numpy as jnp
from jax import lax
from jax.experimental import pallas as pl
from jax.experimental.pallas import tpu as pltpu

F32 = jnp.float32
BF16 = jnp.bfloat16

HEAD_DIM = 128
BLOCK = 128
LANES = 128
NEG = -1e30
ATTN_SCALE = HEAD_DIM ** -0.5
LOG2E = 1.4426950408889634
LN2 = 0.6931471805599453
DEPTH = 2
ALPHA = (2 * DEPTH) ** 0.25
LN_EPS = 1e-5

A_PAIRS = ((128, 1), (512, 4), (2048, 16))
A_HEADS_PER_PAIR = 4
A_HEADS = A_HEADS_PER_PAIR * len(A_PAIRS)
B_HEADS = 8
B_KV_GROUPS = 2
B_GROUP_SIZE = B_HEADS // B_KV_GROUPS
B_CMP_LEN = 32
B_CMP_STRIDE = 16
B_SEL_LEN = 64
B_SEL_RATIO = B_SEL_LEN // B_CMP_STRIDE
B_N_SEL = 8
B_WINDOW = 512
B_CMP_HIDDEN = 512
C_HEADS = 8
D_HEADS = 8
D_IDX_HEADS = 8
D_IDX_DIM = 64
D_TOPK = 256
N_BRANCH = 4
N_ALIBI = A_HEADS + B_HEADS + D_HEADS
A_SLOPE_IDX = (0, 1, 2, 3, 12, 13, 14, 15, 24, 25, 26, 27)
B_SLOPE_IDX = (4, 5, 6, 7, 8, 9, 10, 11)
D_SLOPE_IDX = (16, 17, 18, 19, 20, 21, 22, 23)

CB_A = 0
CB_BQ = 36
CB_BKV = 44
CB_CQ, CB_CK, CB_CV = 56, 64, 72
CB_DQ = 80
CB_DIQ = 88
CB_DK, CB_DV = 92, 93
CB_MISC = 94
N_CB = 96
MISC_BG, MISC_CF, MISC_IK, MISC_IW = 0, 24, 32, 96

CAUSAL_CLASSES = 8
VMEM_LIMIT = 56 * 1024 * 1024


def _cparams(sem):
    return pltpu.CompilerParams(dimension_semantics=sem, vmem_limit_bytes=VMEM_LIMIT)


def _tile(dim, pref):
    return pref if dim % pref == 0 else dim


def _dot_nt(a, b):
    return lax.dot_general(a, b, (((1,), (1,)), ((), ())), preferred_element_type=F32)


def _softmax2_pv(s2, v):
    m = jnp.max(s2, axis=-1, keepdims=True)
    e = jnp.exp2(s2 - m)
    den = jnp.maximum(jnp.sum(e, axis=-1, keepdims=True), 1e-30)
    return jnp.dot(e.astype(BF16), v, preferred_element_type=F32) / den


def _for_causal_class(i, nb, body):
    n_cls = CAUSAL_CLASSES if nb % CAUSAL_CLASSES == 0 else 1
    per = nb // n_cls
    for c in range(n_cls):
        pl.when((i >= c * per) & (i < (c + 1) * per))(functools.partial(body, (c + 1) * per * BLOCK))


def _cast_kernel(x_ref, o_ref):
    o_ref[...] = x_ref[0].astype(o_ref.dtype)


def _cast_bf16(w_stack, l):
    _, R, C = w_stack.shape
    tr = R
    for cand in (1024, 512, 256, 128, 64, 32, 16):
        if R % cand == 0:
            tr = cand
            if cand * C * 4 <= 8 * 1024 * 1024:
                break
    return pl.pallas_call(
        _cast_kernel,
        grid=(R // tr,),
        in_specs=[pl.BlockSpec((1, tr, C), lambda i: (l, i, 0))],
        out_specs=pl.BlockSpec((tr, C), lambda i: (i, 0)),
        out_shape=jax.ShapeDtypeStruct((R, C), BF16),
        compiler_params=_cparams(("parallel",)),
        name="cast_bf16",
    )(w_stack)


def _sigmoid(z):
    return 0.5 * jnp.tanh(0.5 * z) + 0.5


def _k_steps(nk, first, middle, last):
    k = pl.program_id(2)
    if nk == 1:
        last(False)
        return
    pl.when(k == 0)(first)
    if nk > 2:
        pl.when((k > 0) & (k < nk - 1))(middle)
    pl.when(k == nk - 1)(functools.partial(last, True))


def _mm_kernel(x_ref, w_ref, o_ref, *acc, nk):
    def prod():
        return jnp.dot(x_ref[...], w_ref[...], preferred_element_type=F32)

    def first():
        acc[0][...] = prod()

    def middle():
        acc[0][...] += prod()

    def last(has_acc):
        o_ref[...] = ((acc[0][...] + prod()) if has_acc else prod()).astype(o_ref.dtype)

    _k_steps(nk, first, middle, last)


def _matmul(x, w, out_dtype, tiles=(1024, 1024, 4096)):
    M, K = x.shape
    N = w.shape[1]
    tm, tn, tk = _tile(M, tiles[0]), _tile(N, tiles[1]), _tile(K, tiles[2])
    nk = K // tk
    return pl.pallas_call(
        functools.partial(_mm_kernel, nk=nk),
        grid=(M // tm, N // tn, nk),
        in_specs=[pl.BlockSpec((tm, tk), lambda i, j, k: (i, k)),
                  pl.BlockSpec((tk, tn), lambda i, j, k: (k, j))],
        out_specs=pl.BlockSpec((tm, tn), lambda i, j, k: (i, j)),
        out_shape=jax.ShapeDtypeStruct((M, N), out_dtype),
        scratch_shapes=[pltpu.VMEM((tm, tn), F32)] if nk > 1 else [],
        compiler_params=_cparams(("parallel", "parallel", "arbitrary")),
        name="matmul",
    )(x, w)


def _ffn_up_kernel(x_ref, wg_ref, wu_ref, o_ref, *acc, nk):
    def prods():
        x = x_ref[...]
        return (jnp.dot(x, wg_ref[...], preferred_element_type=F32),
                jnp.dot(x, wu_ref[...], preferred_element_type=F32))

    def first():
        acc[0][...], acc[1][...] = prods()

    def middle():
        g, u = prods()
        acc[0][...] += g
        acc[1][...] += u

    def last(has_acc):
        g, u = prods()
        if has_acc:
            g, u = acc[0][...] + g, acc[1][...] + u
        o_ref[...] = (g * _sigmoid(g) * u).astype(o_ref.dtype)

    _k_steps(nk, first, middle, last)


def _ffn_up(x, wg, wu, tiles=(1024, 1024, 2048)):
    M, K = x.shape
    N = wg.shape[1]
    tm, tn, tk = _tile(M, tiles[0]), _tile(N, tiles[1]), _tile(K, tiles[2])
    nk = K // tk
    return pl.pallas_call(
        functools.partial(_ffn_up_kernel, nk=nk),
        grid=(M // tm, N // tn, nk),
        in_specs=[pl.BlockSpec((tm, tk), lambda i, j, k: (i, k)),
                  pl.BlockSpec((tk, tn), lambda i, j, k: (k, j)),
                  pl.BlockSpec((tk, tn), lambda i, j, k: (k, j))],
        out_specs=pl.BlockSpec((tm, tn), lambda i, j, k: (i, j)),
        out_shape=jax.ShapeDtypeStruct((M, N), BF16),
        scratch_shapes=[pltpu.VMEM((tm, tn), F32), pltpu.VMEM((tm, tn), F32)] if nk > 1 else [],
        compiler_params=_cparams(("parallel", "parallel", "arbitrary")),
        name="ffn_up",
    )(x, wg, wu)


def _add_ln_kernel(x_ref, y_ref, g_ref, b_ref, o_ref, ob_ref, *, cy):
    z = ALPHA * x_ref[...] + cy * y_ref[...]
    mu = jnp.mean(z, axis=-1, keepdims=True)
    zc = z - mu
    var = jnp.mean(zc * zc, axis=-1, keepdims=True)
    out = zc * lax.rsqrt(var + LN_EPS) * g_ref[...] + b_ref[...]
    o_ref[...] = out
    ob_ref[...] = out.astype(BF16)


def _add_ln(x, y, g, b, cy):
    M, D = x.shape
    tm = _tile(M, 256)
    row = pl.BlockSpec((tm, D), lambda i: (i, 0))
    vec = pl.BlockSpec((1, D), lambda i: (0, 0))
    return pl.pallas_call(
        functools.partial(_add_ln_kernel, cy=cy),
        grid=(M // tm,),
        in_specs=[row, row, vec, vec],
        out_specs=[row, row],
        out_shape=[jax.ShapeDtypeStruct((M, D), F32), jax.ShapeDtypeStruct((M, D), BF16)],
        compiler_params=_cparams(("parallel",)),
        name="add_ln",
    )(x, y, g.reshape(1, D), b.reshape(1, D))


def _banded_kernel(slope_ref, q_ref, k_ref, v_ref, o_ref, lse_ref, *, window, n_prev, step, nb, hp):
    g = pl.program_id(1)
    i = pl.program_id(2)
    kwb = min(n_prev + 1, nb)
    kw = kwb * BLOCK
    start = pl.multiple_of(jnp.maximum(i - n_prev, 0) * BLOCK, BLOCK)
    qpos = i * BLOCK + lax.broadcasted_iota(jnp.int32, (BLOCK, kw), 0)
    kpos = start + lax.broadcasted_iota(jnp.int32, (BLOCK, kw), 1)
    dist = qpos - kpos
    maskadd = jnp.where((dist >= 0) & (dist <= window), 0.0, NEG)
    krel = (step * lax.broadcasted_iota(jnp.int32, (1, kw), 1)).astype(F32)
    qrel = (step * (i * BLOCK - start + lax.broadcasted_iota(jnp.int32, (BLOCK, 1), 0))).astype(F32)
    for h in range(hp):
        cs = slice(h * HEAD_DIM, (h + 1) * HEAD_DIM)
        slope = slope_ref[g * hp + h]
        q = q_ref[0, :, cs].astype(BF16)
        kwin = k_ref[0, pl.ds(start, kw), cs].astype(BF16)
        vwin = v_ref[0, pl.ds(start, kw), cs].astype(BF16)
        s = _dot_nt(q, kwin) * ATTN_SCALE + (slope * krel + maskadd)
        m = jnp.max(s, axis=-1, keepdims=True)
        e = jnp.exp(s - m)
        den = jnp.maximum(jnp.sum(e, axis=-1, keepdims=True), 1e-30)
        o_ref[0, :, cs] = jnp.dot(e.astype(BF16), vwin, preferred_element_type=F32) / den
        lse_ref[0, 0, :, h:h + 1] = m + jnp.log(den) - slope * qrel


def _banded_attention(qa, ka, va, slopes, *, n_groups, hp, q_blk, k_blk, v_blk, window, n_prev, step):
    B, N, _ = qa.shape
    nb = N // BLOCK
    qspec = pl.BlockSpec((1, BLOCK, hp * HEAD_DIM), lambda b, g, i: (b, i, q_blk(g)))
    kspec = pl.BlockSpec((1, N, hp * HEAD_DIM), lambda b, g, i: (b, 0, k_blk(g)))
    vspec = pl.BlockSpec((1, N, hp * HEAD_DIM), lambda b, g, i: (b, 0, v_blk(g)))
    ospec = pl.BlockSpec((1, BLOCK, hp * HEAD_DIM), lambda b, g, i: (b, i, g))
    oshape = jax.ShapeDtypeStruct((B, N, n_groups * hp * HEAD_DIM), F32)
    lspec = pl.BlockSpec((1, 1, BLOCK, hp), lambda b, g, i: (b, g, i, 0))
    lshape = jax.ShapeDtypeStruct((B, n_groups, N, hp), F32)
    return pl.pallas_call(
        functools.partial(_banded_kernel, window=window, n_prev=n_prev, step=step, nb=nb, hp=hp),
        grid=(B, n_groups, nb),
        in_specs=[pl.BlockSpec(memory_space=pltpu.SMEM), qspec, kspec, vspec],
        out_specs=[ospec, lspec],
        out_shape=[oshape, lshape],
        compiler_params=_cparams(("parallel", "parallel", "arbitrary")),
        name="banded_attention",
    )(slopes, qa, ka, va)


def _a_combine_kernel(o0, o1, o2, l0, l1, l2, out_ref):
    a, b, c = l0[...], l1[...], l2[...]
    m = jnp.maximum(jnp.maximum(a, b), c)
    ea, eb, ec = jnp.exp(a - m), jnp.exp(b - m), jnp.exp(c - m)
    tot = ea + eb + ec
    wa, wb, wc = ea / tot, eb / tot, ec / tot
    for h in range(a.shape[1]):
        cs = slice(h * HEAD_DIM, (h + 1) * HEAD_DIM)
        out = wa[:, h:h + 1] * o0[:, cs] + wb[:, h:h + 1] * o1[:, cs] + wc[:, h:h + 1] * o2[:, cs]
        out_ref[:, cs] = out.astype(out_ref.dtype)


def _a_combine(outs, lses):
    M, C = outs[0].shape
    tm = _tile(M, 512)
    spec = pl.BlockSpec((tm, C), lambda i: (i, 0))
    lspec = pl.BlockSpec((tm, lses[0].shape[1]), lambda i: (i, 0))
    return pl.pallas_call(
        _a_combine_kernel,
        grid=(M // tm,),
        in_specs=[spec] * 3 + [lspec] * 3,
        out_specs=spec,
        out_shape=jax.ShapeDtypeStruct((M, C), BF16),
        compiler_params=_cparams(("parallel",)),
        name="a_combine",
    )(*outs, *lses)


def _dilated_attention(y3, slopes_all):
    B, L, _ = y3.shape
    slopes = slopes_all[np.array(A_SLOPE_IDX)]
    hpp = A_HEADS_PER_PAIR
    gw = 3 * hpp * HEAD_DIM
    outs, lses = [], []
    for g, (window, dil) in enumerate(A_PAIRS):
        n = L // dil
        assert n % BLOCK == 0
        steps = window // dil
        if dil == 1:
            ya, base = y3, CB_A // hpp + g
            blk = lambda part: (lambda r: base + part * (A_HEADS // hpp))
        else:
            cols = [y3[:, :, (CB_A + p * A_HEADS + g * hpp) * LANES:(CB_A + p * A_HEADS + (g + 1) * hpp) * LANES]
                    for p in range(3)]
            ya = jnp.concatenate(cols, axis=-1).reshape(B, n, dil * gw)
            blk = lambda part: (lambda r: r * 3 + part)
        o, lse = _banded_attention(ya, ya, ya, jnp.tile(slopes[g * hpp:(g + 1) * hpp], dil),
                                   n_groups=dil, hp=hpp, q_blk=blk(0), k_blk=blk(1), v_blk=blk(2),
                                   window=steps, n_prev=-(-steps // BLOCK), step=dil)
        outs.append(o.reshape(B * L, hpp * HEAD_DIM))
        lses.append(lse.transpose(0, 2, 1, 3).reshape(B * L, hpp))
    return _a_combine(outs, lses)


def _logsig_cumsum_kernel(z_ref, bias_ref, o_ref):
    L = z_ref.shape[1]
    row = lax.broadcasted_iota(jnp.int32, (BLOCK, BLOCK), 0)
    col = lax.broadcasted_iota(jnp.int32, (BLOCK, BLOCK), 1)
    tri = jnp.where(row >= col, 1.0, 0.0).astype(F32)
    carry = jnp.zeros((1, LANES), F32)
    for j in range(L // BLOCK):
        z = z_ref[0, j * BLOCK:(j + 1) * BLOCK, :] + bias_ref[...]
        ls = jnp.minimum(z, 0.0) - jnp.log(1.0 + jnp.exp(-jnp.abs(z)))
        c = jnp.dot(tri, ls, preferred_element_type=F32, precision=lax.Precision.HIGHEST) + carry
        o_ref[0, j * BLOCK:(j + 1) * BLOCK, :] = c
        carry = c[BLOCK - 1:BLOCK, :]


def _logsig_cumsum(z, bias):
    B, L, _ = z.shape
    spec = pl.BlockSpec((1, L, LANES), lambda b: (b, 0, 0))
    return pl.pallas_call(
        _logsig_cumsum_kernel,
        grid=(B,),
        in_specs=[spec, pl.BlockSpec((1, LANES), lambda b: (0, 0))],
        out_specs=spec,
        out_shape=jax.ShapeDtypeStruct((B, L, LANES), F32),
        compiler_params=_cparams(("parallel",)),
        name="logsig_cumsum",
    )(z, bias)


def _fox_kernel(q_ref, k_ref, v_ref, cq_ref, ck_ref, o_ref):
    i = pl.program_id(1)
    nb = k_ref.shape[1] // BLOCK

    def body(kw):
        qpos = i * BLOCK + lax.broadcasted_iota(jnp.int32, (BLOCK, kw), 0)
        kpos = lax.broadcasted_iota(jnp.int32, (BLOCK, kw), 1)
        maskadd = jnp.where(qpos >= kpos, 0.0, NEG)
        for h in range(C_HEADS):
            cs = slice(h * HEAD_DIM, (h + 1) * HEAD_DIM)
            q = q_ref[0, :, cs].astype(BF16)
            k = k_ref[0, :kw, cs].astype(BF16)
            v = v_ref[0, :kw, cs].astype(BF16)
            s2 = _dot_nt(q, k) * (ATTN_SCALE * LOG2E) + cq_ref[0, h] * LOG2E - ck_ref[0, h, :, :kw] * LOG2E
            o_ref[0, :, cs] = _softmax2_pv(s2 + maskadd, v).astype(o_ref.dtype)

    _for_causal_class(i, nb, body)


def _forgetting_attention(y3, c_col, c_row):
    B, L, _ = y3.shape
    nb = L // BLOCK
    cw = C_HEADS * HEAD_DIM
    return pl.pallas_call(
        _fox_kernel,
        grid=(B, nb),
        in_specs=[pl.BlockSpec((1, BLOCK, cw), lambda b, i: (b, i, CB_CQ // C_HEADS)),
                  pl.BlockSpec((1, L, cw), lambda b, i: (b, 0, CB_CK // C_HEADS)),
                  pl.BlockSpec((1, L, cw), lambda b, i: (b, 0, CB_CV // C_HEADS)),
                  pl.BlockSpec((1, C_HEADS, BLOCK, 1), lambda b, i: (b, 0, i, 0)),
                  pl.BlockSpec((1, C_HEADS, 1, L), lambda b, i: (b, 0, 0, 0))],
        out_specs=pl.BlockSpec((1, BLOCK, cw), lambda b, i: (b, i, 0)),
        out_shape=jax.ShapeDtypeStruct((B, L, cw), BF16),
        compiler_params=_cparams(("parallel", "arbitrary")),
        name="forgetting_attention",
    )(y3, y3, y3, c_col, c_row)


def _order_key(x):
    bits = lax.bitcast_convert_type(x, jnp.int32)
    return bits ^ ((bits >> 31) & jnp.int32(0x7FFFFFFF))


def _kth_largest_key(key, k):
    rows = key.shape[0]

    def count_ge(t):
        return jnp.sum(jnp.where(key >= t, 1.0, 0.0), axis=-1, keepdims=True)

    t0 = jnp.where(count_ge(jnp.zeros((rows, 1), jnp.int32)) >= k,
                   jnp.int32(0), jnp.int32(-2 ** 31)) + jnp.zeros((rows, 1), jnp.int32)

    def two_bits(it, t):
        lo = jnp.left_shift(jnp.int32(1), 29 - 2 * it)
        c1, c2 = t | lo, t | (lo + lo)
        c3 = c2 | lo
        n1, n2, n3 = count_ge(c1), count_ge(c2), count_ge(c3)
        return jnp.where(n3 >= k, c3, jnp.where(n2 >= k, c2, jnp.where(n1 >= k, c1, t)))

    t = lax.fori_loop(0, 15, two_bits, t0)
    last = t | jnp.int32(1)
    return jnp.where(count_ge(last) >= k, last, t)


def _dsa_kernel(slope_ref, iq_ref, ik_ref, iw_ref, q_ref, k_ref, v_ref, o_ref, *, n_top):
    i = pl.program_id(1)
    nb = k_ref.shape[1] // BLOCK
    iw = iw_ref[0]

    def body(kw):
        ik = ik_ref[0, :kw, :].astype(BF16)
        score = jnp.zeros((BLOCK, kw), F32)
        for h in range(D_IDX_HEADS):
            iq = iq_ref[0, :, h * D_IDX_DIM:(h + 1) * D_IDX_DIM].astype(BF16)
            rel = jnp.maximum(_dot_nt(iq, ik), 0.0)
            score = score + iw[:, h:h + 1] * rel
        qpos = i * BLOCK + lax.broadcasted_iota(jnp.int32, (BLOCK, kw), 0)
        kpos = lax.broadcasted_iota(jnp.int32, (BLOCK, kw), 1)
        dist = qpos - kpos
        causal = dist >= 0
        key = _order_key(jnp.where(causal, score, NEG))

        thr = _kth_largest_key(key, float(n_top))
        gt = key > thr
        eq = key == thr
        need = float(n_top) - jnp.sum(jnp.where(gt, 1.0, 0.0), axis=-1, keepdims=True)
        row = lax.broadcasted_iota(jnp.int32, (BLOCK, BLOCK), 0)
        col = lax.broadcasted_iota(jnp.int32, (BLOCK, BLOCK), 1)
        upper = jnp.where(row <= col, 1.0, 0.0).astype(BF16)
        eqf = jnp.where(eq, 1.0, 0.0).astype(BF16)
        carry = jnp.zeros((BLOCK, 1), F32)
        rank_tiles = []
        for j in range(kw // BLOCK):
            rank = carry + jnp.dot(eqf[:, j * BLOCK:(j + 1) * BLOCK], upper, preferred_element_type=F32)
            rank_tiles.append(rank)
            carry = rank[:, BLOCK - 1:BLOCK]
        rank = jnp.concatenate(rank_tiles, axis=1)
        mask = (gt | (eq & (rank <= need))) & causal
        maskadd = jnp.where(mask, 0.0, NEG)

        k = k_ref[0, :kw, :].astype(BF16)
        v = v_ref[0, :kw, :].astype(BF16)
        distf = dist.astype(F32)
        for h in range(D_HEADS):
            q = q_ref[0, :, h * HEAD_DIM:(h + 1) * HEAD_DIM].astype(BF16)
            s2 = _dot_nt(q, k) * (ATTN_SCALE * LOG2E) - (slope_ref[h] * LOG2E) * distf + maskadd
            o_ref[0, :, h * HEAD_DIM:(h + 1) * HEAD_DIM] = _softmax2_pv(s2, v).astype(o_ref.dtype)

    _for_causal_class(i, nb, body)


def _indexed_sparse_attention(y3, ik, iw, slopes_all):
    B, L, _ = y3.shape
    nb = L // BLOCK
    n_top = min(D_TOPK, L // 4)
    slopes = slopes_all[np.array(D_SLOPE_IDX)]
    qw, iqw = D_HEADS * HEAD_DIM, D_IDX_HEADS * D_IDX_DIM
    return pl.pallas_call(
        functools.partial(_dsa_kernel, n_top=n_top),
        grid=(B, nb),
        in_specs=[pl.BlockSpec(memory_space=pltpu.SMEM),
                  pl.BlockSpec((1, BLOCK, iqw), lambda b, i: (b, i, CB_DIQ * LANES // iqw)),
                  pl.BlockSpec((1, L, D_IDX_DIM), lambda b, i: (b, 0, 0)),
                  pl.BlockSpec((1, BLOCK, D_IDX_HEADS), lambda b, i: (b, i, 0)),
                  pl.BlockSpec((1, BLOCK, qw), lambda b, i: (b, i, CB_DQ * LANES // qw)),
                  pl.BlockSpec((1, L, HEAD_DIM), lambda b, i: (b, 0, CB_DK)),
                  pl.BlockSpec((1, L, HEAD_DIM), lambda b, i: (b, 0, CB_DV))],
        out_specs=pl.BlockSpec((1, BLOCK, qw), lambda b, i: (b, i, 0)),
        out_shape=jax.ShapeDtypeStruct((B, L, qw), BF16),
        compiler_params=_cparams(("parallel", "arbitrary")),
        name="indexed_sparse_attention",
    )(slopes, y3, ik, iw, y3, y3, y3)


def _compress_kernel(x_ref, pos_ref, w1_ref, w2_ref, o_ref):
    x = x_ref[0, 0].astype(F32)
    nxt = pltpu.roll(x, x.shape[0] - 1, 0)
    blk = jnp.concatenate([x, nxt], axis=1) + pos_ref[0]
    h = jnp.dot(blk.astype(BF16), w1_ref[0], preferred_element_type=F32)
    h = jax.nn.gelu(h, approximate=True)
    o_ref[0, 0] = jnp.dot(h.astype(BF16), w2_ref[0], preferred_element_type=F32).astype(o_ref.dtype)


def _compress(xc, pos, w1, w2):
    B, _, n_chunk, cw = xc.shape
    G = B_KV_GROUPS
    return pl.pallas_call(
        _compress_kernel,
        grid=(2 * G, B),
        in_specs=[pl.BlockSpec((1, 1, n_chunk, cw), lambda a, b: (b, a, 0, 0)),
                  pl.BlockSpec((1, 1, 2 * cw), lambda a, b: (a // G, 0, 0)),
                  pl.BlockSpec((1, 2 * cw, B_CMP_HIDDEN), lambda a, b: (a // G, 0, 0)),
                  pl.BlockSpec((1, B_CMP_HIDDEN, HEAD_DIM), lambda a, b: (a // G, 0, 0))],
        out_specs=pl.BlockSpec((1, 1, n_chunk, HEAD_DIM), lambda a, b: (b, a, 0, 0)),
        out_shape=jax.ShapeDtypeStruct((B, 2 * G, n_chunk, HEAD_DIM), BF16),
        compiler_params=_cparams(("arbitrary", "arbitrary")),
        name="nsa_compress",
    )(xc, pos, w1, w2)


def _nsa_kernel(slope_ref, *refs, n_cmp, n_slc, window, n_prev):
    G, R = B_KV_GROUPS, B_GROUP_SIZE
    q_refs, ks_refs, vs_refs, kw_refs, vw_refs = (refs[n * G:(n + 1) * G] for n in range(5))
    ckv_ref, gl_ref, o_ref, sel_ref, cmp_ref, win_ref = refs[5 * G:]
    i = pl.program_id(1)
    nb = ks_refs[0].shape[1] // BLOCK
    gate = _sigmoid(gl_ref[0])
    hd = lambda h: slice(h * HEAD_DIM, (h + 1) * HEAD_DIM)

    kwb = min(n_prev + 1, nb)
    wlen = kwb * BLOCK
    start = pl.multiple_of(jnp.maximum(i - n_prev, 0) * BLOCK, BLOCK)
    wq = i * BLOCK + lax.broadcasted_iota(jnp.int32, (BLOCK, wlen), 0)
    wk = start + lax.broadcasted_iota(jnp.int32, (BLOCK, wlen), 1)
    wdist = wq - wk
    wmask = jnp.where((wdist >= 0) & (wdist <= window), 0.0, NEG)
    wdistf = wdist.astype(F32)
    for g in range(G):
        kwin = kw_refs[g][0, pl.ds(start, wlen), :].astype(BF16)
        vwin = vw_refs[g][0, pl.ds(start, wlen), :].astype(BF16)
        for r in range(R):
            h = g * R + r
            q = q_refs[g][0, :, hd(r)].astype(BF16)
            s2 = _dot_nt(q, kwin) * (ATTN_SCALE * LOG2E) - (slope_ref[h] * LOG2E) * wdistf + wmask
            win_ref[:, hd(h)] = gate[:, 2 * B_HEADS + h:2 * B_HEADS + h + 1] * _softmax2_pv(s2, vwin)

    nc = ckv_ref.shape[2]
    t = i * BLOCK + lax.broadcasted_iota(jnp.int32, (BLOCK, nc), 0)
    n = lax.broadcasted_iota(jnp.int32, (BLOCK, nc), 1)
    dist_c = t - (n * B_CMP_STRIDE + B_CMP_LEN - 1)
    mask = (dist_c >= 0) & (n < n_cmp)
    distf = dist_c.astype(F32)

    nn = lax.broadcasted_iota(jnp.int32, (nc, LANES), 0)
    jj = lax.broadcasted_iota(jnp.int32, (nc, LANES), 1)
    off = nn - B_SEL_RATIO * jj + 1
    w = jnp.where((off == 0) | (off == B_SEL_RATIO), 1.0, jnp.where((off > 0) & (off < B_SEL_RATIO), 2.0, 0.0))
    w = jnp.where((nn < n_cmp) & (jj < n_slc), w, 0.0).astype(F32)
    tq = i * BLOCK + lax.broadcasted_iota(jnp.int32, (BLOCK, LANES), 0)
    j = lax.broadcasted_iota(jnp.int32, (BLOCK, LANES), 1)
    cur = tq >> int(np.log2(B_SEL_LEN))
    forced = (j == 0) | (j == cur) | (j == cur - 1)
    jf = j.astype(F32)

    for g in range(G):
        kc = ckv_ref[0, g]
        vc = ckv_ref[0, G + g]
        imp = jnp.zeros((BLOCK, nc), F32)
        for r in range(R):
            h = g * R + r
            q = q_refs[g][0, :, hd(r)].astype(BF16)
            s = _dot_nt(q, kc) * ATTN_SCALE
            s = jnp.where(mask, s - slope_ref[h] * distf, NEG)
            m = jnp.max(s, axis=-1, keepdims=True)
            e = jnp.where(mask, jnp.exp(s - m), 0.0)
            p = e / jnp.maximum(jnp.sum(e, axis=-1, keepdims=True), 1e-30)
            cmp_ref[:, hd(h)] = gate[:, h:h + 1] * jnp.dot(p.astype(BF16), vc, preferred_element_type=F32)
            imp = imp + p
        p_slc = jnp.dot(imp, w, preferred_element_type=F32, precision=lax.Precision.HIGHEST)
        score = jnp.where(forced, 1e9, jnp.where(j <= cur, p_slc, -1e9))
        score = jnp.where(j < n_slc, score, -3e38)
        sel = jnp.zeros((BLOCK, LANES), F32)
        for _ in range(min(B_N_SEL, n_slc)):
            m = jnp.max(score, axis=-1, keepdims=True)
            first = jnp.min(jnp.where(score == m, jf, float(LANES)), axis=-1, keepdims=True)
            hit = jf == first
            sel = jnp.where(hit, 1.0, sel)
            score = jnp.where(hit, -3e38, score)
        sel_ref[g] = sel.astype(sel_ref.dtype)

    def body(kw):
        jj = lax.broadcasted_iota(jnp.int32, (LANES, kw), 0)
        ss = lax.broadcasted_iota(jnp.int32, (LANES, kw), 1)
        expand = jnp.where((ss >> int(np.log2(B_SEL_LEN))) == jj, 1.0, 0.0).astype(BF16)
        qpos = i * BLOCK + lax.broadcasted_iota(jnp.int32, (BLOCK, kw), 0)
        kpos = lax.broadcasted_iota(jnp.int32, (BLOCK, kw), 1)
        dist = qpos - kpos
        distf = dist.astype(F32)
        for g in range(G):
            picked = jnp.dot(sel_ref[g], expand, preferred_element_type=F32) > 0.5
            maskadd = jnp.where(picked & (dist >= 0), 0.0, NEG)
            k = ks_refs[g][0, :kw, :].astype(BF16)
            v = vs_refs[g][0, :kw, :].astype(BF16)
            for r in range(R):
                h = g * R + r
                q = q_refs[g][0, :, hd(r)].astype(BF16)
                s2 = _dot_nt(q, k) * (ATTN_SCALE * LOG2E) - (slope_ref[h] * LOG2E) * distf + maskadd
                o_slc = gate[:, B_HEADS + h:B_HEADS + h + 1] * _softmax2_pv(s2, v)
                o_ref[0, :, hd(h)] = (cmp_ref[:, hd(h)] + o_slc + win_ref[:, hd(h)]).astype(o_ref.dtype)

    _for_causal_class(i, nb, body)


def _nsa_attention(y3, cmp_kv, gate_logits, slopes, n_cmp, n_slc):
    B, L, _ = y3.shape
    G, R = B_KV_GROUPS, B_GROUP_SIZE
    nc = cmp_kv.shape[2]
    w_steps = B_WINDOW - 1
    q_specs = [pl.BlockSpec((1, BLOCK, R * HEAD_DIM), lambda b, i, g=g: (b, i, CB_BQ // R + g)) for g in range(G)]
    kv_specs = [pl.BlockSpec((1, L, HEAD_DIM), lambda b, i, c=CB_BKV + (br * 2 + kv) * G + g: (b, 0, c))
                for br in (1, 2) for kv in range(2) for g in range(G)]
    hw = B_HEADS * HEAD_DIM
    return pl.pallas_call(
        functools.partial(_nsa_kernel, n_cmp=n_cmp, n_slc=n_slc, window=w_steps, n_prev=-(-w_steps // BLOCK)),
        grid=(B, L // BLOCK),
        in_specs=[pl.BlockSpec(memory_space=pltpu.SMEM)] + q_specs + kv_specs +
                 [pl.BlockSpec((1, 2 * G, nc, HEAD_DIM), lambda b, i: (b, 0, 0, 0)),
                  pl.BlockSpec((1, BLOCK, gate_logits.shape[2]), lambda b, i: (b, i, 0))],
        out_specs=pl.BlockSpec((1, BLOCK, hw), lambda b, i: (b, i, 0)),
        out_shape=jax.ShapeDtypeStruct((B, L, hw), BF16),
        scratch_shapes=[pltpu.VMEM((G, BLOCK, LANES), BF16), pltpu.VMEM((BLOCK, hw), F32),
                        pltpu.VMEM((BLOCK, hw), F32)],
        compiler_params=_cparams(("parallel", "arbitrary")),
        name="nsa_attention",
    )(slopes, *([y3] * (5 * G)), cmp_kv, gate_logits)


def _native_sparse_attention(y3, gate_logits, cmp_w1, cmp_w2, cmp_pos, slopes_all):
    B, L, _ = y3.shape
    G, R = B_KV_GROUPS, B_GROUP_SIZE
    slopes = slopes_all[np.array(B_SLOPE_IDX)]
    n_chunk = L // B_CMP_STRIDE
    n_cmp = n_chunk - B_CMP_LEN // B_CMP_STRIDE + 1
    n_slc = L // B_SEL_LEN
    assert B_CMP_LEN == 2 * B_CMP_STRIDE and n_chunk % 8 == 0 and n_slc <= LANES

    xc = y3[:, :, CB_BKV * LANES:(CB_BKV + 2 * G) * LANES].reshape(B, L, 2 * G, HEAD_DIM)
    xc = xc.transpose(0, 2, 1, 3).reshape(B, 2 * G, n_chunk, B_CMP_STRIDE * HEAD_DIM)
    cmp_kv = _compress(xc, cmp_pos.reshape(2, 1, B_CMP_LEN * HEAD_DIM), cmp_w1.astype(BF16), cmp_w2.astype(BF16))

    o = _nsa_attention(y3, cmp_kv, gate_logits.reshape(B, L, -1), slopes, n_cmp, n_slc)
    return o.reshape(B * L, B_HEADS * HEAD_DIM)


def _gated_merge_kernel(x_ref, *refs, nk):
    nbr = N_BRANCH
    wg, o, w, bias = refs[:nbr], refs[nbr:2 * nbr], refs[2 * nbr:3 * nbr], refs[3 * nbr:4 * nbr]
    out_ref, acc = refs[4 * nbr], refs[4 * nbr + 1:]

    def logits(c):
        return jnp.dot(x_ref[...], wg[c][...], preferred_element_type=F32)

    def first():
        for c in range(nbr):
            acc[0][c] = logits(c)

    def middle():
        for c in range(nbr):
            acc[0][c] += logits(c)

    def last(has_acc):
        merged = None
        for c in range(nbr):
            z = (acc[0][c] + logits(c)) if has_acc else logits(c)
            term = _sigmoid(z + bias[c][...]) * jnp.dot(o[c][...], w[c][...], preferred_element_type=F32)
            merged = term if merged is None else merged + term
        out_ref[...] = merged.astype(out_ref.dtype)

    _k_steps(nk, first, middle, last)


def _gated_merge(xb, w_gate, b_gate, branch_outs, branch_ws):
    M, K = xb.shape
    D = branch_ws[0].shape[1]
    tm, tn, tk = _tile(M, 1024), _tile(D, 512), _tile(K, 1024)
    nj, nk = D // tn, K // tk
    wg_specs = [pl.BlockSpec((tk, tn), lambda i, j, k, c=c: (k, c * nj + j)) for c in range(N_BRANCH)]
    o_specs = [pl.BlockSpec((tm, o.shape[1]), lambda i, j, k: (i, 0)) for o in branch_outs]
    w_specs = [pl.BlockSpec((w.shape[0], tn), lambda i, j, k: (0, j)) for w in branch_ws]
    b_specs = [pl.BlockSpec((1, tn), lambda i, j, k, c=c: (0, c * nj + j)) for c in range(N_BRANCH)]
    return pl.pallas_call(
        functools.partial(_gated_merge_kernel, nk=nk),
        grid=(M // tm, nj, nk),
        in_specs=[pl.BlockSpec((tm, tk), lambda i, j, k: (i, k))] + wg_specs + o_specs + w_specs + b_specs,
        out_specs=pl.BlockSpec((tm, tn), lambda i, j, k: (i, j)),
        out_shape=jax.ShapeDtypeStruct((M, D), BF16),
        scratch_shapes=[pltpu.VMEM((N_BRANCH, tm, tn), F32)] if nk > 1 else [],
        compiler_params=_cparams(("parallel", "parallel", "arbitrary")),
        name="gated_merge",
    )(xb, *([w_gate] * N_BRANCH), *branch_outs, *branch_ws, *([b_gate.reshape(1, -1)] * N_BRANCH))


def _attn_weight(w_in):
    D = w_in.shape[0]
    sizes = (A_HEADS * HEAD_DIM,) * 3 + (B_HEADS * HEAD_DIM, 3 * 2 * B_KV_GROUPS * HEAD_DIM, 3 * B_HEADS) + \
            (C_HEADS * HEAD_DIM,) * 3 + (C_HEADS,) + (D_HEADS * HEAD_DIM, HEAD_DIM, HEAD_DIM) + \
            (D_IDX_HEADS * D_IDX_DIM, D_IDX_DIM, D_IDX_HEADS)
    offs = np.concatenate([[0], np.cumsum(sizes)])
    (a_q, a_k, a_v, b_q, b_kv, b_g, c_q, c_k, c_v, c_f, d_q, d_k, d_v, d_iq, d_ik, d_iw) = [
        w_in[:, offs[n]:offs[n + 1]] for n in range(len(sizes))]
    misc = jnp.zeros((D, 2 * LANES), w_in.dtype)
    misc = misc.at[:, MISC_BG:MISC_BG + 3 * B_HEADS].set(b_g)
    misc = misc.at[:, MISC_CF:MISC_CF + C_HEADS].set(c_f)
    misc = misc.at[:, MISC_IK:MISC_IK + D_IDX_DIM].set(d_ik)
    misc = misc.at[:, MISC_IW:MISC_IW + D_IDX_HEADS].set(d_iw)
    parts = [a_q, a_k, a_v, b_q, b_kv, c_q, c_k, c_v, d_q, d_iq, d_k, d_v, misc]
    w = jnp.concatenate([p.astype(BF16) for p in parts], axis=1)
    assert w.shape[1] == N_CB * LANES
    return w, int(offs[-1])


def _hybrid_mixer(xb, B, L, w_in, b_forget, b_gate, cmp_w1, cmp_w2, cmp_pos, w_branch, w_out):
    M, D = xb.shape
    slopes_all = jnp.exp2(-8.0 * jnp.arange(1, N_ALIBI + 1, dtype=F32) / N_ALIBI)
    w_attn, gate_off = _attn_weight(w_in)
    y3 = _matmul(xb, w_attn, BF16).reshape(B, L, N_CB * LANES)
    misc = _matmul(xb, w_attn[:, CB_MISC * LANES:(CB_MISC + 1) * LANES], F32)

    bias = jnp.zeros((1, LANES), F32).at[0, MISC_CF:MISC_CF + C_HEADS].set(b_forget)
    c = _logsig_cumsum(misc.reshape(B, L, LANES), bias)[:, :, MISC_CF:MISC_CF + C_HEADS].transpose(0, 2, 1)

    o_a = _dilated_attention(y3, slopes_all)
    o_b = _native_sparse_attention(y3, misc[:, MISC_BG:MISC_BG + 3 * B_HEADS], cmp_w1, cmp_w2, cmp_pos, slopes_all)
    o_c = _forgetting_attention(y3, c[..., None], c[:, :, None, :])
    o_d = _indexed_sparse_attention(y3, misc[:, MISC_IK:MISC_IK + D_IDX_DIM].reshape(B, L, D_IDX_DIM),
                                    misc[:, MISC_IW:MISC_IW + D_IDX_HEADS].reshape(B, L, D_IDX_HEADS), slopes_all)

    sizes = (A_HEADS_PER_PAIR * HEAD_DIM, B_HEADS * HEAD_DIM, C_HEADS * HEAD_DIM, D_HEADS * HEAD_DIM)
    offs = np.concatenate([[0], np.cumsum(sizes)])
    ws = [w_branch[offs[n]:offs[n + 1]] for n in range(N_BRANCH)]
    outs = [o_a, o_b, o_c.reshape(M, -1), o_d.reshape(M, -1)]
    merged = _gated_merge(xb, w_in[:, gate_off:], b_gate, outs, ws)
    return _matmul(merged, w_out, F32)


def kernel(x, ln_g, ln_b, ffn1_w_gate, ffn1_w_up, ffn1_w_down, w_in, b_forget, b_gate, cmp_w1, cmp_w2, cmp_pos,
           w_branch, w_out, ffn2_w_gate, ffn2_w_up, ffn2_w_down):
    B, L, D = x.shape
    assert L % BLOCK == 0 and D % LANES == 0
    xf = x.reshape(B * L, D)
    xb = xf.astype(BF16)

    def ffn(xf, xb, wg, wu, wd, l, g, b):
        h = _ffn_up(xb, _cast_bf16(wg, l), _cast_bf16(wu, l))
        y = _matmul(h, _cast_bf16(wd, l), F32)
        return _add_ln(xf, y, g, b, 0.5)

    for l in range(ln_g.shape[0]):
        xf, xb = ffn(xf, xb, ffn1_w_gate, ffn1_w_up, ffn1_w_down, l, ln_g[l, 0], ln_b[l, 0])
        y = _hybrid_mixer(xb, B, L, _cast_bf16(w_in, l), b_forget[l], b_gate[l], cmp_w1[l], cmp_w2[l], cmp_pos[l],
                          _cast_bf16(w_branch, l), _cast_bf16(w_out, l))
        xf, xb = _add_ln(xf, y, ln_g[l, 1], ln_b[l, 1], 1.0)
        xf, xb = ffn(xf, xb, ffn2_w_gate, ffn2_w_up, ffn2_w_down, l, ln_g[l, 2], ln_b[l, 2])
    return xf.reshape(B, L, D)
```

```python
import functools

import numpy as np
import jax
import jax.numpy as jnp
from jax import lax
from jax.experimental import pallas as pl
from jax.experimental.pallas import tpu as pltpu

F32 = jnp.float32
BF16 = jnp.bfloat16

HEAD_DIM = 128
BLOCK = 128
LANES = 128
NEG = -1e30
ATTN_SCALE = HEAD_DIM ** -0.5
LOG2E = 1.4426950408889634
LN2 = 0.6931471805599453
DEPTH = 2
ALPHA = (2 * DEPTH) ** 0.25
LN_EPS = 1e-5

A_PAIRS = ((128, 1), (512, 4), (2048, 16))
A_HEADS_PER_PAIR = 4
A_HEADS = A_HEADS_PER_PAIR * len(A_PAIRS)
B_HEADS = 8
B_KV_GROUPS = 2
B_GROUP_SIZE = B_HEADS // B_KV_GROUPS
B_CMP_LEN = 32
B_CMP_STRIDE = 16
B_SEL_LEN = 64
B_SEL_RATIO = B_SEL_LEN // B_CMP_STRIDE
B_N_SEL = 8
B_WINDOW = 512
B_CMP_HIDDEN = 512
C_HEADS = 8
D_HEADS = 8
D_IDX_HEADS = 8
D_IDX_DIM = 64
D_TOPK = 256
N_BRANCH = 4
N_ALIBI = A_HEADS + B_HEADS + D_HEADS
A_SLOPE_IDX = (0, 1, 2, 3, 12, 13, 14, 15, 24, 25, 26, 27)
B_SLOPE_IDX = (4, 5, 6, 7, 8, 9, 10, 11)
D_SLOPE_IDX = (16, 17, 18, 19, 20, 21, 22, 23)

CB_A = 0
CB_BQ = 36
CB_BKV = 44
CB_CQ, CB_CK, CB_CV = 56, 64, 72
CB_DQ = 80
CB_DIQ = 88
CB_DK, CB_DV = 92, 93
CB_MISC = 94
N_CB = 96
MISC_BG, MISC_CF, MISC_IK, MISC_IW = 0, 24, 32, 96

CAUSAL_CLASSES = 8
VMEM_LIMIT = 56 * 1024 * 1024


def _cparams(sem):
    return pltpu.CompilerParams(dimension_semantics=sem, vmem_limit_bytes=VMEM_LIMIT)


def _tile(dim, pref):
    return pref if dim % pref == 0 else dim


def _dot_nt(a, b):
    return lax.dot_general(a, b, (((1,), (1,)), ((), ())), preferred_element_type=F32)


def _softmax2_pv(s2, v):
    m = jnp.max(s2, axis=-1, keepdims=True)
    e = jnp.exp2(s2 - m)
    den = jnp.maximum(jnp.sum(e, axis=-1, keepdims=True), 1e-30)
    return jnp.dot(e.astype(BF16), v, preferred_element_type=F32) / den


def _for_causal_class(i, nb, body):
    n_cls = CAUSAL_CLASSES if nb % CAUSAL_CLASSES == 0 else 1
    per = nb // n_cls
    for c in range(n_cls):
        pl.when((i >= c * per) & (i < (c + 1) * per))(functools.partial(body, (c + 1) * per * BLOCK))


def _cast_kernel(x_ref, o_ref):
    o_ref[...] = x_ref[0].astype(o_ref.dtype)


def _cast_bf16(w_stack, l):
    _, R, C = w_stack.shape
    tr = R
    for cand in (1024, 512, 256, 128, 64, 32, 16):
        if R % cand == 0:
            tr = cand
            if cand * C * 4 <= 8 * 1024 * 1024:
                break
    return pl.pallas_call(
        _cast_kernel,
        grid=(R // tr,),
        in_specs=[pl.BlockSpec((1, tr, C), lambda i: (l, i, 0))],
        out_specs=pl.BlockSpec((tr, C), lambda i: (i, 0)),
        out_shape=jax.ShapeDtypeStruct((R, C), BF16),
        compiler_params=_cparams(("parallel",)),
        name="cast_bf16",
    )(w_stack)


def _sigmoid(z):
    return 0.5 * jnp.tanh(0.5 * z) + 0.5


def _k_steps(nk, first, middle, last):
    k = pl.program_id(2)
    if nk == 1:
        last(False)
        return
    pl.when(k == 0)(first)
    if nk > 2:
        pl.when((k > 0) & (k < nk - 1))(middle)
    pl.when(k == nk - 1)(functools.partial(last, True))


def _mm_kernel(x_ref, w_ref, o_ref, *acc, nk):
    def prod():
        return jnp.dot(x_ref[...], w_ref[...], preferred_element_type=F32)

    def first():
        acc[0][...] = prod()

    def middle():
        acc[0][...] += prod()

    def last(has_acc):
        o_ref[...] = ((acc[0][...] + prod()) if has_acc else prod()).astype(o_ref.dtype)

    _k_steps(nk, first, middle, last)


def _matmul(x, w, out_dtype, tiles=(1024, 1024, 4096)):
    M, K = x.shape
    N = w.shape[1]
    tm, tn, tk = _tile(M, tiles[0]), _tile(N, tiles[1]), _tile(K, tiles[2])
    nk = K // tk
    return pl.pallas_call(
        functools.partial(_mm_kernel, nk=nk),
        grid=(M // tm, N // tn, nk),
        in_specs=[pl.BlockSpec((tm, tk), lambda i, j, k: (i, k)),
                  pl.BlockSpec((tk, tn), lambda i, j, k: (k, j))],
        out_specs=pl.BlockSpec((tm, tn), lambda i, j, k: (i, j)),
        out_shape=jax.ShapeDtypeStruct((M, N), out_dtype),
        scratch_shapes=[pltpu.VMEM((tm, tn), F32)] if nk > 1 else [],
        compiler_params=_cparams(("parallel", "parallel", "arbitrary")),
        name="matmul",
    )(x, w)


def _ffn_up_kernel(x_ref, wg_ref, wu_ref, o_ref, *acc, nk):
    def prods():
        x = x_ref[...]
        return (jnp.dot(x, wg_ref[...], preferred_element_type=F32),
                jnp.dot(x, wu_ref[...], preferred_element_type=F32))

    def first():
        acc[0][...], acc[1][...] = prods()

    def middle():
        g, u = prods()
        acc[0][...] += g
        acc[1][...] += u

    def last(has_acc):
        g, u = prods()
        if has_acc:
            g, u = acc[0][...] + g, acc[1][...] + u
        o_ref[...] = (g * _sigmoid(g) * u).astype(o_ref.dtype)

    _k_steps(nk, first, middle, last)


def _ffn_up(x, wg, wu, tiles=(1024, 1024, 2048)):
    M, K = x.shape
    N = wg.shape[1]
    tm, tn, tk = _tile(M, tiles[0]), _tile(N, tiles[1]), _tile(K, tiles[2])
    nk = K // tk
    return pl.pallas_call(
        functools.partial(_ffn_up_kernel, nk=nk),
        grid=(M // tm, N // tn, nk),
        in_specs=[pl.BlockSpec((tm, tk), lambda i, j, k: (i, k)),
                  pl.BlockSpec((tk, tn), lambda i, j, k: (k, j)),
                  pl.BlockSpec((tk, tn), lambda i, j, k: (k, j))],
        out_specs=pl.BlockSpec((tm, tn), lambda i, j, k: (i, j)),
        out_shape=jax.ShapeDtypeStruct((M, N), BF16),
        scratch_shapes=[pltpu.VMEM((tm, tn), F32), pltpu.VMEM((tm, tn), F32)] if nk > 1 else [],
        compiler_params=_cparams(("parallel", "parallel", "arbitrary")),
        name="ffn_up",
    )(x, wg, wu)


def _add_ln_kernel(x_ref, y_ref, g_ref, b_ref, o_ref, ob_ref, *, cy):
    z = ALPHA * x_ref[...] + cy * y_ref[...]
    mu = jnp.mean(z, axis=-1, keepdims=True)
    zc = z - mu
    var = jnp.mean(zc * zc, axis=-1, keepdims=True)
    out = zc * lax.rsqrt(var + LN_EPS) * g_ref[...] + b_ref[...]
    o_ref[...] = out
    ob_ref[...] = out.astype(BF16)


def _add_ln(x, y, g, b, cy):
    M, D = x.shape
    tm = _tile(M, 256)
    row = pl.BlockSpec((tm, D), lambda i: (i, 0))
    vec = pl.BlockSpec((1, D), lambda i: (0, 0))
    return pl.pallas_call(
        functools.partial(_add_ln_kernel, cy=cy),
        grid=(M // tm,),
        in_specs=[row, row, vec, vec],
        out_specs=[row, row],
        out_shape=[jax.ShapeDtypeStruct((M, D), F32), jax.ShapeDtypeStruct((M, D), BF16)],
        compiler_params=_cparams(("parallel",)),
        name="add_ln",
    )(x, y, g.reshape(1, D), b.reshape(1, D))


def _banded_kernel(slope_ref, q_ref, k_ref, v_ref, o_ref, lse_ref, *, window, n_prev, step, nb, hp):
    g = pl.program_id(1)
    i = pl.program_id(2)
    kwb = min(n_prev + 1, nb)
    kw = kwb * BLOCK
    start = pl.multiple_of(jnp.maximum(i - n_prev, 0) * BLOCK, BLOCK)
    qpos = i * BLOCK + lax.broadcasted_iota(jnp.int32, (BLOCK, kw), 0)
    kpos = start + lax.broadcasted_iota(jnp.int32, (BLOCK, kw), 1)
    dist = qpos - kpos
    maskadd = jnp.where((dist >= 0) & (dist <= window), 0.0, NEG)
    krel = (step * lax.broadcasted_iota(jnp.int32, (1, kw), 1)).astype(F32)
    qrel = (step * (i * BLOCK - start + lax.broadcasted_iota(jnp.int32, (BLOCK, 1), 0))).astype(F32)
    for h in range(hp):
        cs = slice(h * HEAD_DIM, (h + 1) * HEAD_DIM)
        slope = slope_ref[g * hp + h]
        q = q_ref[0, :, cs].astype(BF16)
        kwin = k_ref[0, pl.ds(start, kw), cs].astype(BF16)
        vwin = v_ref[0, pl.ds(start, kw), cs].astype(BF16)
        s = _dot_nt(q, kwin) * ATTN_SCALE + (slope * krel + maskadd)
        m = jnp.max(s, axis=-1, keepdims=True)
        e = jnp.exp(s - m)
        den = jnp.maximum(jnp.sum(e, axis=-1, keepdims=True), 1e-30)
        o_ref[0, :, cs] = jnp.dot(e.astype(BF16), vwin, preferred_element_type=F32) / den
        lse_ref[0, 0, :, h:h + 1] = m + jnp.log(den) - slope * qrel


def _banded_attention(qa, ka, va, slopes, *, n_groups, hp, q_blk, k_blk, v_blk, window, n_prev, step):
    B, N, _ = qa.shape
    nb = N // BLOCK
    qspec = pl.BlockSpec((1, BLOCK, hp * HEAD_DIM), lambda b, g, i: (b, i, q_blk(g)))
    kspec = pl.BlockSpec((1, N, hp * HEAD_DIM), lambda b, g, i: (b, 0, k_blk(g)))
    vspec = pl.BlockSpec((1, N, hp * HEAD_DIM), lambda b, g, i: (b, 0, v_blk(g)))
    ospec = pl.BlockSpec((1, BLOCK, hp * HEAD_DIM), lambda b, g, i: (b, i, g))
    oshape = jax.ShapeDtypeStruct((B, N, n_groups * hp * HEAD_DIM), F32)
    lspec = pl.BlockSpec((1, 1, BLOCK, hp), lambda b, g, i: (b, g, i, 0))
    lshape = jax.ShapeDtypeStruct((B, n_groups, N, hp), F32)
    return pl.pallas_call(
        functools.partial(_banded_kernel, window=window, n_prev=n_prev, step=step, nb=nb, hp=hp),
        grid=(B, n_groups, nb),
        in_specs=[pl.BlockSpec(memory_space=pltpu.SMEM), qspec, kspec, vspec],
        out_specs=[ospec, lspec],
        out_shape=[oshape, lshape],
        compiler_params=_cparams(("parallel", "parallel", "arbitrary")),
        name="banded_attention",
    )(slopes, qa, ka, va)


def _a_combine_kernel(o0, o1, o2, l0, l1, l2, out_ref):
    a, b, c = l0[...], l1[...], l2[...]
    m = jnp.maximum(jnp.maximum(a, b), c)
    ea, eb, ec = jnp.exp(a - m), jnp.exp(b - m), jnp.exp(c - m)
    tot = ea + eb + ec
    wa, wb, wc = ea / tot, eb / tot, ec / tot
    for h in range(a.shape[1]):
        cs = slice(h * HEAD_DIM, (h + 1) * HEAD_DIM)
        out = wa[:, h:h + 1] * o0[:, cs] + wb[:, h:h + 1] * o1[:, cs] + wc[:, h:h + 1] * o2[:, cs]
        out_ref[:, cs] = out.astype(out_ref.dtype)


def _a_combine(outs, lses):
    M, C = outs[0].shape
    tm = _tile(M, 512)
    spec = pl.BlockSpec((tm, C), lambda i: (i, 0))
    lspec = pl.BlockSpec((tm, lses[0].shape[1]), lambda i: (i, 0))
    return pl.pallas_call(
        _a_combine_kernel,
        grid=(M // tm,),
        in_specs=[spec] * 3 + [lspec] * 3,
        out_specs=spec,
        out_shape=jax.ShapeDtypeStruct((M, C), BF16),
        compiler_params=_cparams(("parallel",)),
        name="a_combine",
    )(*outs, *lses)


def _dilated_attention(y3, slopes_all):
    B, L, _ = y3.shape
    slopes = slopes_all[np.array(A_SLOPE_IDX)]
    hpp = A_HEADS_PER_PAIR
    gw = 3 * hpp * HEAD_DIM
    outs, lses = [], []
    for g, (window, dil) in enumerate(A_PAIRS):
        n = L // dil
        assert n % BLOCK == 0
        steps = window // dil
        if dil == 1:
            ya, base = y3, CB_A // hpp + g
            blk = lambda part: (lambda r: base + part * (A_HEADS // hpp))
        else:
            cols = [y3[:, :, (CB_A + p * A_HEADS + g * hpp) * LANES:(CB_A + p * A_HEADS + (g + 1) * hpp) * LANES]
                    for p in range(3)]
            ya = jnp.concatenate(cols, axis=-1).reshape(B, n, dil * gw)
            blk = lambda part: (lambda r: r * 3 + part)
        o, lse = _banded_attention(ya, ya, ya, jnp.tile(slopes[g * hpp:(g + 1) * hpp], dil),
                                   n_groups=dil, hp=hpp, q_blk=blk(0), k_blk=blk(1), v_blk=blk(2),
                                   window=steps, n_prev=-(-steps // BLOCK), step=dil)
        outs.append(o.reshape(B * L, hpp * HEAD_DIM))
        lses.append(lse.transpose(0, 2, 1, 3).reshape(B * L, hpp))
    return _a_combine(outs, lses)


def _logsig_cumsum_kernel(z_ref, bias_ref, o_ref):
    L = z_ref.shape[1]
    row = lax.broadcasted_iota(jnp.int32, (BLOCK, BLOCK), 0)
    col = lax.broadcasted_iota(jnp.int32, (BLOCK, BLOCK), 1)
    tri = jnp.where(row >= col, 1.0, 0.0).astype(F32)
    carry = jnp.zeros((1, LANES), F32)
    for j in range(L // BLOCK):
        z = z_ref[0, j * BLOCK:(j + 1) * BLOCK, :] + bias_ref[...]
        ls = jnp.minimum(z, 0.0) - jnp.log(1.0 + jnp.exp(-jnp.abs(z)))
        c = jnp.dot(tri, ls, preferred_element_type=F32, precision=lax.Precision.HIGHEST) + carry
        o_ref[0, j * BLOCK:(j + 1) * BLOCK, :] = c
        carry = c[BLOCK - 1:BLOCK, :]


def _logsig_cumsum(z, bias):
    B, L, _ = z.shape
    spec = pl.BlockSpec((1, L, LANES), lambda b: (b, 0, 0))
    return pl.pallas_call(
        _logsig_cumsum_kernel,
        grid=(B,),
        in_specs=[spec, pl.BlockSpec((1, LANES), lambda b: (0, 0))],
        out_specs=spec,
        out_shape=jax.ShapeDtypeStruct((B, L, LANES), F32),
        compiler_params=_cparams(("parallel",)),
        name="logsig_cumsum",
    )(z, bias)


def _fox_kernel(q_ref, k_ref, v_ref, cq_ref, ck_ref, o_ref):
    i = pl.program_id(1)
    nb = k_ref.shape[1] // BLOCK

    def body(kw):
        qpos = i * BLOCK + lax.broadcasted_iota(jnp.int32, (BLOCK, kw), 0)
        kpos = lax.broadcasted_iota(jnp.int32, (BLOCK, kw), 1)
        maskadd = jnp.where(qpos >= kpos, 0.0, NEG)
        for h in range(C_HEADS):
            cs = slice(h * HEAD_DIM, (h + 1) * HEAD_DIM)
            q = q_ref[0, :, cs].astype(BF16)
            k = k_ref[0, :kw, cs].astype(BF16)
            v = v_ref[0, :kw, cs].astype(BF16)
            s2 = _dot_nt(q, k) * (ATTN_SCALE * LOG2E) + cq_ref[0, h] * LOG2E - ck_ref[0, h, :, :kw] * LOG2E
            o_ref[0, :, cs] = _softmax2_pv(s2 + maskadd, v).astype(o_ref.dtype)

    _for_causal_class(i, nb, body)


def _forgetting_attention(y3, c_col, c_row):
    B, L, _ = y3.shape
    nb = L // BLOCK
    cw = C_HEADS * HEAD_DIM
    return pl.pallas_call(
        _fox_kernel,
        grid=(B, nb),
        in_specs=[pl.BlockSpec((1, BLOCK, cw), lambda b, i: (b, i, CB_CQ // C_HEADS)),
                  pl.BlockSpec((1, L, cw), lambda b, i: (b, 0, CB_CK // C_HEADS)),
                  pl.BlockSpec((1, L, cw), lambda b, i: (b, 0, CB_CV // C_HEADS)),
                  pl.BlockSpec((1, C_HEADS, BLOCK, 1), lambda b, i: (b, 0, i, 0)),
                  pl.BlockSpec((1, C_HEADS, 1, L), lambda b, i: (b, 0, 0, 0))],
        out_specs=pl.BlockSpec((1, BLOCK, cw), lambda b, i: (b, i, 0)),
        out_shape=jax.ShapeDtypeStruct((B, L, cw), BF16),
        compiler_params=_cparams(("parallel", "arbitrary")),
        name="forgetting_attention",
    )(y3, y3, y3, c_col, c_row)


def _order_key(x):
    bits = lax.bitcast_convert_type(x, jnp.int32)
    return bits ^ ((bits >> 31) & jnp.int32(0x7FFFFFFF))


def _kth_largest_key(key, k):
    rows = key.shape[0]

    def count_ge(t):
        return jnp.sum(jnp.where(key >= t, 1.0, 0.0), axis=-1, keepdims=True)

    t0 = jnp.where(count_ge(jnp.zeros((rows, 1), jnp.int32)) >= k,
                   jnp.int32(0), jnp.int32(-2 ** 31)) + jnp.zeros((rows, 1), jnp.int32)

    def two_bits(it, t):
        lo = jnp.left_shift(jnp.int32(1), 29 - 2 * it)
        c1, c2 = t | lo, t | (lo + lo)
        c3 = c2 | lo
        n1, n2, n3 = count_ge(c1), count_ge(c2), count_ge(c3)
        return jnp.where(n3 >= k, c3, jnp.where(n2 >= k, c2, jnp.where(n1 >= k, c1, t)))

    t = lax.fori_loop(0, 15, two_bits, t0)
    last = t | jnp.int32(1)
    return jnp.where(count_ge(last) >= k, last, t)


def _dsa_kernel(slope_ref, iq_ref, ik_ref, iw_ref, q_ref, k_ref, v_ref, o_ref, *, n_top):
    i = pl.program_id(1)
    nb = k_ref.shape[1] // BLOCK
    iw = iw_ref[0]

    def body(kw):
        ik = ik_ref[0, :kw, :].astype(BF16)
        score = jnp.zeros((BLOCK, kw), F32)
        for h in range(D_IDX_HEADS):
            iq = iq_ref[0, :, h * D_IDX_DIM:(h + 1) * D_IDX_DIM].astype(BF16)
            rel = jnp.maximum(_dot_nt(iq, ik), 0.0)
            score = score + iw[:, h:h + 1] * rel
        qpos = i * BLOCK + lax.broadcasted_iota(jnp.int32, (BLOCK, kw), 0)
        kpos = lax.broadcasted_iota(jnp.int32, (BLOCK, kw), 1)
        dist = qpos - kpos
        causal = dist >= 0
        key = _order_key(jnp.where(causal, score, NEG))

        thr = _kth_largest_key(key, float(n_top))
        gt = key > thr
        eq = key == thr
        need = float(n_top) - jnp.sum(jnp.where(gt, 1.0, 0.0), axis=-1, keepdims=True)
        row = lax.broadcasted_iota(jnp.int32, (BLOCK, BLOCK), 0)
        col = lax.broadcasted_iota(jnp.int32, (BLOCK, BLOCK), 1)
        upper = jnp.where(row <= col, 1.0, 0.0).astype(BF16)
        eqf = jnp.where(eq, 1.0, 0.0).astype(BF16)
        carry = jnp.zeros((BLOCK, 1), F32)
        rank_tiles = []
        for j in range(kw // BLOCK):
            rank = carry + jnp.dot(eqf[:, j * BLOCK:(j + 1) * BLOCK], upper, preferred_element_type=F32)
            rank_tiles.append(rank)
            carry = rank[:, BLOCK - 1:BLOCK]
        rank = jnp.concatenate(rank_tiles, axis=1)
        mask = (gt | (eq & (rank <= need))) & causal
        maskadd = jnp.where(mask, 0.0, NEG)

        k = k_ref[0, :kw, :].astype(BF16)
        v = v_ref[0, :kw, :].astype(BF16)
        distf = dist.astype(F32)
        for h in range(D_HEADS):
            q = q_ref[0, :, h * HEAD_DIM:(h + 1) * HEAD_DIM].astype(BF16)
            s2 = _dot_nt(q, k) * (ATTN_SCALE * LOG2E) - (slope_ref[h] * LOG2E) * distf + maskadd
            o_ref[0, :, h * HEAD_DIM:(h + 1) * HEAD_DIM] = _softmax2_pv(s2, v).astype(o_ref.dtype)

    _for_causal_class(i, nb, body)


def _indexed_sparse_attention(y3, ik, iw, slopes_all):
    B, L, _ = y3.shape
    nb = L // BLOCK
    n_top = min(D_TOPK, L // 4)
    slopes = slopes_all[np.array(D_SLOPE_IDX)]
    qw, iqw = D_HEADS * HEAD_DIM, D_IDX_HEADS * D_IDX_DIM
    return pl.pallas_call(
        functools.partial(_dsa_kernel, n_top=n_top),
        grid=(B, nb),
        in_specs=[pl.BlockSpec(memory_space=pltpu.SMEM),
                  pl.BlockSpec((1, BLOCK, iqw), lambda b, i: (b, i, CB_DIQ * LANES // iqw)),
                  pl.BlockSpec((1, L, D_IDX_DIM), lambda b, i: (b, 0, 0)),
                  pl.BlockSpec((1, BLOCK, D_IDX_HEADS), lambda b, i: (b, i, 0)),
                  pl.BlockSpec((1, BLOCK, qw), lambda b, i: (b, i, CB_DQ * LANES // qw)),
                  pl.BlockSpec((1, L, HEAD_DIM), lambda b, i: (b, 0, CB_DK)),
                  pl.BlockSpec((1, L, HEAD_DIM), lambda b, i: (b, 0, CB_DV))],
        out_specs=pl.BlockSpec((1, BLOCK, qw), lambda b, i: (b, i, 0)),
        out_shape=jax.ShapeDtypeStruct((B, L, qw), BF16),
        compiler_params=_cparams(("parallel", "arbitrary")),
        name="indexed_sparse_attention",
    )(slopes, y3, ik, iw, y3, y3, y3)


def _compress_kernel(x_ref, pos_ref, w1_ref, w2_ref, o_ref):
    x = x_ref[0, 0].astype(F32)
    nxt = pltpu.roll(x, x.shape[0] - 1, 0)
    blk = jnp.concatenate([x, nxt], axis=1) + pos_ref[0]
    h = jnp.dot(blk.astype(BF16), w1_ref[0], preferred_element_type=F32)
    h = jax.nn.gelu(h, approximate=True)
    o_ref[0, 0] = jnp.dot(h.astype(BF16), w2_ref[0], preferred_element_type=F32).astype(o_ref.dtype)


def _compress(xc, pos, w1, w2):
    B, _, n_chunk, cw = xc.shape
    G = B_KV_GROUPS
    return pl.pallas_call(
        _compress_kernel,
        grid=(2 * G, B),
        in_specs=[pl.BlockSpec((1, 1, n_chunk, cw), lambda a, b: (b, a, 0, 0)),
                  pl.BlockSpec((1, 1, 2 * cw), lambda a, b: (a // G, 0, 0)),
                  pl.BlockSpec((1, 2 * cw, B_CMP_HIDDEN), lambda a, b: (a // G, 0, 0)),
                  pl.BlockSpec((1, B_CMP_HIDDEN, HEAD_DIM), lambda a, b: (a // G, 0, 0))],
        out_specs=pl.BlockSpec((1, 1, n_chunk, HEAD_DIM), lambda a, b: (b, a, 0, 0)),
        out_shape=jax.ShapeDtypeStruct((B, 2 * G, n_chunk, HEAD_DIM), BF16),
        compiler_params=_cparams(("arbitrary", "arbitrary")),
        name="nsa_compress",
    )(xc, pos, w1, w2)


def _nsa_kernel(slope_ref, *refs, n_cmp, n_slc, window, n_prev):
    G, R = B_KV_GROUPS, B_GROUP_SIZE
    q_refs, ks_refs, vs_refs, kw_refs, vw_refs = (refs[n * G:(n + 1) * G] for n in range(5))
    ckv_ref, gl_ref, o_ref, sel_ref, cmp_ref, win_ref = refs[5 * G:]
    i = pl.program_id(1)
    nb = ks_refs[0].shape[1] // BLOCK
    gate = _sigmoid(gl_ref[0])
    hd = lambda h: slice(h * HEAD_DIM, (h + 1) * HEAD_DIM)

    kwb = min(n_prev + 1, nb)
    wlen = kwb * BLOCK
    start = pl.multiple_of(jnp.maximum(i - n_prev, 0) * BLOCK, BLOCK)
    wq = i * BLOCK + lax.broadcasted_iota(jnp.int32, (BLOCK, wlen), 0)
    wk = start + lax.broadcasted_iota(jnp.int32, (BLOCK, wlen), 1)
    wdist = wq - wk
    wmask = jnp.where((wdist >= 0) & (wdist <= window), 0.0, NEG)
    wdistf = wdist.astype(F32)
    for g in range(G):
        kwin = kw_refs[g][0, pl.ds(start, wlen), :].astype(BF16)
        vwin = vw_refs[g][0, pl.ds(start, wlen), :].astype(BF16)
        for r in range(R):
            h = g * R + r
            q = q_refs[g][0, :, hd(r)].astype(BF16)
            s2 = _dot_nt(q, kwin) * (ATTN_SCALE * LOG2E) - (slope_ref[h] * LOG2E) * wdistf + wmask
            win_ref[:, hd(h)] = gate[:, 2 * B_HEADS + h:2 * B_HEADS + h + 1] * _softmax2_pv(s2, vwin)

    nc = ckv_ref.shape[2]
    t = i * BLOCK + lax.broadcasted_iota(jnp.int32, (BLOCK, nc), 0)
    n = lax.broadcasted_iota(jnp.int32, (BLOCK, nc), 1)
    dist_c = t - (n * B_CMP_STRIDE + B_CMP_LEN - 1)
    mask = (dist_c >= 0) & (n < n_cmp)
    distf = dist_c.astype(F32)

    nn = lax.broadcasted_iota(jnp.int32, (nc, LANES), 0)
    jj = lax.broadcasted_iota(jnp.int32, (nc, LANES), 1)
    off = nn - B_SEL_RATIO * jj + 1
    w = jnp.where((off == 0) | (off == B_SEL_RATIO), 1.0, jnp.where((off > 0) & (off < B_SEL_RATIO), 2.0, 0.0))
    w = jnp.where((nn < n_cmp) & (jj < n_slc), w, 0.0).astype(F32)
    tq = i * BLOCK + lax.broadcasted_iota(jnp.int32, (BLOCK, LANES), 0)
    j = lax.broadcasted_iota(jnp.int32, (BLOCK, LANES), 1)
    cur = tq >> int(np.log2(B_SEL_LEN))
    forced = (j == 0) | (j == cur) | (j == cur - 1)
    jf = j.astype(F32)

    for g in range(G):
        kc = ckv_ref[0, g]
        vc = ckv_ref[0, G + g]
        imp = jnp.zeros((BLOCK, nc), F32)
        for r in range(R):
            h = g * R + r
            q = q_refs[g][0, :, hd(r)].astype(BF16)
            s = _dot_nt(q, kc) * ATTN_SCALE
            s = jnp.where(mask, s - slope_ref[h] * distf, NEG)
            m = jnp.max(s, axis=-1, keepdims=True)
            e = jnp.where(mask, jnp.exp(s - m), 0.0)
            p = e / jnp.maximum(jnp.sum(e, axis=-1, keepdims=True), 1e-30)
            cmp_ref[:, hd(h)] = gate[:, h:h + 1] * jnp.dot(p.astype(BF16), vc, preferred_element_type=F32)
            imp = imp + p
        p_slc = jnp.dot(imp, w, preferred_element_type=F32, precision=lax.Precision.HIGHEST)
        score = jnp.where(forced, 1e9, jnp.where(j <= cur, p_slc, -1e9))
        score = jnp.where(j < n_slc, score, -3e38)
        sel = jnp.zeros((BLOCK, LANES), F32)
        for _ in range(min(B_N_SEL, n_slc)):
            m = jnp.max(score, axis=-1, keepdims=True)
            first = jnp.min(jnp.where(score == m, jf, float(LANES)), axis=-1, keepdims=True)
            hit = jf == first
            sel = jnp.where(hit, 1.0, sel)
            score = jnp.where(hit, -3e38, score)
        sel_ref[g] = sel.astype(sel_ref.dtype)

    def body(kw):
        jj = lax.broadcasted_iota(jnp.int32, (LANES, kw), 0)
        ss = lax.broadcasted_iota(jnp.int32, (LANES, kw), 1)
        expand = jnp.where((ss >> int(np.log2(B_SEL_LEN))) == jj, 1.0, 0.0).astype(BF16)
        qpos = i * BLOCK + lax.broadcasted_iota(jnp.int32, (BLOCK, kw), 0)
        kpos = lax.broadcasted_iota(jnp.int32, (BLOCK, kw), 1)
        dist = qpos - kpos
        distf = dist.astype(F32)
        for g in range(G):
            picked = jnp.dot(sel_ref[g], expand, preferred_element_type=F32) > 0.5
            maskadd = jnp.where(picked & (dist >= 0), 0.0, NEG)
            k = ks_refs[g][0, :kw, :].astype(BF16)
            v = vs_refs[g][0, :kw, :].astype(BF16)
            for r in range(R):
                h = g * R + r
                q = q_refs[g][0, :, hd(r)].astype(BF16)
                s2 = _dot_nt(q, k) * (ATTN_SCALE * LOG2E) - (slope_ref[h] * LOG2E) * distf + maskadd
                o_slc = gate[:, B_HEADS + h:B_HEADS + h + 1] * _softmax2_pv(s2, v)
                o_ref[0, :, hd(h)] = (cmp_ref[:, hd(h)] + o_slc + win_ref[:, hd(h)]).astype(o_ref.dtype)

    _for_causal_class(i, nb, body)


def _nsa_attention(y3, cmp_kv, gate_logits, slopes, n_cmp, n_slc):
    B, L, _ = y3.shape
    G, R = B_KV_GROUPS, B_GROUP_SIZE
    nc = cmp_kv.shape[2]
    w_steps = B_WINDOW - 1
    q_specs = [pl.BlockSpec((1, BLOCK, R * HEAD_DIM), lambda b, i, g=g: (b, i, CB_BQ // R + g)) for g in range(G)]
    kv_specs = [pl.BlockSpec((1, L, HEAD_DIM), lambda b, i, c=CB_BKV + (br * 2 + kv) * G + g: (b, 0, c))
                for br in (1, 2) for kv in range(2) for g in range(G)]
    hw = B_HEADS * HEAD_DIM
    return pl.pallas_call(
        functools.partial(_nsa_kernel, n_cmp=n_cmp, n_slc=n_slc, window=w_steps, n_prev=-(-w_steps // BLOCK)),
        grid=(B, L // BLOCK),
        in_specs=[pl.BlockSpec(memory_space=pltpu.SMEM)] + q_specs + kv_specs +
                 [pl.BlockSpec((1, 2 * G, nc, HEAD_DIM), lambda b, i: (b, 0, 0, 0)),
                  pl.BlockSpec((1, BLOCK, gate_logits.shape[2]), lambda b, i: (b, i, 0))],
        out_specs=pl.BlockSpec((1, BLOCK, hw), lambda b, i: (b, i, 0)),
        out_shape=jax.ShapeDtypeStruct((B, L, hw), BF16),
        scratch_shapes=[pltpu.VMEM((G, BLOCK, LANES), BF16), pltpu.VMEM((BLOCK, hw), F32),
                        pltpu.VMEM((BLOCK, hw), F32)],
        compiler_params=_cparams(("parallel", "arbitrary")),
        name="nsa_attention",
    )(slopes, *([y3] * (5 * G)), cmp_kv, gate_logits)


def _native_sparse_attention(y3, gate_logits, cmp_w1, cmp_w2, cmp_pos, slopes_all):
    B, L, _ = y3.shape
    G, R = B_KV_GROUPS, B_GROUP_SIZE
    slopes = slopes_all[np.array(B_SLOPE_IDX)]
    n_chunk = L // B_CMP_STRIDE
    n_cmp = n_chunk - B_CMP_LEN // B_CMP_STRIDE + 1
    n_slc = L // B_SEL_LEN
    assert B_CMP_LEN == 2 * B_CMP_STRIDE and n_chunk % 8 == 0 and n_slc <= LANES

    xc = y3[:, :, CB_BKV * LANES:(CB_BKV + 2 * G) * LANES].reshape(B, L, 2 * G, HEAD_DIM)
    xc = xc.transpose(0, 2, 1, 3).reshape(B, 2 * G, n_chunk, B_CMP_STRIDE * HEAD_DIM)
    cmp_kv = _compress(xc, cmp_pos.reshape(2, 1, B_CMP_LEN * HEAD_DIM), cmp_w1.astype(BF16), cmp_w2.astype(BF16))

    o = _nsa_attention(y3, cmp_kv, gate_logits.reshape(B, L, -1), slopes, n_cmp, n_slc)
    return o.reshape(B * L, B_HEADS * HEAD_DIM)


def _gated_merge_kernel(x_ref, *refs, nk):
    nbr = N_BRANCH
    wg, o, w, bias = refs[:nbr], refs[nbr:2 * nbr], refs[2 * nbr:3 * nbr], refs[3 * nbr:4 * nbr]
    out_ref, acc = refs[4 * nbr], refs[4 * nbr + 1:]

    def logits(c):
        return jnp.dot(x_ref[...], wg[c][...], preferred_element_type=F32)

    def first():
        for c in range(nbr):
            acc[0][c] = logits(c)

    def middle():
        for c in range(nbr):
            acc[0][c] += logits(c)

    def last(has_acc):
        merged = None
        for c in range(nbr):
            z = (acc[0][c] + logits(c)) if has_acc else logits(c)
            term = _sigmoid(z + bias[c][...]) * jnp.dot(o[c][...], w[c][...], preferred_element_type=F32)
            merged = term if merged is None else merged + term
        out_ref[...] = merged.astype(out_ref.dtype)

    _k_steps(nk, first, middle, last)


def _gated_merge(xb, w_gate, b_gate, branch_outs, branch_ws):
    M, K = xb.shape
    D = branch_ws[0].shape[1]
    tm, tn, tk = _tile(M, 1024), _tile(D, 512), _tile(K, 1024)
    nj, nk = D // tn, K // tk
    wg_specs = [pl.BlockSpec((tk, tn), lambda i, j, k, c=c: (k, c * nj + j)) for c in range(N_BRANCH)]
    o_specs = [pl.BlockSpec((tm, o.shape[1]), lambda i, j, k: (i, 0)) for o in branch_outs]
    w_specs = [pl.BlockSpec((w.shape[0], tn), lambda i, j, k: (0, j)) for w in branch_ws]
    b_specs = [pl.BlockSpec((1, tn), lambda i, j, k, c=c: (0, c * nj + j)) for c in range(N_BRANCH)]
    return pl.pallas_call(
        functools.partial(_gated_merge_kernel, nk=nk),
        grid=(M // tm, nj, nk),
        in_specs=[pl.BlockSpec((tm, tk), lambda i, j, k: (i, k))] + wg_specs + o_specs + w_specs + b_specs,
        out_specs=pl.BlockSpec((tm, tn), lambda i, j, k: (i, j)),
        out_shape=jax.ShapeDtypeStruct((M, D), BF16),
        scratch_shapes=[pltpu.VMEM((N_BRANCH, tm, tn), F32)] if nk > 1 else [],
        compiler_params=_cparams(("parallel", "parallel", "arbitrary")),
        name="gated_merge",
    )(xb, *([w_gate] * N_BRANCH), *branch_outs, *branch_ws, *([b_gate.reshape(1, -1)] * N_BRANCH))


def _attn_weight(w_in):
    D = w_in.shape[0]
    sizes = (A_HEADS * HEAD_DIM,) * 3 + (B_HEADS * HEAD_DIM, 3 * 2 * B_KV_GROUPS * HEAD_DIM, 3 * B_HEADS) + \
            (C_HEADS * HEAD_DIM,) * 3 + (C_HEADS,) + (D_HEADS * HEAD_DIM, HEAD_DIM, HEAD_DIM) + \
            (D_IDX_HEADS * D_IDX_DIM, D_IDX_DIM, D_IDX_HEADS)
    offs = np.concatenate([[0], np.cumsum(sizes)])
    (a_q, a_k, a_v, b_q, b_kv, b_g, c_q, c_k, c_v, c_f, d_q, d_k, d_v, d_iq, d_ik, d_iw) = [
        w_in[:, offs[n]:offs[n + 1]] for n in range(len(sizes))]
    misc = jnp.zeros((D, 2 * LANES), w_in.dtype)
    misc = misc.at[:, MISC_BG:MISC_BG + 3 * B_HEADS].set(b_g)
    misc = misc.at[:, MISC_CF:MISC_CF + C_HEADS].set(c_f)
    misc = misc.at[:, MISC_IK:MISC_IK + D_IDX_DIM].set(d_ik)
    misc = misc.at[:, MISC_IW:MISC_IW + D_IDX_HEADS].set(d_iw)
    parts = [a_q, a_k, a_v, b_q, b_kv, c_q, c_k, c_v, d_q, d_iq, d_k, d_v, misc]
    w = jnp.concatenate([p.astype(BF16) for p in parts], axis=1)
    assert w.shape[1] == N_CB * LANES
    return w, int(offs[-1])


def _hybrid_mixer(xb, B, L, w_in, b_forget, b_gate, cmp_w1, cmp_w2, cmp_pos, w_branch, w_out):
    M, D = xb.shape
    slopes_all = jnp.exp2(-8.0 * jnp.arange(1, N_ALIBI + 1, dtype=F32) / N_ALIBI)
    w_attn, gate_off = _attn_weight(w_in)
    y3 = _matmul(xb, w_attn, BF16).reshape(B, L, N_CB * LANES)
    misc = _matmul(xb, w_attn[:, CB_MISC * LANES:(CB_MISC + 1) * LANES], F32)

    bias = jnp.zeros((1, LANES), F32).at[0, MISC_CF:MISC_CF + C_HEADS].set(b_forget)
    c = _logsig_cumsum(misc.reshape(B, L, LANES), bias)[:, :, MISC_CF:MISC_CF + C_HEADS].transpose(0, 2, 1)

    o_a = _dilated_attention(y3, slopes_all)
    o_b = _native_sparse_attention(y3, misc[:, MISC_BG:MISC_BG + 3 * B_HEADS], cmp_w1, cmp_w2, cmp_pos, slopes_all)
    o_c = _forgetting_attention(y3, c[..., None], c[:, :, None, :])
    o_d = _indexed_sparse_attention(y3, misc[:, MISC_IK:MISC_IK + D_IDX_DIM].reshape(B, L, D_IDX_DIM),
                                    misc[:, MISC_IW:MISC_IW + D_IDX_HEADS].reshape(B, L, D_IDX_HEADS), slopes_all)

    sizes = (A_HEADS_PER_PAIR * HEAD_DIM, B_HEADS * HEAD_DIM, C_HEADS * HEAD_DIM, D_HEADS * HEAD_DIM)
    offs = np.concatenate([[0], np.cumsum(sizes)])
    ws = [w_branch[offs[n]:offs[n + 1]] for n in range(N_BRANCH)]
    outs = [o_a, o_b, o_c.reshape(M, -1), o_d.reshape(M, -1)]
    merged = _gated_merge(xb, w_in[:, gate_off:].astype(BF16), b_gate, outs, ws)
    return _matmul(merged, w_out, F32)


def kernel(x, ln_g, ln_b, ffn1_w_gate, ffn1_w_up, ffn1_w_down, w_in, b_forget, b_gate, cmp_w1, cmp_w2, cmp_pos,
           w_branch, w_out, ffn2_w_gate, ffn2_w_up, ffn2_w_down):
    B, L, D = x.shape
    assert L % BLOCK == 0 and D % LANES == 0
    xf = x.reshape(B * L, D)
    xb = xf.astype(BF16)

    def ffn(xf, xb, wg, wu, wd, l, g, b):
        h = _ffn_up(xb, _cast_bf16(wg, l), _cast_bf16(wu, l))
        y = _matmul(h, _cast_bf16(wd, l), F32)
        return _add_ln(xf, y, g, b, 0.5)

    for l in range(ln_g.shape[0]):
        xf, xb = ffn(xf, xb, ffn1_w_gate, ffn1_w_up, ffn1_w_down, l, ln_g[l, 0], ln_b[l, 0])
        y = _hybrid_mixer(xb, B, L, w_in[l], b_forget[l], b_gate[l], cmp_w1[l], cmp_w2[l], cmp_pos[l],
                          _cast_bf16(w_branch, l), _cast_bf16(w_out, l))
        xf, xb = _add_ln(xf, y, ln_g[l, 1], ln_b[l, 1], 1.0)
        xf, xb = ffn(xf, xb, ffn2_w_gate, ffn2_w_up, ffn2_w_down, l, ln_g[l, 2], ln_b[l, 2])
    return xf.reshape(B, L, D)
```

```python
import functools

import numpy as np
import jax
import jax.numpy as jnp
from jax import lax
from jax.experimental import pallas as pl
from jax.experimental.pallas import tpu as pltpu

F32 = jnp.float32
BF16 = jnp.bfloat16

HEAD_DIM = 128
BLOCK = 128
LANES = 128
NEG = -1e30
ATTN_SCALE = HEAD_DIM ** -0.5
LOG2E = 1.4426950408889634
LN2 = 0.6931471805599453
DEPTH = 2
ALPHA = (2 * DEPTH) ** 0.25
LN_EPS = 1e-5

A_PAIRS = ((128, 1), (512, 4), (2048, 16))
A_HEADS_PER_PAIR = 4
A_HEADS = A_HEADS_PER_PAIR * len(A_PAIRS)
B_HEADS = 8
B_KV_GROUPS = 2
B_GROUP_SIZE = B_HEADS // B_KV_GROUPS
B_CMP_LEN = 32
B_CMP_STRIDE = 16
B_SEL_LEN = 64
B_SEL_RATIO = B_SEL_LEN // B_CMP_STRIDE
B_N_SEL = 8
B_WINDOW = 512
B_CMP_HIDDEN = 512
C_HEADS = 8
D_HEADS = 8
D_IDX_HEADS = 8
D_IDX_DIM = 64
D_TOPK = 256
D_HEAD_STACK = 4
N_BRANCH = 4
N_ALIBI = A_HEADS + B_HEADS + D_HEADS
A_SLOPE_IDX = (0, 1, 2, 3, 12, 13, 14, 15, 24, 25, 26, 27)
B_SLOPE_IDX = (4, 5, 6, 7, 8, 9, 10, 11)
D_SLOPE_IDX = (16, 17, 18, 19, 20, 21, 22, 23)

CB_A = 0
CB_BQ = 36
CB_BKV = 44
CB_CQ, CB_CK, CB_CV = 56, 64, 72
CB_DQ = 80
CB_DIQ = 88
CB_DK, CB_DV = 92, 93
CB_MISC = 94
N_CB = 96
MISC_BG, MISC_CF, MISC_IK, MISC_IW = 0, 24, 32, 96

CAUSAL_CLASSES = 8
VMEM_LIMIT = 56 * 1024 * 1024


def _cparams(sem):
    return pltpu.CompilerParams(dimension_semantics=sem, vmem_limit_bytes=VMEM_LIMIT)


def _tile(dim, pref):
    return pref if dim % pref == 0 else dim


def _dot_nt(a, b):
    return lax.dot_general(a, b, (((1,), (1,)), ((), ())), preferred_element_type=F32)


def _softmax2_pv(s2, v):
    m = jnp.max(s2, axis=-1, keepdims=True)
    e = jnp.exp2(s2 - m)
    den = jnp.maximum(jnp.sum(e, axis=-1, keepdims=True), 1e-30)
    return jnp.dot(e.astype(BF16), v, preferred_element_type=F32) / den


def _stacked_softmax2_pv(q_stack, k, v, logits2, n):
    qk = _dot_nt(q_stack, k)
    es, dens = [], []
    for r in range(n):
        s2 = logits2(r, qk[r * BLOCK:(r + 1) * BLOCK])
        e = jnp.exp2(s2 - jnp.max(s2, axis=-1, keepdims=True))
        dens.append(jnp.maximum(jnp.sum(e, axis=-1, keepdims=True), 1e-30))
        es.append(e.astype(BF16))
    pv = jnp.dot(jnp.concatenate(es, axis=0), v, preferred_element_type=F32)
    return [pv[r * BLOCK:(r + 1) * BLOCK] / dens[r] for r in range(n)]


def _for_causal_class(i, nb, body):
    n_cls = CAUSAL_CLASSES if nb % CAUSAL_CLASSES == 0 else 1
    per = nb // n_cls
    for c in range(n_cls):
        pl.when((i >= c * per) & (i < (c + 1) * per))(functools.partial(body, (c + 1) * per * BLOCK))


def _cast_kernel(x_ref, o_ref):
    o_ref[...] = x_ref[0].astype(o_ref.dtype)


def _cast_bf16(w_stack, l):
    _, R, C = w_stack.shape
    tr = R
    for cand in (1024, 512, 256, 128, 64, 32, 16):
        if R % cand == 0:
            tr = cand
            if cand * C * 4 <= 8 * 1024 * 1024:
                break
    return pl.pallas_call(
        _cast_kernel,
        grid=(R // tr,),
        in_specs=[pl.BlockSpec((1, tr, C), lambda i: (l, i, 0))],
        out_specs=pl.BlockSpec((tr, C), lambda i: (i, 0)),
        out_shape=jax.ShapeDtypeStruct((R, C), BF16),
        compiler_params=_cparams(("parallel",)),
        name="cast_bf16",
    )(w_stack)


def _sigmoid(z):
    return 0.5 * jnp.tanh(0.5 * z) + 0.5


def _k_steps(nk, first, middle, last):
    k = pl.program_id(2)
    if nk == 1:
        last(False)
        return
    pl.when(k == 0)(first)
    if nk > 2:
        pl.when((k > 0) & (k < nk - 1))(middle)
    pl.when(k == nk - 1)(functools.partial(last, True))


def _mm_kernel(x_ref, w_ref, o_ref, *acc, nk):
    def prod():
        return jnp.dot(x_ref[...], w_ref[...], preferred_element_type=F32)

    def first():
        acc[0][...] = prod()

    def middle():
        acc[0][...] += prod()

    def last(has_acc):
        o_ref[...] = ((acc[0][...] + prod()) if has_acc else prod()).astype(o_ref.dtype)

    _k_steps(nk, first, middle, last)


def _matmul(x, w, out_dtype, tiles=(1024, 1024, 4096)):
    M, K = x.shape
    N = w.shape[1]
    tm, tn, tk = _tile(M, tiles[0]), _tile(N, tiles[1]), _tile(K, tiles[2])
    nk = K // tk
    return pl.pallas_call(
        functools.partial(_mm_kernel, nk=nk),
        grid=(M // tm, N // tn, nk),
        in_specs=[pl.BlockSpec((tm, tk), lambda i, j, k: (i, k)),
                  pl.BlockSpec((tk, tn), lambda i, j, k: (k, j))],
        out_specs=pl.BlockSpec((tm, tn), lambda i, j, k: (i, j)),
        out_shape=jax.ShapeDtypeStruct((M, N), out_dtype),
        scratch_shapes=[pltpu.VMEM((tm, tn), F32)] if nk > 1 else [],
        compiler_params=_cparams(("parallel", "parallel", "arbitrary")),
        name="matmul",
    )(x, w)


def _ffn_up_kernel(x_ref, wg_ref, wu_ref, o_ref, *acc, nk):
    def prods():
        x = x_ref[...]
        return (jnp.dot(x, wg_ref[...], preferred_element_type=F32),
                jnp.dot(x, wu_ref[...], preferred_element_type=F32))

    def first():
        acc[0][...], acc[1][...] = prods()

    def middle():
        g, u = prods()
        acc[0][...] += g
        acc[1][...] += u

    def last(has_acc):
        g, u = prods()
        if has_acc:
            g, u = acc[0][...] + g, acc[1][...] + u
        o_ref[...] = (g * _sigmoid(g) * u).astype(o_ref.dtype)

    _k_steps(nk, first, middle, last)


def _ffn_up(x, wg, wu, tiles=(1024, 1024, 2048)):
    M, K = x.shape
    N = wg.shape[1]
    tm, tn, tk = _tile(M, tiles[0]), _tile(N, tiles[1]), _tile(K, tiles[2])
    nk = K // tk
    return pl.pallas_call(
        functools.partial(_ffn_up_kernel, nk=nk),
        grid=(M // tm, N // tn, nk),
        in_specs=[pl.BlockSpec((tm, tk), lambda i, j, k: (i, k)),
                  pl.BlockSpec((tk, tn), lambda i, j, k: (k, j)),
                  pl.BlockSpec((tk, tn), lambda i, j, k: (k, j))],
        out_specs=pl.BlockSpec((tm, tn), lambda i, j, k: (i, j)),
        out_shape=jax.ShapeDtypeStruct((M, N), BF16),
        scratch_shapes=[pltpu.VMEM((tm, tn), F32), pltpu.VMEM((tm, tn), F32)] if nk > 1 else [],
        compiler_params=_cparams(("parallel", "parallel", "arbitrary")),
        name="ffn_up",
    )(x, wg, wu)


def _add_ln_kernel(x_ref, y_ref, g_ref, b_ref, o_ref, ob_ref, *, cy):
    z = ALPHA * x_ref[...] + cy * y_ref[...]
    mu = jnp.mean(z, axis=-1, keepdims=True)
    zc = z - mu
    var = jnp.mean(zc * zc, axis=-1, keepdims=True)
    out = zc * lax.rsqrt(var + LN_EPS) * g_ref[...] + b_ref[...]
    o_ref[...] = out
    ob_ref[...] = out.astype(BF16)


def _add_ln(x, y, g, b, cy):
    M, D = x.shape
    tm = _tile(M, 256)
    row = pl.BlockSpec((tm, D), lambda i: (i, 0))
    vec = pl.BlockSpec((1, D), lambda i: (0, 0))
    return pl.pallas_call(
        functools.partial(_add_ln_kernel, cy=cy),
        grid=(M // tm,),
        in_specs=[row, row, vec, vec],
        out_specs=[row, row],
        out_shape=[jax.ShapeDtypeStruct((M, D), F32), jax.ShapeDtypeStruct((M, D), BF16)],
        compiler_params=_cparams(("parallel",)),
        name="add_ln",
    )(x, y, g.reshape(1, D), b.reshape(1, D))


def _banded_kernel(slope_ref, q_ref, k_ref, v_ref, o_ref, lse_ref, *, window, n_prev, step, nb, hp):
    g = pl.program_id(1)
    i = pl.program_id(2)
    kwb = min(n_prev + 1, nb)
    kw = kwb * BLOCK
    start = pl.multiple_of(jnp.maximum(i - n_prev, 0) * BLOCK, BLOCK)
    qpos = i * BLOCK + lax.broadcasted_iota(jnp.int32, (BLOCK, kw), 0)
    kpos = start + lax.broadcasted_iota(jnp.int32, (BLOCK, kw), 1)
    dist = qpos - kpos
    maskadd = jnp.where((dist >= 0) & (dist <= window), 0.0, NEG)
    krel = (step * lax.broadcasted_iota(jnp.int32, (1, kw), 1)).astype(F32)
    qrel = (step * (i * BLOCK - start + lax.broadcasted_iota(jnp.int32, (BLOCK, 1), 0))).astype(F32)
    for h in range(hp):
        cs = slice(h * HEAD_DIM, (h + 1) * HEAD_DIM)
        slope = slope_ref[g * hp + h]
        q = q_ref[0, :, cs].astype(BF16)
        kwin = k_ref[0, pl.ds(start, kw), cs].astype(BF16)
        vwin = v_ref[0, pl.ds(start, kw), cs].astype(BF16)
        s = _dot_nt(q, kwin) * ATTN_SCALE + (slope * krel + maskadd)
        m = jnp.max(s, axis=-1, keepdims=True)
        e = jnp.exp(s - m)
        den = jnp.maximum(jnp.sum(e, axis=-1, keepdims=True), 1e-30)
        o_ref[0, :, cs] = jnp.dot(e.astype(BF16), vwin, preferred_element_type=F32) / den
        lse_ref[0, 0, :, h:h + 1] = m + jnp.log(den) - slope * qrel


def _banded_attention(qa, ka, va, slopes, *, n_groups, hp, q_blk, k_blk, v_blk, window, n_prev, step):
    B, N, _ = qa.shape
    nb = N // BLOCK
    qspec = pl.BlockSpec((1, BLOCK, hp * HEAD_DIM), lambda b, g, i: (b, i, q_blk(g)))
    kspec = pl.BlockSpec((1, N, hp * HEAD_DIM), lambda b, g, i: (b, 0, k_blk(g)))
    vspec = pl.BlockSpec((1, N, hp * HEAD_DIM), lambda b, g, i: (b, 0, v_blk(g)))
    ospec = pl.BlockSpec((1, BLOCK, hp * HEAD_DIM), lambda b, g, i: (b, i, g))
    oshape = jax.ShapeDtypeStruct((B, N, n_groups * hp * HEAD_DIM), F32)
    lspec = pl.BlockSpec((1, 1, BLOCK, hp), lambda b, g, i: (b, g, i, 0))
    lshape = jax.ShapeDtypeStruct((B, n_groups, N, hp), F32)
    return pl.pallas_call(
        functools.partial(_banded_kernel, window=window, n_prev=n_prev, step=step, nb=nb, hp=hp),
        grid=(B, n_groups, nb),
        in_specs=[pl.BlockSpec(memory_space=pltpu.SMEM), qspec, kspec, vspec],
        out_specs=[ospec, lspec],
        out_shape=[oshape, lshape],
        compiler_params=_cparams(("parallel", "parallel", "arbitrary")),
        name="banded_attention",
    )(slopes, qa, ka, va)


def _a_combine_kernel(o0, o1, o2, l0, l1, l2, out_ref):
    a, b, c = l0[...], l1[...], l2[...]
    m = jnp.maximum(jnp.maximum(a, b), c)
    ea, eb, ec = jnp.exp(a - m), jnp.exp(b - m), jnp.exp(c - m)
    tot = ea + eb + ec
    wa, wb, wc = ea / tot, eb / tot, ec / tot
    for h in range(a.shape[1]):
        cs = slice(h * HEAD_DIM, (h + 1) * HEAD_DIM)
        out = wa[:, h:h + 1] * o0[:, cs] + wb[:, h:h + 1] * o1[:, cs] + wc[:, h:h + 1] * o2[:, cs]
        out_ref[:, cs] = out.astype(out_ref.dtype)


def _a_combine(outs, lses):
    M, C = outs[0].shape
    tm = _tile(M, 512)
    spec = pl.BlockSpec((tm, C), lambda i: (i, 0))
    lspec = pl.BlockSpec((tm, lses[0].shape[1]), lambda i: (i, 0))
    return pl.pallas_call(
        _a_combine_kernel,
        grid=(M // tm,),
        in_specs=[spec] * 3 + [lspec] * 3,
        out_specs=spec,
        out_shape=jax.ShapeDtypeStruct((M, C), BF16),
        compiler_params=_cparams(("parallel",)),
        name="a_combine",
    )(*outs, *lses)


def _dilated_attention(y3, slopes_all):
    B, L, _ = y3.shape
    slopes = slopes_all[np.array(A_SLOPE_IDX)]
    hpp = A_HEADS_PER_PAIR
    gw = 3 * hpp * HEAD_DIM
    outs, lses = [], []
    for g, (window, dil) in enumerate(A_PAIRS):
        n = L // dil
        assert n % BLOCK == 0
        steps = window // dil
        if dil == 1:
            ya, base = y3, CB_A // hpp + g
            blk = lambda part: (lambda r: base + part * (A_HEADS // hpp))
        else:
            cols = [y3[:, :, (CB_A + p * A_HEADS + g * hpp) * LANES:(CB_A + p * A_HEADS + (g + 1) * hpp) * LANES]
                    for p in range(3)]
            ya = jnp.concatenate(cols, axis=-1).reshape(B, n, dil * gw)
            blk = lambda part: (lambda r: r * 3 + part)
        o, lse = _banded_attention(ya, ya, ya, jnp.tile(slopes[g * hpp:(g + 1) * hpp], dil),
                                   n_groups=dil, hp=hpp, q_blk=blk(0), k_blk=blk(1), v_blk=blk(2),
                                   window=steps, n_prev=-(-steps // BLOCK), step=dil)
        outs.append(o.reshape(B * L, hpp * HEAD_DIM))
        lses.append(lse.transpose(0, 2, 1, 3).reshape(B * L, hpp))
    return _a_combine(outs, lses)


def _logsig_cumsum_kernel(z_ref, bias_ref, o_ref):
    L = z_ref.shape[1]
    row = lax.broadcasted_iota(jnp.int32, (BLOCK, BLOCK), 0)
    col = lax.broadcasted_iota(jnp.int32, (BLOCK, BLOCK), 1)
    tri = jnp.where(row >= col, 1.0, 0.0).astype(F32)
    carry = jnp.zeros((1, LANES), F32)
    for j in range(L // BLOCK):
        z = z_ref[0, j * BLOCK:(j + 1) * BLOCK, :] + bias_ref[...]
        ls = jnp.minimum(z, 0.0) - jnp.log(1.0 + jnp.exp(-jnp.abs(z)))
        c = jnp.dot(tri, ls, preferred_element_type=F32, precision=lax.Precision.HIGHEST) + carry
        o_ref[0, j * BLOCK:(j + 1) * BLOCK, :] = c
        carry = c[BLOCK - 1:BLOCK, :]


def _logsig_cumsum(z, bias):
    B, L, _ = z.shape
    spec = pl.BlockSpec((1, L, LANES), lambda b: (b, 0, 0))
    return pl.pallas_call(
        _logsig_cumsum_kernel,
        grid=(B,),
        in_specs=[spec, pl.BlockSpec((1, LANES), lambda b: (0, 0))],
        out_specs=spec,
        out_shape=jax.ShapeDtypeStruct((B, L, LANES), F32),
        compiler_params=_cparams(("parallel",)),
        name="logsig_cumsum",
    )(z, bias)


def _fox_kernel(q_ref, k_ref, v_ref, cq_ref, ck_ref, o_ref):
    i = pl.program_id(1)
    nb = k_ref.shape[1] // BLOCK

    def body(kw):
        qpos = i * BLOCK + lax.broadcasted_iota(jnp.int32, (BLOCK, kw), 0)
        kpos = lax.broadcasted_iota(jnp.int32, (BLOCK, kw), 1)
        maskadd = jnp.where(qpos >= kpos, 0.0, NEG)
        for h in range(C_HEADS):
            cs = slice(h * HEAD_DIM, (h + 1) * HEAD_DIM)
            q = q_ref[0, :, cs].astype(BF16)
            k = k_ref[0, :kw, cs].astype(BF16)
            v = v_ref[0, :kw, cs].astype(BF16)
            s2 = _dot_nt(q, k) * (ATTN_SCALE * LOG2E) + cq_ref[0, h] * LOG2E - ck_ref[0, h, :, :kw] * LOG2E
            o_ref[0, :, cs] = _softmax2_pv(s2 + maskadd, v).astype(o_ref.dtype)

    _for_causal_class(i, nb, body)


def _forgetting_attention(y3, c_col, c_row):
    B, L, _ = y3.shape
    nb = L // BLOCK
    cw = C_HEADS * HEAD_DIM
    return pl.pallas_call(
        _fox_kernel,
        grid=(B, nb),
        in_specs=[pl.BlockSpec((1, BLOCK, cw), lambda b, i: (b, i, CB_CQ // C_HEADS)),
                  pl.BlockSpec((1, L, cw), lambda b, i: (b, 0, CB_CK // C_HEADS)),
                  pl.BlockSpec((1, L, cw), lambda b, i: (b, 0, CB_CV // C_HEADS)),
                  pl.BlockSpec((1, C_HEADS, BLOCK, 1), lambda b, i: (b, 0, i, 0)),
                  pl.BlockSpec((1, C_HEADS, 1, L), lambda b, i: (b, 0, 0, 0))],
        out_specs=pl.BlockSpec((1, BLOCK, cw), lambda b, i: (b, i, 0)),
        out_shape=jax.ShapeDtypeStruct((B, L, cw), BF16),
        compiler_params=_cparams(("parallel", "arbitrary")),
        name="forgetting_attention",
    )(y3, y3, y3, c_col, c_row)


def _order_key(x):
    bits = lax.bitcast_convert_type(x, jnp.int32)
    return bits ^ ((bits >> 31) & jnp.int32(0x7FFFFFFF))


def _kth_largest_key(key, k):
    rows = key.shape[0]

    def count_ge(t):
        return jnp.sum(jnp.where(key >= t, 1.0, 0.0), axis=-1, keepdims=True)

    t0 = jnp.where(count_ge(jnp.zeros((rows, 1), jnp.int32)) >= k,
                   jnp.int32(0), jnp.int32(-2 ** 31)) + jnp.zeros((rows, 1), jnp.int32)

    def two_bits(it, t):
        lo = jnp.left_shift(jnp.int32(1), 29 - 2 * it)
        c1, c2 = t | lo, t | (lo + lo)
        c3 = c2 | lo
        n1, n2, n3 = count_ge(c1), count_ge(c2), count_ge(c3)
        return jnp.where(n3 >= k, c3, jnp.where(n2 >= k, c2, jnp.where(n1 >= k, c1, t)))

    t = lax.fori_loop(0, 15, two_bits, t0)
    last = t | jnp.int32(1)
    return jnp.where(count_ge(last) >= k, last, t)


def _dsa_kernel(slope_ref, iq_ref, ik_ref, iw_ref, q_ref, k_ref, v_ref, o_ref, *, n_top):
    i = pl.program_id(1)
    nb = k_ref.shape[1] // BLOCK
    iw = iw_ref[0]

    def body(kw):
        ik = ik_ref[0, :kw, :].astype(BF16)
        score = jnp.zeros((BLOCK, kw), F32)
        hs = D_HEAD_STACK
        for h0 in range(0, D_IDX_HEADS, hs):
            iq_rows = jnp.concatenate([iq_ref[0, :, (h0 + r) * D_IDX_DIM:(h0 + r + 1) * D_IDX_DIM].astype(BF16)
                                       for r in range(hs)], axis=0)
            z = _dot_nt(iq_rows, ik)
            for r in range(hs):
                rel = jnp.maximum(z[r * BLOCK:(r + 1) * BLOCK], 0.0)
                score = score + iw[:, h0 + r:h0 + r + 1] * rel
        qpos = i * BLOCK + lax.broadcasted_iota(jnp.int32, (BLOCK, kw), 0)
        kpos = lax.broadcasted_iota(jnp.int32, (BLOCK, kw), 1)
        dist = qpos - kpos
        causal = dist >= 0
        key = _order_key(jnp.where(causal, score, NEG))

        thr = _kth_largest_key(key, float(n_top))
        gt = key > thr
        eq = key == thr
        need = float(n_top) - jnp.sum(jnp.where(gt, 1.0, 0.0), axis=-1, keepdims=True)
        row = lax.broadcasted_iota(jnp.int32, (BLOCK, BLOCK), 0)
        col = lax.broadcasted_iota(jnp.int32, (BLOCK, BLOCK), 1)
        upper = jnp.where(row <= col, 1.0, 0.0).astype(BF16)
        eqf = jnp.where(eq, 1.0, 0.0).astype(BF16)
        carry = jnp.zeros((BLOCK, 1), F32)
        rank_tiles = []
        for j in range(kw // BLOCK):
            rank = carry + jnp.dot(eqf[:, j * BLOCK:(j + 1) * BLOCK], upper, preferred_element_type=F32)
            rank_tiles.append(rank)
            carry = rank[:, BLOCK - 1:BLOCK]
        rank = jnp.concatenate(rank_tiles, axis=1)
        mask = (gt | (eq & (rank <= need))) & causal
        maskadd = jnp.where(mask, 0.0, NEG)

        k = k_ref[0, :kw, :].astype(BF16)
        v = v_ref[0, :kw, :].astype(BF16)
        distf = dist.astype(F32)
        hs = D_HEAD_STACK
        for h0 in range(0, D_HEADS, hs):
            q_rows = jnp.concatenate([q_ref[0, :, (h0 + r) * HEAD_DIM:(h0 + r + 1) * HEAD_DIM].astype(BF16)
                                      for r in range(hs)], axis=0)
            outs = _stacked_softmax2_pv(
                q_rows, k, v,
                lambda r, qk, h0=h0: (qk * (ATTN_SCALE * LOG2E) - (slope_ref[h0 + r] * LOG2E) * distf + maskadd), hs)
            for r in range(hs):
                o_ref[0, :, (h0 + r) * HEAD_DIM:(h0 + r + 1) * HEAD_DIM] = outs[r].astype(o_ref.dtype)

    _for_causal_class(i, nb, body)


def _indexed_sparse_attention(y3, ik, iw, slopes_all):
    B, L, _ = y3.shape
    nb = L // BLOCK
    n_top = min(D_TOPK, L // 4)
    slopes = slopes_all[np.array(D_SLOPE_IDX)]
    qw, iqw = D_HEADS * HEAD_DIM, D_IDX_HEADS * D_IDX_DIM
    return pl.pallas_call(
        functools.partial(_dsa_kernel, n_top=n_top),
        grid=(B, nb),
        in_specs=[pl.BlockSpec(memory_space=pltpu.SMEM),
                  pl.BlockSpec((1, BLOCK, iqw), lambda b, i: (b, i, CB_DIQ * LANES // iqw)),
                  pl.BlockSpec((1, L, D_IDX_DIM), lambda b, i: (b, 0, 0)),
                  pl.BlockSpec((1, BLOCK, D_IDX_HEADS), lambda b, i: (b, i, 0)),
                  pl.BlockSpec((1, BLOCK, qw), lambda b, i: (b, i, CB_DQ * LANES // qw)),
                  pl.BlockSpec((1, L, HEAD_DIM), lambda b, i: (b, 0, CB_DK)),
                  pl.BlockSpec((1, L, HEAD_DIM), lambda b, i: (b, 0, CB_DV))],
        out_specs=pl.BlockSpec((1, BLOCK, qw), lambda b, i: (b, i, 0)),
        out_shape=jax.ShapeDtypeStruct((B, L, qw), BF16),
        compiler_params=_cparams(("parallel", "arbitrary")),
        name="indexed_sparse_attention",
    )(slopes, y3, ik, iw, y3, y3, y3)


def _compress_kernel(x_ref, pos_ref, w1_ref, w2_ref, o_ref):
    x = x_ref[0, 0].astype(F32)
    nxt = pltpu.roll(x, x.shape[0] - 1, 0)
    blk = jnp.concatenate([x, nxt], axis=1) + pos_ref[0]
    h = jnp.dot(blk.astype(BF16), w1_ref[0], preferred_element_type=F32)
    h = jax.nn.gelu(h, approximate=True)
    o_ref[0, 0] = jnp.dot(h.astype(BF16), w2_ref[0], preferred_element_type=F32).astype(o_ref.dtype)


def _compress(xc, pos, w1, w2):
    B, _, n_chunk, cw = xc.shape
    G = B_KV_GROUPS
    return pl.pallas_call(
        _compress_kernel,
        grid=(2 * G, B),
        in_specs=[pl.BlockSpec((1, 1, n_chunk, cw), lambda a, b: (b, a, 0, 0)),
                  pl.BlockSpec((1, 1, 2 * cw), lambda a, b: (a // G, 0, 0)),
                  pl.BlockSpec((1, 2 * cw, B_CMP_HIDDEN), lambda a, b: (a // G, 0, 0)),
                  pl.BlockSpec((1, B_CMP_HIDDEN, HEAD_DIM), lambda a, b: (a // G, 0, 0))],
        out_specs=pl.BlockSpec((1, 1, n_chunk, HEAD_DIM), lambda a, b: (b, a, 0, 0)),
        out_shape=jax.ShapeDtypeStruct((B, 2 * G, n_chunk, HEAD_DIM), BF16),
        compiler_params=_cparams(("arbitrary", "arbitrary")),
        name="nsa_compress",
    )(xc, pos, w1, w2)


def _nsa_kernel(slope_ref, *refs, n_cmp, n_slc, window, n_prev):
    G, R = B_KV_GROUPS, B_GROUP_SIZE
    q_refs, ks_refs, vs_refs, kw_refs, vw_refs = (refs[n * G:(n + 1) * G] for n in range(5))
    ckv_ref, gl_ref, o_ref, sel_ref, cmp_ref, win_ref = refs[5 * G:]
    i = pl.program_id(1)
    nb = ks_refs[0].shape[1] // BLOCK
    gate = _sigmoid(gl_ref[0])
    hd = lambda h: slice(h * HEAD_DIM, (h + 1) * HEAD_DIM)

    kwb = min(n_prev + 1, nb)
    wlen = kwb * BLOCK
    start = pl.multiple_of(jnp.maximum(i - n_prev, 0) * BLOCK, BLOCK)
    wq = i * BLOCK + lax.broadcasted_iota(jnp.int32, (BLOCK, wlen), 0)
    wk = start + lax.broadcasted_iota(jnp.int32, (BLOCK, wlen), 1)
    wdist = wq - wk
    wmask = jnp.where((wdist >= 0) & (wdist <= window), 0.0, NEG)
    wdistf = wdist.astype(F32)
    rows = lambda r: slice(r * BLOCK, (r + 1) * BLOCK)
    q_stack = [jnp.concatenate([q_refs[g][0, :, hd(r)].astype(BF16) for r in range(R)], axis=0) for g in range(G)]
    for g in range(G):
        kwin = kw_refs[g][0, pl.ds(start, wlen), :].astype(BF16)
        vwin = vw_refs[g][0, pl.ds(start, wlen), :].astype(BF16)
        outs = _stacked_softmax2_pv(
            q_stack[g], kwin, vwin,
            lambda r, qk, g=g: qk * (ATTN_SCALE * LOG2E) - (slope_ref[g * R + r] * LOG2E) * wdistf + wmask, R)
        for r in range(R):
            h = g * R + r
            win_ref[:, hd(h)] = gate[:, 2 * B_HEADS + h:2 * B_HEADS + h + 1] * outs[r]

    nc = ckv_ref.shape[2]
    t = i * BLOCK + lax.broadcasted_iota(jnp.int32, (BLOCK, nc), 0)
    n = lax.broadcasted_iota(jnp.int32, (BLOCK, nc), 1)
    dist_c = t - (n * B_CMP_STRIDE + B_CMP_LEN - 1)
    mask = (dist_c >= 0) & (n < n_cmp)
    distf = dist_c.astype(F32)

    nn = lax.broadcasted_iota(jnp.int32, (nc, LANES), 0)
    jj = lax.broadcasted_iota(jnp.int32, (nc, LANES), 1)
    off = nn - B_SEL_RATIO * jj + 1
    w = jnp.where((off == 0) | (off == B_SEL_RATIO), 1.0, jnp.where((off > 0) & (off < B_SEL_RATIO), 2.0, 0.0))
    w = jnp.where((nn < n_cmp) & (jj < n_slc), w, 0.0).astype(F32)
    tq = i * BLOCK + lax.broadcasted_iota(jnp.int32, (BLOCK, LANES), 0)
    j = lax.broadcasted_iota(jnp.int32, (BLOCK, LANES), 1)
    cur = tq >> int(np.log2(B_SEL_LEN))
    forced = (j == 0) | (j == cur) | (j == cur - 1)
    scores = []
    for g in range(G):
        kc = ckv_ref[0, g]
        vc = ckv_ref[0, G + g]
        imp = jnp.zeros((BLOCK, nc), F32)
        qk = _dot_nt(q_stack[g], kc)
        ps = []
        for r in range(R):
            s = qk[rows(r)] * ATTN_SCALE
            s = jnp.where(mask, s - slope_ref[g * R + r] * distf, NEG)
            m = jnp.max(s, axis=-1, keepdims=True)
            e = jnp.where(mask, jnp.exp(s - m), 0.0)
            p = e / jnp.maximum(jnp.sum(e, axis=-1, keepdims=True), 1e-30)
            ps.append(p.astype(BF16))
            imp = imp + p
        pv = jnp.dot(jnp.concatenate(ps, axis=0), vc, preferred_element_type=F32)
        for r in range(R):
            h = g * R + r
            cmp_ref[:, hd(h)] = gate[:, h:h + 1] * pv[rows(r)]
        p_slc = jnp.dot(imp, w, preferred_element_type=F32, precision=lax.Precision.HIGHEST)
        score = jnp.where(forced, 1e9, jnp.where(j <= cur, p_slc, -1e9))
        scores.append(jnp.where(j < n_slc, score, -3e38))

    score = jnp.concatenate(scores, axis=0)
    jf = lax.broadcasted_iota(jnp.int32, score.shape, 1).astype(F32)
    sel = jnp.zeros(score.shape, F32)
    for _ in range(min(B_N_SEL, n_slc)):
        m = jnp.max(score, axis=-1, keepdims=True)
        first = jnp.min(jnp.where(score == m, jf, float(LANES)), axis=-1, keepdims=True)
        hit = jf == first
        sel = jnp.where(hit, 1.0, sel)
        score = jnp.where(hit, -3e38, score)
    for g in range(G):
        sel_ref[g] = sel[g * BLOCK:(g + 1) * BLOCK].astype(sel_ref.dtype)

    def body(kw):
        jj = lax.broadcasted_iota(jnp.int32, (LANES, kw), 0)
        ss = lax.broadcasted_iota(jnp.int32, (LANES, kw), 1)
        expand = jnp.where((ss >> int(np.log2(B_SEL_LEN))) == jj, 1.0, 0.0).astype(BF16)
        qpos = i * BLOCK + lax.broadcasted_iota(jnp.int32, (BLOCK, kw), 0)
        kpos = lax.broadcasted_iota(jnp.int32, (BLOCK, kw), 1)
        dist = qpos - kpos
        distf = dist.astype(F32)
        for g in range(G):
            picked = jnp.dot(sel_ref[g], expand, preferred_element_type=F32) > 0.5
            maskadd = jnp.where(picked & (dist >= 0), 0.0, NEG)
            k = ks_refs[g][0, :kw, :].astype(BF16)
            v = vs_refs[g][0, :kw, :].astype(BF16)
            q_rows = jnp.concatenate([q_refs[g][0, :, hd(r)].astype(BF16) for r in range(R)], axis=0)
            outs = _stacked_softmax2_pv(
                q_rows, k, v,
                lambda r, qk, g=g, maskadd=maskadd: (qk * (ATTN_SCALE * LOG2E)
                                                     - (slope_ref[g * R + r] * LOG2E) * distf + maskadd), R)
            for r in range(R):
                h = g * R + r
                o_slc = gate[:, B_HEADS + h:B_HEADS + h + 1] * outs[r]
                o_ref[0, :, hd(h)] = (cmp_ref[:, hd(h)] + o_slc + win_ref[:, hd(h)]).astype(o_ref.dtype)

    _for_causal_class(i, nb, body)


def _nsa_attention(y3, cmp_kv, gate_logits, slopes, n_cmp, n_slc):
    B, L, _ = y3.shape
    G, R = B_KV_GROUPS, B_GROUP_SIZE
    nc = cmp_kv.shape[2]
    w_steps = B_WINDOW - 1
    q_specs = [pl.BlockSpec((1, BLOCK, R * HEAD_DIM), lambda b, i, g=g: (b, i, CB_BQ // R + g)) for g in range(G)]
    kv_specs = [pl.BlockSpec((1, L, HEAD_DIM), lambda b, i, c=CB_BKV + (br * 2 + kv) * G + g: (b, 0, c))
                for br in (1, 2) for kv in range(2) for g in range(G)]
    hw = B_HEADS * HEAD_DIM
    return pl.pallas_call(
        functools.partial(_nsa_kernel, n_cmp=n_cmp, n_slc=n_slc, window=w_steps, n_prev=-(-w_steps // BLOCK)),
        grid=(B, L // BLOCK),
        in_specs=[pl.BlockSpec(memory_space=pltpu.SMEM)] + q_specs + kv_specs +
                 [pl.BlockSpec((1, 2 * G, nc, HEAD_DIM), lambda b, i: (b, 0, 0, 0)),
                  pl.BlockSpec((1, BLOCK, gate_logits.shape[2]), lambda b, i: (b, i, 0))],
        out_specs=pl.BlockSpec((1, BLOCK, hw), lambda b, i: (b, i, 0)),
        out_shape=jax.ShapeDtypeStruct((B, L, hw), BF16),
        scratch_shapes=[pltpu.VMEM((G, BLOCK, LANES), BF16), pltpu.VMEM((BLOCK, hw), F32),
                        pltpu.VMEM((BLOCK, hw), F32)],
        compiler_params=_cparams(("parallel", "arbitrary")),
        name="nsa_attention",
    )(slopes, *([y3] * (5 * G)), cmp_kv, gate_logits)


def _native_sparse_attention(y3, gate_logits, cmp_w1, cmp_w2, cmp_pos, slopes_all):
    B, L, _ = y3.shape
    G, R = B_KV_GROUPS, B_GROUP_SIZE
    slopes = slopes_all[np.array(B_SLOPE_IDX)]
    n_chunk = L // B_CMP_STRIDE
    n_cmp = n_chunk - B_CMP_LEN // B_CMP_STRIDE + 1
    n_slc = L // B_SEL_LEN
    assert B_CMP_LEN == 2 * B_CMP_STRIDE and n_chunk % 8 == 0 and n_slc <= LANES

    xc = y3[:, :, CB_BKV * LANES:(CB_BKV + 2 * G) * LANES].reshape(B, L, 2 * G, HEAD_DIM)
    xc = xc.transpose(0, 2, 1, 3).reshape(B, 2 * G, n_chunk, B_CMP_STRIDE * HEAD_DIM)
    cmp_kv = _compress(xc, cmp_pos.reshape(2, 1, B_CMP_LEN * HEAD_DIM), cmp_w1.astype(BF16), cmp_w2.astype(BF16))

    o = _nsa_attention(y3, cmp_kv, gate_logits.reshape(B, L, -1), slopes, n_cmp, n_slc)
    return o.reshape(B * L, B_HEADS * HEAD_DIM)


def _gated_merge_kernel(x_ref, *refs, nk):
    nbr = N_BRANCH
    wg, o, w, bias = refs[:nbr], refs[nbr:2 * nbr], refs[2 * nbr:3 * nbr], refs[3 * nbr:4 * nbr]
    out_ref, acc = refs[4 * nbr], refs[4 * nbr + 1:]

    def logits(c):
        return jnp.dot(x_ref[...], wg[c][...], preferred_element_type=F32)

    def first():
        for c in range(nbr):
            acc[0][c] = logits(c)

    def middle():
        for c in range(nbr):
            acc[0][c] += logits(c)

    def last(has_acc):
        merged = None
        for c in range(nbr):
            z = (acc[0][c] + logits(c)) if has_acc else logits(c)
            term = _sigmoid(z + bias[c][...]) * jnp.dot(o[c][...], w[c][...], preferred_element_type=F32)
            merged = term if merged is None else merged + term
        out_ref[...] = merged.astype(out_ref.dtype)

    _k_steps(nk, first, middle, last)


def _gated_merge(xb, w_gate, b_gate, branch_outs, branch_ws):
    M, K = xb.shape
    D = branch_ws[0].shape[1]
    tm, tn, tk = _tile(M, 1024), _tile(D, 512), _tile(K, 1024)
    nj, nk = D // tn, K // tk
    wg_specs = [pl.BlockSpec((tk, tn), lambda i, j, k, c=c: (k, c * nj + j)) for c in range(N_BRANCH)]
    o_specs = [pl.BlockSpec((tm, o.shape[1]), lambda i, j, k: (i, 0)) for o in branch_outs]
    w_specs = [pl.BlockSpec((w.shape[0], tn), lambda i, j, k: (0, j)) for w in branch_ws]
    b_specs = [pl.BlockSpec((1, tn), lambda i, j, k, c=c: (0, c * nj + j)) for c in range(N_BRANCH)]
    return pl.pallas_call(
        functools.partial(_gated_merge_kernel, nk=nk),
        grid=(M // tm, nj, nk),
        in_specs=[pl.BlockSpec((tm, tk), lambda i, j, k: (i, k))] + wg_specs + o_specs + w_specs + b_specs,
        out_specs=pl.BlockSpec((tm, tn), lambda i, j, k: (i, j)),
        out_shape=jax.ShapeDtypeStruct((M, D), BF16),
        scratch_shapes=[pltpu.VMEM((N_BRANCH, tm, tn), F32)] if nk > 1 else [],
        compiler_params=_cparams(("parallel", "parallel", "arbitrary")),
        name="gated_merge",
    )(xb, *([w_gate] * N_BRANCH), *branch_outs, *branch_ws, *([b_gate.reshape(1, -1)] * N_BRANCH))


def _attn_weight(w_in):
    D = w_in.shape[0]
    sizes = (A_HEADS * HEAD_DIM,) * 3 + (B_HEADS * HEAD_DIM, 3 * 2 * B_KV_GROUPS * HEAD_DIM, 3 * B_HEADS) + \
            (C_HEADS * HEAD_DIM,) * 3 + (C_HEADS,) + (D_HEADS * HEAD_DIM, HEAD_DIM, HEAD_DIM) + \
            (D_IDX_HEADS * D_IDX_DIM, D_IDX_DIM, D_IDX_HEADS)
    offs = np.concatenate([[0], np.cumsum(sizes)])
    (a_q, a_k, a_v, b_q, b_kv, b_g, c_q, c_k, c_v, c_f, d_q, d_k, d_v, d_iq, d_ik, d_iw) = [
        w_in[:, offs[n]:offs[n + 1]] for n in range(len(sizes))]
    misc = jnp.zeros((D, 2 * LANES), w_in.dtype)
    misc = misc.at[:, MISC_BG:MISC_BG + 3 * B_HEADS].set(b_g)
    misc = misc.at[:, MISC_CF:MISC_CF + C_HEADS].set(c_f)
    misc = misc.at[:, MISC_IK:MISC_IK + D_IDX_DIM].set(d_ik)
    misc = misc.at[:, MISC_IW:MISC_IW + D_IDX_HEADS].set(d_iw)
    parts = [a_q, a_k, a_v, b_q, b_kv, c_q, c_k, c_v, d_q, d_iq, d_k, d_v, misc]
    w = jnp.concatenate([p.astype(BF16) for p in parts], axis=1)
    assert w.shape[1] == N_CB * LANES
    return w, int(offs[-1])


def _hybrid_mixer(xb, B, L, w_in, b_forget, b_gate, cmp_w1, cmp_w2, cmp_pos, w_branch, w_out):
    M, D = xb.shape
    slopes_all = jnp.exp2(-8.0 * jnp.arange(1, N_ALIBI + 1, dtype=F32) / N_ALIBI)
    w_attn, gate_off = _attn_weight(w_in)
    y3 = _matmul(xb, w_attn, BF16).reshape(B, L, N_CB * LANES)
    misc = _matmul(xb, w_attn[:, CB_MISC * LANES:(CB_MISC + 1) * LANES], F32)

    bias = jnp.zeros((1, LANES), F32).at[0, MISC_CF:MISC_CF + C_HEADS].set(b_forget)
    c = _logsig_cumsum(misc.reshape(B, L, LANES), bias)[:, :, MISC_CF:MISC_CF + C_HEADS].transpose(0, 2, 1)

    o_a = _dilated_attention(y3, slopes_all)
    o_b = _native_sparse_attention(y3, misc[:, MISC_BG:MISC_BG + 3 * B_HEADS], cmp_w1, cmp_w2, cmp_pos, slopes_all)
    o_c = _forgetting_attention(y3, c[..., None], c[:, :, None, :])
    o_d = _indexed_sparse_attention(y3, misc[:, MISC_IK:MISC_IK + D_IDX_DIM].reshape(B, L, D_IDX_DIM),
                                    misc[:, MISC_IW:MISC_IW + D_IDX_HEADS].reshape(B, L, D_IDX_HEADS), slopes_all)

    sizes = (A_HEADS_PER_PAIR * HEAD_DIM, B_HEADS * HEAD_DIM, C_HEADS * HEAD_DIM, D_HEADS * HEAD_DIM)
    offs = np.concatenate([[0], np.cumsum(sizes)])
    ws = [w_branch[offs[n]:offs[n + 1]] for n in range(N_BRANCH)]
    outs = [o_a, o_b, o_c.reshape(M, -1), o_d.reshape(M, -1)]
    merged = _gated_merge(xb, w_in[:, gate_off:].astype(BF16), b_gate, outs, ws)
    return _matmul(merged, w_out, F32)


def kernel(x, ln_g, ln_b, ffn1_w_gate, ffn1_w_up, ffn1_w_down, w_in, b_forget, b_gate, cmp_w1, cmp_w2, cmp_pos,
           w_branch, w_out, ffn2_w_gate, ffn2_w_up, ffn2_w_down):
    B, L, D = x.shape
    assert L % BLOCK == 0 and D % LANES == 0
    xf = x.reshape(B * L, D)
    xb = xf.astype(BF16)

    def ffn(xf, xb, wg, wu, wd, l, g, b):
        h = _ffn_up(xb, _cast_bf16(wg, l), _cast_bf16(wu, l))
        y = _matmul(h, _cast_bf16(wd, l), F32)
        return _add_ln(xf, y, g, b, 0.5)

    for l in range(ln_g.shape[0]):
        xf, xb = ffn(xf, xb, ffn1_w_gate, ffn1_w_up, ffn1_w_down, l, ln_g[l, 0], ln_b[l, 0])
        y = _hybrid_mixer(xb, B, L, w_in[l], b_forget[l], b_gate[l], cmp_w1[l], cmp_w2[l], cmp_pos[l],
                          _cast_bf16(w_branch, l), _cast_bf16(w_out, l))
        xf, xb = _add_ln(xf, y, ln_g[l, 1], ln_b[l, 1], 1.0)
        xf, xb = ffn(xf, xb, ffn2_w_gate, ffn2_w_up, ffn2_w_down, l, ln_g[l, 2], ln_b[l, 2])
    return xf.reshape(B, L, D)
```

```python
import functools

import numpy as np
import jax
import jax.numpy as jnp
from jax import lax
from jax.experimental import pallas as pl
from jax.experimental.pallas import tpu as pltpu

F32 = jnp.float32
BF16 = jnp.bfloat16

HEAD_DIM = 128
BLOCK = 128
LANES = 128
NEG = -1e30
ATTN_SCALE = HEAD_DIM ** -0.5
LOG2E = 1.4426950408889634
LN2 = 0.6931471805599453
DEPTH = 2
ALPHA = (2 * DEPTH) ** 0.25
LN_EPS = 1e-5

A_PAIRS = ((128, 1), (512, 4), (2048, 16))
A_HEADS_PER_PAIR = 4
A_HEADS = A_HEADS_PER_PAIR * len(A_PAIRS)
B_HEADS = 8
B_KV_GROUPS = 2
B_GROUP_SIZE = B_HEADS // B_KV_GROUPS
B_CMP_LEN = 32
B_CMP_STRIDE = 16
B_SEL_LEN = 64
B_SEL_RATIO = B_SEL_LEN // B_CMP_STRIDE
B_N_SEL = 8
B_WINDOW = 512
B_CMP_HIDDEN = 512
C_HEADS = 8
D_HEADS = 8
D_IDX_HEADS = 8
D_IDX_DIM = 64
D_TOPK = 256
D_HEAD_STACK = 4
D_STACK_MAX_KEYS = 768
N_BRANCH = 4
N_ALIBI = A_HEADS + B_HEADS + D_HEADS
A_SLOPE_IDX = (0, 1, 2, 3, 12, 13, 14, 15, 24, 25, 26, 27)
B_SLOPE_IDX = (4, 5, 6, 7, 8, 9, 10, 11)
D_SLOPE_IDX = (16, 17, 18, 19, 20, 21, 22, 23)

CB_A = 0
CB_BQ = 36
CB_BKV = 44
CB_CQ, CB_CK, CB_CV = 56, 64, 72
CB_DQ = 80
CB_DIQ = 88
CB_DK, CB_DV = 92, 93
CB_MISC = 94
N_CB = 96
MISC_BG, MISC_CF, MISC_IK, MISC_IW = 0, 24, 32, 96

CAUSAL_CLASSES = 8
VMEM_LIMIT = 56 * 1024 * 1024


def _cparams(sem):
    return pltpu.CompilerParams(dimension_semantics=sem, vmem_limit_bytes=VMEM_LIMIT)


def _tile(dim, pref):
    return pref if dim % pref == 0 else dim


def _dot_nt(a, b):
    return lax.dot_general(a, b, (((1,), (1,)), ((), ())), preferred_element_type=F32)


def _softmax2_pv(s2, v):
    m = jnp.max(s2, axis=-1, keepdims=True)
    e = jnp.exp2(s2 - m)
    den = jnp.maximum(jnp.sum(e, axis=-1, keepdims=True), 1e-30)
    return jnp.dot(e.astype(BF16), v, preferred_element_type=F32) / den


def _stacked_softmax2_pv(q_stack, k, v, logits2, n):
    qk = _dot_nt(q_stack, k)
    es, dens = [], []
    for r in range(n):
        s2 = logits2(r, qk[r * BLOCK:(r + 1) * BLOCK])
        e = jnp.exp2(s2 - jnp.max(s2, axis=-1, keepdims=True))
        dens.append(jnp.maximum(jnp.sum(e, axis=-1, keepdims=True), 1e-30))
        es.append(e.astype(BF16))
    pv = jnp.dot(jnp.concatenate(es, axis=0), v, preferred_element_type=F32)
    return [pv[r * BLOCK:(r + 1) * BLOCK] / dens[r] for r in range(n)]


def _for_causal_class(i, nb, body):
    n_cls = CAUSAL_CLASSES if nb % CAUSAL_CLASSES == 0 else 1
    per = nb // n_cls
    for c in range(n_cls):
        pl.when((i >= c * per) & (i < (c + 1) * per))(functools.partial(body, (c + 1) * per * BLOCK))


def _cast_kernel(x_ref, o_ref):
    o_ref[...] = x_ref[0].astype(o_ref.dtype)


def _cast_bf16(w_stack, l):
    _, R, C = w_stack.shape
    tr = R
    for cand in (1024, 512, 256, 128, 64, 32, 16):
        if R % cand == 0:
            tr = cand
            if cand * C * 4 <= 8 * 1024 * 1024:
                break
    return pl.pallas_call(
        _cast_kernel,
        grid=(R // tr,),
        in_specs=[pl.BlockSpec((1, tr, C), lambda i: (l, i, 0))],
        out_specs=pl.BlockSpec((tr, C), lambda i: (i, 0)),
        out_shape=jax.ShapeDtypeStruct((R, C), BF16),
        compiler_params=_cparams(("parallel",)),
        name="cast_bf16",
    )(w_stack)


def _sigmoid(z):
    return 0.5 * jnp.tanh(0.5 * z) + 0.5


def _k_steps(nk, first, middle, last):
    k = pl.program_id(2)
    if nk == 1:
        last(False)
        return
    pl.when(k == 0)(first)
    if nk > 2:
        pl.when((k > 0) & (k < nk - 1))(middle)
    pl.when(k == nk - 1)(functools.partial(last, True))


def _mm_kernel(x_ref, w_ref, o_ref, *acc, nk):
    def prod():
        return jnp.dot(x_ref[...], w_ref[...], preferred_element_type=F32)

    def first():
        acc[0][...] = prod()

    def middle():
        acc[0][...] += prod()

    def last(has_acc):
        o_ref[...] = ((acc[0][...] + prod()) if has_acc else prod()).astype(o_ref.dtype)

    _k_steps(nk, first, middle, last)


def _matmul(x, w, out_dtype, tiles=(1024, 1024, 4096)):
    M, K = x.shape
    N = w.shape[1]
    tm, tn, tk = _tile(M, tiles[0]), _tile(N, tiles[1]), _tile(K, tiles[2])
    nk = K // tk
    return pl.pallas_call(
        functools.partial(_mm_kernel, nk=nk),
        grid=(M // tm, N // tn, nk),
        in_specs=[pl.BlockSpec((tm, tk), lambda i, j, k: (i, k)),
                  pl.BlockSpec((tk, tn), lambda i, j, k: (k, j))],
        out_specs=pl.BlockSpec((tm, tn), lambda i, j, k: (i, j)),
        out_shape=jax.ShapeDtypeStruct((M, N), out_dtype),
        scratch_shapes=[pltpu.VMEM((tm, tn), F32)] if nk > 1 else [],
        compiler_params=_cparams(("parallel", "parallel", "arbitrary")),
        name="matmul",
    )(x, w)


def _ffn_up_kernel(x_ref, wg_ref, wu_ref, o_ref, *acc, nk):
    def prods():
        x = x_ref[...]
        return (jnp.dot(x, wg_ref[...], preferred_element_type=F32),
                jnp.dot(x, wu_ref[...], preferred_element_type=F32))

    def first():
        acc[0][...], acc[1][...] = prods()

    def middle():
        g, u = prods()
        acc[0][...] += g
        acc[1][...] += u

    def last(has_acc):
        g, u = prods()
        if has_acc:
            g, u = acc[0][...] + g, acc[1][...] + u
        o_ref[...] = (g * _sigmoid(g) * u).astype(o_ref.dtype)

    _k_steps(nk, first, middle, last)


def _ffn_up(x, wg, wu, tiles=(1024, 1024, 2048)):
    M, K = x.shape
    N = wg.shape[1]
    tm, tn, tk = _tile(M, tiles[0]), _tile(N, tiles[1]), _tile(K, tiles[2])
    nk = K // tk
    return pl.pallas_call(
        functools.partial(_ffn_up_kernel, nk=nk),
        grid=(M // tm, N // tn, nk),
        in_specs=[pl.BlockSpec((tm, tk), lambda i, j, k: (i, k)),
                  pl.BlockSpec((tk, tn), lambda i, j, k: (k, j)),
                  pl.BlockSpec((tk, tn), lambda i, j, k: (k, j))],
        out_specs=pl.BlockSpec((tm, tn), lambda i, j, k: (i, j)),
        out_shape=jax.ShapeDtypeStruct((M, N), BF16),
        scratch_shapes=[pltpu.VMEM((tm, tn), F32), pltpu.VMEM((tm, tn), F32)] if nk > 1 else [],
        compiler_params=_cparams(("parallel", "parallel", "arbitrary")),
        name="ffn_up",
    )(x, wg, wu)


def _add_ln_kernel(x_ref, y_ref, g_ref, b_ref, o_ref, ob_ref, *, cy):
    z = ALPHA * x_ref[...] + cy * y_ref[...]
    mu = jnp.mean(z, axis=-1, keepdims=True)
    zc = z - mu
    var = jnp.mean(zc * zc, axis=-1, keepdims=True)
    out = zc * lax.rsqrt(var + LN_EPS) * g_ref[...] + b_ref[...]
    o_ref[...] = out
    ob_ref[...] = out.astype(BF16)


def _add_ln(x, y, g, b, cy):
    M, D = x.shape
    tm = _tile(M, 256)
    row = pl.BlockSpec((tm, D), lambda i: (i, 0))
    vec = pl.BlockSpec((1, D), lambda i: (0, 0))
    return pl.pallas_call(
        functools.partial(_add_ln_kernel, cy=cy),
        grid=(M // tm,),
        in_specs=[row, row, vec, vec],
        out_specs=[row, row],
        out_shape=[jax.ShapeDtypeStruct((M, D), F32), jax.ShapeDtypeStruct((M, D), BF16)],
        compiler_params=_cparams(("parallel",)),
        name="add_ln",
    )(x, y, g.reshape(1, D), b.reshape(1, D))


def _banded_kernel(slope_ref, q_ref, k_ref, v_ref, o_ref, lse_ref, *, window, n_prev, step, nb, hp):
    g = pl.program_id(1)
    i = pl.program_id(2)
    kwb = min(n_prev + 1, nb)
    kw = kwb * BLOCK
    start = pl.multiple_of(jnp.maximum(i - n_prev, 0) * BLOCK, BLOCK)
    qpos = i * BLOCK + lax.broadcasted_iota(jnp.int32, (BLOCK, kw), 0)
    kpos = start + lax.broadcasted_iota(jnp.int32, (BLOCK, kw), 1)
    dist = qpos - kpos
    maskadd = jnp.where((dist >= 0) & (dist <= window), 0.0, NEG)
    krel = (step * lax.broadcasted_iota(jnp.int32, (1, kw), 1)).astype(F32)
    qrel = (step * (i * BLOCK - start + lax.broadcasted_iota(jnp.int32, (BLOCK, 1), 0))).astype(F32)
    for h in range(hp):
        cs = slice(h * HEAD_DIM, (h + 1) * HEAD_DIM)
        slope = slope_ref[g * hp + h]
        q = q_ref[0, :, cs].astype(BF16)
        kwin = k_ref[0, pl.ds(start, kw), cs].astype(BF16)
        vwin = v_ref[0, pl.ds(start, kw), cs].astype(BF16)
        s = _dot_nt(q, kwin) * ATTN_SCALE + (slope * krel + maskadd)
        m = jnp.max(s, axis=-1, keepdims=True)
        e = jnp.exp(s - m)
        den = jnp.maximum(jnp.sum(e, axis=-1, keepdims=True), 1e-30)
        o_ref[0, :, cs] = jnp.dot(e.astype(BF16), vwin, preferred_element_type=F32) / den
        lse_ref[0, 0, :, h:h + 1] = m + jnp.log(den) - slope * qrel


def _banded_attention(qa, ka, va, slopes, *, n_groups, hp, q_blk, k_blk, v_blk, window, n_prev, step):
    B, N, _ = qa.shape
    nb = N // BLOCK
    qspec = pl.BlockSpec((1, BLOCK, hp * HEAD_DIM), lambda b, g, i: (b, i, q_blk(g)))
    kspec = pl.BlockSpec((1, N, hp * HEAD_DIM), lambda b, g, i: (b, 0, k_blk(g)))
    vspec = pl.BlockSpec((1, N, hp * HEAD_DIM), lambda b, g, i: (b, 0, v_blk(g)))
    ospec = pl.BlockSpec((1, BLOCK, hp * HEAD_DIM), lambda b, g, i: (b, i, g))
    oshape = jax.ShapeDtypeStruct((B, N, n_groups * hp * HEAD_DIM), F32)
    lspec = pl.BlockSpec((1, 1, BLOCK, hp), lambda b, g, i: (b, g, i, 0))
    lshape = jax.ShapeDtypeStruct((B, n_groups, N, hp), F32)
    return pl.pallas_call(
        functools.partial(_banded_kernel, window=window, n_prev=n_prev, step=step, nb=nb, hp=hp),
        grid=(B, n_groups, nb),
        in_specs=[pl.BlockSpec(memory_space=pltpu.SMEM), qspec, kspec, vspec],
        out_specs=[ospec, lspec],
        out_shape=[oshape, lshape],
        compiler_params=_cparams(("parallel", "parallel", "arbitrary")),
        name="banded_attention",
    )(slopes, qa, ka, va)


def _a_combine_kernel(o0, o1, o2, l0, l1, l2, out_ref):
    a, b, c = l0[...], l1[...], l2[...]
    m = jnp.maximum(jnp.maximum(a, b), c)
    ea, eb, ec = jnp.exp(a - m), jnp.exp(b - m), jnp.exp(c - m)
    tot = ea + eb + ec
    wa, wb, wc = ea / tot, eb / tot, ec / tot
    for h in range(a.shape[1]):
        cs = slice(h * HEAD_DIM, (h + 1) * HEAD_DIM)
        out = wa[:, h:h + 1] * o0[:, cs] + wb[:, h:h + 1] * o1[:, cs] + wc[:, h:h + 1] * o2[:, cs]
        out_ref[:, cs] = out.astype(out_ref.dtype)


def _a_combine(outs, lses):
    M, C = outs[0].shape
    tm = _tile(M, 512)
    spec = pl.BlockSpec((tm, C), lambda i: (i, 0))
    lspec = pl.BlockSpec((tm, lses[0].shape[1]), lambda i: (i, 0))
    return pl.pallas_call(
        _a_combine_kernel,
        grid=(M // tm,),
        in_specs=[spec] * 3 + [lspec] * 3,
        out_specs=spec,
        out_shape=jax.ShapeDtypeStruct((M, C), BF16),
        compiler_params=_cparams(("parallel",)),
        name="a_combine",
    )(*outs, *lses)


def _dilated_attention(y3, slopes_all):
    B, L, _ = y3.shape
    slopes = slopes_all[np.array(A_SLOPE_IDX)]
    hpp = A_HEADS_PER_PAIR
    gw = 3 * hpp * HEAD_DIM
    outs, lses = [], []
    for g, (window, dil) in enumerate(A_PAIRS):
        n = L // dil
        assert n % BLOCK == 0
        steps = window // dil
        if dil == 1:
            ya, base = y3, CB_A // hpp + g
            blk = lambda part: (lambda r: base + part * (A_HEADS // hpp))
        else:
            cols = [y3[:, :, (CB_A + p * A_HEADS + g * hpp) * LANES:(CB_A + p * A_HEADS + (g + 1) * hpp) * LANES]
                    for p in range(3)]
            ya = jnp.concatenate(cols, axis=-1).reshape(B, n, dil * gw)
            blk = lambda part: (lambda r: r * 3 + part)
        o, lse = _banded_attention(ya, ya, ya, jnp.tile(slopes[g * hpp:(g + 1) * hpp], dil),
                                   n_groups=dil, hp=hpp, q_blk=blk(0), k_blk=blk(1), v_blk=blk(2),
                                   window=steps, n_prev=-(-steps // BLOCK), step=dil)
        outs.append(o.reshape(B * L, hpp * HEAD_DIM))
        lses.append(lse.transpose(0, 2, 1, 3).reshape(B * L, hpp))
    return _a_combine(outs, lses)


def _logsig_cumsum_kernel(z_ref, bias_ref, o_ref):
    L = z_ref.shape[1]
    row = lax.broadcasted_iota(jnp.int32, (BLOCK, BLOCK), 0)
    col = lax.broadcasted_iota(jnp.int32, (BLOCK, BLOCK), 1)
    tri = jnp.where(row >= col, 1.0, 0.0).astype(F32)
    carry = jnp.zeros((1, LANES), F32)
    for j in range(L // BLOCK):
        z = z_ref[0, j * BLOCK:(j + 1) * BLOCK, :] + bias_ref[...]
        ls = jnp.minimum(z, 0.0) - jnp.log(1.0 + jnp.exp(-jnp.abs(z)))
        c = jnp.dot(tri, ls, preferred_element_type=F32, precision=lax.Precision.HIGHEST) + carry
        o_ref[0, j * BLOCK:(j + 1) * BLOCK, :] = c
        carry = c[BLOCK - 1:BLOCK, :]


def _logsig_cumsum(z, bias):
    B, L, _ = z.shape
    spec = pl.BlockSpec((1, L, LANES), lambda b: (b, 0, 0))
    return pl.pallas_call(
        _logsig_cumsum_kernel,
        grid=(B,),
        in_specs=[spec, pl.BlockSpec((1, LANES), lambda b: (0, 0))],
        out_specs=spec,
        out_shape=jax.ShapeDtypeStruct((B, L, LANES), F32),
        compiler_params=_cparams(("parallel",)),
        name="logsig_cumsum",
    )(z, bias)


def _fox_kernel(q_ref, k_ref, v_ref, cq_ref, ck_ref, o_ref):
    i = pl.program_id(1)
    nb = k_ref.shape[1] // BLOCK

    def body(kw):
        qpos = i * BLOCK + lax.broadcasted_iota(jnp.int32, (BLOCK, kw), 0)
        kpos = lax.broadcasted_iota(jnp.int32, (BLOCK, kw), 1)
        maskadd = jnp.where(qpos >= kpos, 0.0, NEG)
        for h in range(C_HEADS):
            cs = slice(h * HEAD_DIM, (h + 1) * HEAD_DIM)
            q = q_ref[0, :, cs].astype(BF16)
            k = k_ref[0, :kw, cs].astype(BF16)
            v = v_ref[0, :kw, cs].astype(BF16)
            s2 = _dot_nt(q, k) * (ATTN_SCALE * LOG2E) + cq_ref[0, h] * LOG2E - ck_ref[0, h, :, :kw] * LOG2E
            o_ref[0, :, cs] = _softmax2_pv(s2 + maskadd, v).astype(o_ref.dtype)

    _for_causal_class(i, nb, body)


def _forgetting_attention(y3, c_col, c_row):
    B, L, _ = y3.shape
    nb = L // BLOCK
    cw = C_HEADS * HEAD_DIM
    return pl.pallas_call(
        _fox_kernel,
        grid=(B, nb),
        in_specs=[pl.BlockSpec((1, BLOCK, cw), lambda b, i: (b, i, CB_CQ // C_HEADS)),
                  pl.BlockSpec((1, L, cw), lambda b, i: (b, 0, CB_CK // C_HEADS)),
                  pl.BlockSpec((1, L, cw), lambda b, i: (b, 0, CB_CV // C_HEADS)),
                  pl.BlockSpec((1, C_HEADS, BLOCK, 1), lambda b, i: (b, 0, i, 0)),
                  pl.BlockSpec((1, C_HEADS, 1, L), lambda b, i: (b, 0, 0, 0))],
        out_specs=pl.BlockSpec((1, BLOCK, cw), lambda b, i: (b, i, 0)),
        out_shape=jax.ShapeDtypeStruct((B, L, cw), BF16),
        compiler_params=_cparams(("parallel", "arbitrary")),
        name="forgetting_attention",
    )(y3, y3, y3, c_col, c_row)


def _order_key(x):
    bits = lax.bitcast_convert_type(x, jnp.int32)
    return bits ^ ((bits >> 31) & jnp.int32(0x7FFFFFFF))


def _kth_largest_key(key, k):
    rows = key.shape[0]

    def count_ge(t):
        return jnp.sum(jnp.where(key >= t, 1.0, 0.0), axis=-1, keepdims=True)

    t0 = jnp.where(count_ge(jnp.zeros((rows, 1), jnp.int32)) >= k,
                   jnp.int32(0), jnp.int32(-2 ** 31)) + jnp.zeros((rows, 1), jnp.int32)

    def two_bits(it, t):
        lo = jnp.left_shift(jnp.int32(1), 29 - 2 * it)
        c1, c2 = t | lo, t | (lo + lo)
        c3 = c2 | lo
        n1, n2, n3 = count_ge(c1), count_ge(c2), count_ge(c3)
        return jnp.where(n3 >= k, c3, jnp.where(n2 >= k, c2, jnp.where(n1 >= k, c1, t)))

    t = lax.fori_loop(0, 15, two_bits, t0)
    last = t | jnp.int32(1)
    return jnp.where(count_ge(last) >= k, last, t)


def _dsa_kernel(slope_ref, iq_ref, ik_ref, iw_ref, q_ref, k_ref, v_ref, o_ref, *, n_top):
    i = pl.program_id(1)
    nb = k_ref.shape[1] // BLOCK
    iw = iw_ref[0]

    def body(kw):
        ik = ik_ref[0, :kw, :].astype(BF16)
        score = jnp.zeros((BLOCK, kw), F32)
        hs = D_HEAD_STACK
        for h0 in range(0, D_IDX_HEADS, hs):
            iq_rows = jnp.concatenate([iq_ref[0, :, (h0 + r) * D_IDX_DIM:(h0 + r + 1) * D_IDX_DIM].astype(BF16)
                                       for r in range(hs)], axis=0)
            z = _dot_nt(iq_rows, ik)
            for r in range(hs):
                rel = jnp.maximum(z[r * BLOCK:(r + 1) * BLOCK], 0.0)
                score = score + iw[:, h0 + r:h0 + r + 1] * rel
        qpos = i * BLOCK + lax.broadcasted_iota(jnp.int32, (BLOCK, kw), 0)
        kpos = lax.broadcasted_iota(jnp.int32, (BLOCK, kw), 1)
        dist = qpos - kpos
        causal = dist >= 0
        key = _order_key(jnp.where(causal, score, NEG))

        thr = _kth_largest_key(key, float(n_top))
        gt = key > thr
        eq = key == thr
        need = float(n_top) - jnp.sum(jnp.where(gt, 1.0, 0.0), axis=-1, keepdims=True)
        row = lax.broadcasted_iota(jnp.int32, (BLOCK, BLOCK), 0)
        col = lax.broadcasted_iota(jnp.int32, (BLOCK, BLOCK), 1)
        upper = jnp.where(row <= col, 1.0, 0.0).astype(BF16)
        eqf = jnp.where(eq, 1.0, 0.0).astype(BF16)
        carry = jnp.zeros((BLOCK, 1), F32)
        rank_tiles = []
        for j in range(kw // BLOCK):
            rank = carry + jnp.dot(eqf[:, j * BLOCK:(j + 1) * BLOCK], upper, preferred_element_type=F32)
            rank_tiles.append(rank)
            carry = rank[:, BLOCK - 1:BLOCK]
        rank = jnp.concatenate(rank_tiles, axis=1)
        mask = (gt | (eq & (rank <= need))) & causal
        maskadd = jnp.where(mask, 0.0, NEG)

        k = k_ref[0, :kw, :].astype(BF16)
        v = v_ref[0, :kw, :].astype(BF16)
        distf = dist.astype(F32)
        hs = D_HEAD_STACK if kw <= D_STACK_MAX_KEYS else D_HEAD_STACK // 2
        for h0 in range(0, D_HEADS, hs):
            q_rows = jnp.concatenate([q_ref[0, :, (h0 + r) * HEAD_DIM:(h0 + r + 1) * HEAD_DIM].astype(BF16)
                                      for r in range(hs)], axis=0)
            outs = _stacked_softmax2_pv(
                q_rows, k, v,
                lambda r, qk, h0=h0: (qk * (ATTN_SCALE * LOG2E) - (slope_ref[h0 + r] * LOG2E) * distf + maskadd), hs)
            for r in range(hs):
                o_ref[0, :, (h0 + r) * HEAD_DIM:(h0 + r + 1) * HEAD_DIM] = outs[r].astype(o_ref.dtype)

    _for_causal_class(i, nb, body)


def _indexed_sparse_attention(y3, ik, iw, slopes_all):
    B, L, _ = y3.shape
    nb = L // BLOCK
    n_top = min(D_TOPK, L // 4)
    slopes = slopes_all[np.array(D_SLOPE_IDX)]
    qw, iqw = D_HEADS * HEAD_DIM, D_IDX_HEADS * D_IDX_DIM
    return pl.pallas_call(
        functools.partial(_dsa_kernel, n_top=n_top),
        grid=(B, nb),
        in_specs=[pl.BlockSpec(memory_space=pltpu.SMEM),
                  pl.BlockSpec((1, BLOCK, iqw), lambda b, i: (b, i, CB_DIQ * LANES // iqw)),
                  pl.BlockSpec((1, L, D_IDX_DIM), lambda b, i: (b, 0, 0)),
                  pl.BlockSpec((1, BLOCK, D_IDX_HEADS), lambda b, i: (b, i, 0)),
                  pl.BlockSpec((1, BLOCK, qw), lambda b, i: (b, i, CB_DQ * LANES // qw)),
                  pl.BlockSpec((1, L, HEAD_DIM), lambda b, i: (b, 0, CB_DK)),
                  pl.BlockSpec((1, L, HEAD_DIM), lambda b, i: (b, 0, CB_DV))],
        out_specs=pl.BlockSpec((1, BLOCK, qw), lambda b, i: (b, i, 0)),
        out_shape=jax.ShapeDtypeStruct((B, L, qw), BF16),
        compiler_params=_cparams(("parallel", "arbitrary")),
        name="indexed_sparse_attention",
    )(slopes, y3, ik, iw, y3, y3, y3)


def _compress_kernel(x_ref, pos_ref, w1_ref, w2_ref, o_ref):
    x = x_ref[0, 0].astype(F32)
    nxt = pltpu.roll(x, x.shape[0] - 1, 0)
    blk = jnp.concatenate([x, nxt], axis=1) + pos_ref[0]
    h = jnp.dot(blk.astype(BF16), w1_ref[0], preferred_element_type=F32)
    h = jax.nn.gelu(h, approximate=True)
    o_ref[0, 0] = jnp.dot(h.astype(BF16), w2_ref[0], preferred_element_type=F32).astype(o_ref.dtype)


def _compress(xc, pos, w1, w2):
    B, _, n_chunk, cw = xc.shape
    G = B_KV_GROUPS
    return pl.pallas_call(
        _compress_kernel,
        grid=(2 * G, B),
        in_specs=[pl.BlockSpec((1, 1, n_chunk, cw), lambda a, b: (b, a, 0, 0)),
                  pl.BlockSpec((1, 1, 2 * cw), lambda a, b: (a // G, 0, 0)),
                  pl.BlockSpec((1, 2 * cw, B_CMP_HIDDEN), lambda a, b: (a // G, 0, 0)),
                  pl.BlockSpec((1, B_CMP_HIDDEN, HEAD_DIM), lambda a, b: (a // G, 0, 0))],
        out_specs=pl.BlockSpec((1, 1, n_chunk, HEAD_DIM), lambda a, b: (b, a, 0, 0)),
        out_shape=jax.ShapeDtypeStruct((B, 2 * G, n_chunk, HEAD_DIM), BF16),
        compiler_params=_cparams(("arbitrary", "arbitrary")),
        name="nsa_compress",
    )(xc, pos, w1, w2)


def _nsa_kernel(slope_ref, *refs, n_cmp, n_slc, window, n_prev):
    G, R = B_KV_GROUPS, B_GROUP_SIZE
    q_refs, ks_refs, vs_refs, kw_refs, vw_refs = (refs[n * G:(n + 1) * G] for n in range(5))
    ckv_ref, gl_ref, o_ref, sel_ref, cmp_ref, win_ref = refs[5 * G:]
    i = pl.program_id(1)
    nb = ks_refs[0].shape[1] // BLOCK
    gate = _sigmoid(gl_ref[0])
    hd = lambda h: slice(h * HEAD_DIM, (h + 1) * HEAD_DIM)

    kwb = min(n_prev + 1, nb)
    wlen = kwb * BLOCK
    start = pl.multiple_of(jnp.maximum(i - n_prev, 0) * BLOCK, BLOCK)
    wq = i * BLOCK + lax.broadcasted_iota(jnp.int32, (BLOCK, wlen), 0)
    wk = start + lax.broadcasted_iota(jnp.int32, (BLOCK, wlen), 1)
    wdist = wq - wk
    wmask = jnp.where((wdist >= 0) & (wdist <= window), 0.0, NEG)
    wdistf = wdist.astype(F32)
    rows = lambda r: slice(r * BLOCK, (r + 1) * BLOCK)
    q_stack = [jnp.concatenate([q_refs[g][0, :, hd(r)].astype(BF16) for r in range(R)], axis=0) for g in range(G)]
    for g in range(G):
        kwin = kw_refs[g][0, pl.ds(start, wlen), :].astype(BF16)
        vwin = vw_refs[g][0, pl.ds(start, wlen), :].astype(BF16)
        outs = _stacked_softmax2_pv(
            q_stack[g], kwin, vwin,
            lambda r, qk, g=g: qk * (ATTN_SCALE * LOG2E) - (slope_ref[g * R + r] * LOG2E) * wdistf + wmask, R)
        for r in range(R):
            h = g * R + r
            win_ref[:, hd(h)] = gate[:, 2 * B_HEADS + h:2 * B_HEADS + h + 1] * outs[r]

    nc = ckv_ref.shape[2]
    t = i * BLOCK + lax.broadcasted_iota(jnp.int32, (BLOCK, nc), 0)
    n = lax.broadcasted_iota(jnp.int32, (BLOCK, nc), 1)
    dist_c = t - (n * B_CMP_STRIDE + B_CMP_LEN - 1)
    mask = (dist_c >= 0) & (n < n_cmp)
    distf = dist_c.astype(F32)

    nn = lax.broadcasted_iota(jnp.int32, (nc, LANES), 0)
    jj = lax.broadcasted_iota(jnp.int32, (nc, LANES), 1)
    off = nn - B_SEL_RATIO * jj + 1
    w = jnp.where((off == 0) | (off == B_SEL_RATIO), 1.0, jnp.where((off > 0) & (off < B_SEL_RATIO), 2.0, 0.0))
    w = jnp.where((nn < n_cmp) & (jj < n_slc), w, 0.0).astype(F32)
    tq = i * BLOCK + lax.broadcasted_iota(jnp.int32, (BLOCK, LANES), 0)
    j = lax.broadcasted_iota(jnp.int32, (BLOCK, LANES), 1)
    cur = tq >> int(np.log2(B_SEL_LEN))
    forced = (j == 0) | (j == cur) | (j == cur - 1)
    scores = []
    for g in range(G):
        kc = ckv_ref[0, g]
        vc = ckv_ref[0, G + g]
        imp = jnp.zeros((BLOCK, nc), F32)
        qk = _dot_nt(q_stack[g], kc)
        ps = []
        for r in range(R):
            s = qk[rows(r)] * ATTN_SCALE
            s = jnp.where(mask, s - slope_ref[g * R + r] * distf, NEG)
            m = jnp.max(s, axis=-1, keepdims=True)
            e = jnp.where(mask, jnp.exp(s - m), 0.0)
            p = e / jnp.maximum(jnp.sum(e, axis=-1, keepdims=True), 1e-30)
            ps.append(p.astype(BF16))
            imp = imp + p
        pv = jnp.dot(jnp.concatenate(ps, axis=0), vc, preferred_element_type=F32)
        for r in range(R):
            h = g * R + r
            cmp_ref[:, hd(h)] = gate[:, h:h + 1] * pv[rows(r)]
        p_slc = jnp.dot(imp, w, preferred_element_type=F32, precision=lax.Precision.HIGHEST)
        score = jnp.where(forced, 1e9, jnp.where(j <= cur, p_slc, -1e9))
        scores.append(jnp.where(j < n_slc, score, -3e38))

    score = jnp.concatenate(scores, axis=0)
    jf = lax.broadcasted_iota(jnp.int32, score.shape, 1).astype(F32)
    sel = jnp.zeros(score.shape, F32)
    for _ in range(min(B_N_SEL, n_slc)):
        m = jnp.max(score, axis=-1, keepdims=True)
        first = jnp.min(jnp.where(score == m, jf, float(LANES)), axis=-1, keepdims=True)
        hit = jf == first
        sel = jnp.where(hit, 1.0, sel)
        score = jnp.where(hit, -3e38, score)
    for g in range(G):
        sel_ref[g] = sel[g * BLOCK:(g + 1) * BLOCK].astype(sel_ref.dtype)

    def body(kw):
        jj = lax.broadcasted_iota(jnp.int32, (LANES, kw), 0)
        ss = lax.broadcasted_iota(jnp.int32, (LANES, kw), 1)
        expand = jnp.where((ss >> int(np.log2(B_SEL_LEN))) == jj, 1.0, 0.0).astype(BF16)
        qpos = i * BLOCK + lax.broadcasted_iota(jnp.int32, (BLOCK, kw), 0)
        kpos = lax.broadcasted_iota(jnp.int32, (BLOCK, kw), 1)
        dist = qpos - kpos
        distf = dist.astype(F32)
        for g in range(G):
            picked = jnp.dot(sel_ref[g], expand, preferred_element_type=F32) > 0.5
            maskadd = jnp.where(picked & (dist >= 0), 0.0, NEG)
            k = ks_refs[g][0, :kw, :].astype(BF16)
            v = vs_refs[g][0, :kw, :].astype(BF16)
            hs = R if kw <= D_STACK_MAX_KEYS else R // 2
            for r0 in range(0, R, hs):
                q_rows = jnp.concatenate([q_refs[g][0, :, hd(r0 + r)].astype(BF16) for r in range(hs)], axis=0)
                outs = _stacked_softmax2_pv(
                    q_rows, k, v,
                    lambda r, qk, h0=g * R + r0, maskadd=maskadd: (
                        qk * (ATTN_SCALE * LOG2E) - (slope_ref[h0 + r] * LOG2E) * distf + maskadd), hs)
                for r in range(hs):
                    h = g * R + r0 + r
                    o_slc = gate[:, B_HEADS + h:B_HEADS + h + 1] * outs[r]
                    o_ref[0, :, hd(h)] = (cmp_ref[:, hd(h)] + o_slc + win_ref[:, hd(h)]).astype(o_ref.dtype)

    _for_causal_class(i, nb, body)


def _nsa_attention(y3, cmp_kv, gate_logits, slopes, n_cmp, n_slc):
    B, L, _ = y3.shape
    G, R = B_KV_GROUPS, B_GROUP_SIZE
    nc = cmp_kv.shape[2]
    w_steps = B_WINDOW - 1
    q_specs = [pl.BlockSpec((1, BLOCK, R * HEAD_DIM), lambda b, i, g=g: (b, i, CB_BQ // R + g)) for g in range(G)]
    kv_specs = [pl.BlockSpec((1, L, HEAD_DIM), lambda b, i, c=CB_BKV + (br * 2 + kv) * G + g: (b, 0, c))
                for br in (1, 2) for kv in range(2) for g in range(G)]
    hw = B_HEADS * HEAD_DIM
    return pl.pallas_call(
        functools.partial(_nsa_kernel, n_cmp=n_cmp, n_slc=n_slc, window=w_steps, n_prev=-(-w_steps // BLOCK)),
        grid=(B, L // BLOCK),
        in_specs=[pl.BlockSpec(memory_space=pltpu.SMEM)] + q_specs + kv_specs +
                 [pl.BlockSpec((1, 2 * G, nc, HEAD_DIM), lambda b, i: (b, 0, 0, 0)),
                  pl.BlockSpec((1, BLOCK, gate_logits.shape[2]), lambda b, i: (b, i, 0))],
        out_specs=pl.BlockSpec((1, BLOCK, hw), lambda b, i: (b, i, 0)),
        out_shape=jax.ShapeDtypeStruct((B, L, hw), BF16),
        scratch_shapes=[pltpu.VMEM((G, BLOCK, LANES), BF16), pltpu.VMEM((BLOCK, hw), F32),
                        pltpu.VMEM((BLOCK, hw), F32)],
        compiler_params=_cparams(("parallel", "arbitrary")),
        name="nsa_attention",
    )(slopes, *([y3] * (5 * G)), cmp_kv, gate_logits)


def _native_sparse_attention(y3, gate_logits, cmp_w1, cmp_w2, cmp_pos, slopes_all):
    B, L, _ = y3.shape
    G, R = B_KV_GROUPS, B_GROUP_SIZE
    slopes = slopes_all[np.array(B_SLOPE_IDX)]
    n_chunk = L // B_CMP_STRIDE
    n_cmp = n_chunk - B_CMP_LEN // B_CMP_STRIDE + 1
    n_slc = L // B_SEL_LEN
    assert B_CMP_LEN == 2 * B_CMP_STRIDE and n_chunk % 8 == 0 and n_slc <= LANES

    xc = y3[:, :, CB_BKV * LANES:(CB_BKV + 2 * G) * LANES].reshape(B, L, 2 * G, HEAD_DIM)
    xc = xc.transpose(0, 2, 1, 3).reshape(B, 2 * G, n_chunk, B_CMP_STRIDE * HEAD_DIM)
    cmp_kv = _compress(xc, cmp_pos.reshape(2, 1, B_CMP_LEN * HEAD_DIM), cmp_w1.astype(BF16), cmp_w2.astype(BF16))

    o = _nsa_attention(y3, cmp_kv, gate_logits.reshape(B, L, -1), slopes, n_cmp, n_slc)
    return o.reshape(B * L, B_HEADS * HEAD_DIM)


def _gated_merge_kernel(x_ref, *refs, nk):
    nbr = N_BRANCH
    wg, o, w, bias = refs[:nbr], refs[nbr:2 * nbr], refs[2 * nbr:3 * nbr], refs[3 * nbr:4 * nbr]
    out_ref, acc = refs[4 * nbr], refs[4 * nbr + 1:]

    def logits(c):
        return jnp.dot(x_ref[...], wg[c][...], preferred_element_type=F32)

    def first():
        for c in range(nbr):
            acc[0][c] = logits(c)

    def middle():
        for c in range(nbr):
            acc[0][c] += logits(c)

    def last(has_acc):
        merged = None
        for c in range(nbr):
            z = (acc[0][c] + logits(c)) if has_acc else logits(c)
            term = _sigmoid(z + bias[c][...]) * jnp.dot(o[c][...], w[c][...], preferred_element_type=F32)
            merged = term if merged is None else merged + term
        out_ref[...] = merged.astype(out_ref.dtype)

    _k_steps(nk, first, middle, last)


def _gated_merge(xb, w_gate, b_gate, branch_outs, branch_ws):
    M, K = xb.shape
    D = branch_ws[0].shape[1]
    tm, tn, tk = _tile(M, 1024), _tile(D, 256), _tile(K, 4096)
    nj, nk = D // tn, K // tk
    wg_specs = [pl.BlockSpec((tk, tn), lambda i, j, k, c=c: (k, c * nj + j)) for c in range(N_BRANCH)]
    o_specs = [pl.BlockSpec((tm, o.shape[1]), lambda i, j, k: (i, 0), pipeline_mode=pl.Buffered(1))
               for o in branch_outs]
    w_specs = [pl.BlockSpec((w.shape[0], tn), lambda i, j, k: (0, j)) for w in branch_ws]
    b_specs = [pl.BlockSpec((1, tn), lambda i, j, k, c=c: (0, c * nj + j)) for c in range(N_BRANCH)]
    return pl.pallas_call(
        functools.partial(_gated_merge_kernel, nk=nk),
        grid=(M // tm, nj, nk),
        in_specs=[pl.BlockSpec((tm, tk), lambda i, j, k: (i, k))] + wg_specs + o_specs + w_specs + b_specs,
        out_specs=pl.BlockSpec((tm, tn), lambda i, j, k: (i, j)),
        out_shape=jax.ShapeDtypeStruct((M, D), BF16),
        scratch_shapes=[pltpu.VMEM((N_BRANCH, tm, tn), F32)] if nk > 1 else [],
        compiler_params=_cparams(("parallel", "parallel", "arbitrary")),
        name="gated_merge",
    )(xb, *([w_gate] * N_BRANCH), *branch_outs, *branch_ws, *([b_gate.reshape(1, -1)] * N_BRANCH))


def _attn_weight(w_in):
    D = w_in.shape[0]
    sizes = (A_HEADS * HEAD_DIM,) * 3 + (B_HEADS * HEAD_DIM, 3 * 2 * B_KV_GROUPS * HEAD_DIM, 3 * B_HEADS) + \
            (C_HEADS * HEAD_DIM,) * 3 + (C_HEADS,) + (D_HEADS * HEAD_DIM, HEAD_DIM, HEAD_DIM) + \
            (D_IDX_HEADS * D_IDX_DIM, D_IDX_DIM, D_IDX_HEADS)
    offs = np.concatenate([[0], np.cumsum(sizes)])
    (a_q, a_k, a_v, b_q, b_kv, b_g, c_q, c_k, c_v, c_f, d_q, d_k, d_v, d_iq, d_ik, d_iw) = [
        w_in[:, offs[n]:offs[n + 1]] for n in range(len(sizes))]
    misc = jnp.zeros((D, 2 * LANES), w_in.dtype)
    misc = misc.at[:, MISC_BG:MISC_BG + 3 * B_HEADS].set(b_g)
    misc = misc.at[:, MISC_CF:MISC_CF + C_HEADS].set(c_f)
    misc = misc.at[:, MISC_IK:MISC_IK + D_IDX_DIM].set(d_ik)
    misc = misc.at[:, MISC_IW:MISC_IW + D_IDX_HEADS].set(d_iw)
    parts = [a_q, a_k, a_v, b_q, b_kv, c_q, c_k, c_v, d_q, d_iq, d_k, d_v, misc]
    w = jnp.concatenate([p.astype(BF16) for p in parts], axis=1)
    assert w.shape[1] == N_CB * LANES
    return w, int(offs[-1])


def _hybrid_mixer(xb, B, L, w_in, b_forget, b_gate, cmp_w1, cmp_w2, cmp_pos, w_branch, w_out):
    M, D = xb.shape
    slopes_all = jnp.exp2(-8.0 * jnp.arange(1, N_ALIBI + 1, dtype=F32) / N_ALIBI)
    w_attn, gate_off = _attn_weight(w_in)
    y3 = _matmul(xb, w_attn, BF16).reshape(B, L, N_CB * LANES)
    misc = _matmul(xb, w_attn[:, CB_MISC * LANES:(CB_MISC + 1) * LANES], F32)

    bias = jnp.zeros((1, LANES), F32).at[0, MISC_CF:MISC_CF + C_HEADS].set(b_forget)
    c = _logsig_cumsum(misc.reshape(B, L, LANES), bias)[:, :, MISC_CF:MISC_CF + C_HEADS].transpose(0, 2, 1)

    o_a = _dilated_attention(y3, slopes_all)
    o_b = _native_sparse_attention(y3, misc[:, MISC_BG:MISC_BG + 3 * B_HEADS], cmp_w1, cmp_w2, cmp_pos, slopes_all)
    o_c = _forgetting_attention(y3, c[..., None], c[:, :, None, :])
    o_d = _indexed_sparse_attention(y3, misc[:, MISC_IK:MISC_IK + D_IDX_DIM].reshape(B, L, D_IDX_DIM),
                                    misc[:, MISC_IW:MISC_IW + D_IDX_HEADS].reshape(B, L, D_IDX_HEADS), slopes_all)

    sizes = (A_HEADS_PER_PAIR * HEAD_DIM, B_HEADS * HEAD_DIM, C_HEADS * HEAD_DIM, D_HEADS * HEAD_DIM)
    offs = np.concatenate([[0], np.cumsum(sizes)])
    ws = [w_branch[offs[n]:offs[n + 1]] for n in range(N_BRANCH)]
    outs = [o_a, o_b, o_c.reshape(M, -1), o_d.reshape(M, -1)]
    merged = _gated_merge(xb, w_in[:, gate_off:].astype(BF16), b_gate, outs, ws)
    return _matmul(merged, w_out, F32)


def kernel(x, ln_g, ln_b, ffn1_w_gate, ffn1_w_up, ffn1_w_down, w_in, b_forget, b_gate, cmp_w1, cmp_w2, cmp_pos,
           w_branch, w_out, ffn2_w_gate, ffn2_w_up, ffn2_w_down):
    B, L, D = x.shape
    assert L % BLOCK == 0 and D % LANES == 0
    xf = x.reshape(B * L, D)
    xb = xf.astype(BF16)

    def ffn(xf, xb, wg, wu, wd, l, g, b):
        h = _ffn_up(xb, _cast_bf16(wg, l), _cast_bf16(wu, l))
        y = _matmul(h, _cast_bf16(wd, l), F32)
        return _add_ln(xf, y, g, b, 0.5)

    for l in range(ln_g.shape[0]):
        xf, xb = ffn(xf, xb, ffn1_w_gate, ffn1_w_up, ffn1_w_down, l, ln_g[l, 0], ln_b[l, 0])
        y = _hybrid_mixer(xb, B, L, w_in[l], b_forget[l], b_gate[l], cmp_w1[l], cmp_w2[l], cmp_pos[l],
                          _cast_bf16(w_branch, l), _cast_bf16(w_out, l))
        xf, xb = _add_ln(xf, y, ln_g[l, 1], ln_b[l, 1], 1.0)
        xf, xb = ffn(xf, xb, ffn2_w_gate, ffn2_w_up, ffn2_w_down, l, ln_g[l, 2], ln_b[l, 2])
    return xf.reshape(B, L, D)
```

```python
import functools

import numpy as np
import jax
import jax.numpy as jnp
from jax import lax
from jax.experimental import pallas as pl
from jax.experimental.pallas import tpu as pltpu

F32 = jnp.float32
BF16 = jnp.bfloat16

HEAD_DIM = 128
BLOCK = 128
LANES = 128
NEG = -1e30
ATTN_SCALE = HEAD_DIM ** -0.5
LOG2E = 1.4426950408889634
LN2 = 0.6931471805599453
DEPTH = 2
ALPHA = (2 * DEPTH) ** 0.25
LN_EPS = 1e-5

A_PAIRS = ((128, 1), (512, 4), (2048, 16))
A_HEADS_PER_PAIR = 4
A_HEADS = A_HEADS_PER_PAIR * len(A_PAIRS)
B_HEADS = 8
B_KV_GROUPS = 2
B_GROUP_SIZE = B_HEADS // B_KV_GROUPS
B_CMP_LEN = 32
B_CMP_STRIDE = 16
B_SEL_LEN = 64
B_SEL_RATIO = B_SEL_LEN // B_CMP_STRIDE
B_N_SEL = 8
B_WINDOW = 512
B_CMP_HIDDEN = 512
C_HEADS = 8
D_HEADS = 8
D_IDX_HEADS = 8
D_IDX_DIM = 64
D_TOPK = 256
D_HEAD_STACK = 4
D_STACK_MAX_KEYS = 768
N_BRANCH = 4
N_ALIBI = A_HEADS + B_HEADS + D_HEADS
A_SLOPE_IDX = (0, 1, 2, 3, 12, 13, 14, 15, 24, 25, 26, 27)
B_SLOPE_IDX = (4, 5, 6, 7, 8, 9, 10, 11)
D_SLOPE_IDX = (16, 17, 18, 19, 20, 21, 22, 23)

CB_A = 0
CB_BQ = 36
CB_BKV = 44
CB_CQ, CB_CK, CB_CV = 56, 64, 72
CB_DQ = 80
CB_DIQ = 88
CB_DK, CB_DV = 92, 93
CB_MISC = 94
N_CB = 96
MISC_BG, MISC_CF, MISC_IK, MISC_IW = 0, 24, 32, 96

CAUSAL_CLASSES = 8
VMEM_LIMIT = 56 * 1024 * 1024


def _cparams(sem):
    return pltpu.CompilerParams(dimension_semantics=sem, vmem_limit_bytes=VMEM_LIMIT)


def _tile(dim, pref):
    return pref if dim % pref == 0 else dim


def _dot_nt(a, b):
    return lax.dot_general(a, b, (((1,), (1,)), ((), ())), preferred_element_type=F32)


def _softmax2_pv(s2, v):
    m = jnp.max(s2, axis=-1, keepdims=True)
    e = jnp.exp2(s2 - m)
    den = jnp.maximum(jnp.sum(e, axis=-1, keepdims=True), 1e-30)
    return jnp.dot(e.astype(BF16), v, preferred_element_type=F32) / den


def _stacked_softmax2_pv(q_stack, k, v, logits2, n):
    qk = _dot_nt(q_stack, k)
    es, dens = [], []
    for r in range(n):
        s2 = logits2(r, qk[r * BLOCK:(r + 1) * BLOCK])
        e = jnp.exp2(s2 - jnp.max(s2, axis=-1, keepdims=True))
        dens.append(jnp.maximum(jnp.sum(e, axis=-1, keepdims=True), 1e-30))
        es.append(e.astype(BF16))
    pv = jnp.dot(jnp.concatenate(es, axis=0), v, preferred_element_type=F32)
    return [pv[r * BLOCK:(r + 1) * BLOCK] / dens[r] for r in range(n)]


def _for_causal_class(i, nb, body):
    n_cls = CAUSAL_CLASSES if nb % CAUSAL_CLASSES == 0 else 1
    per = nb // n_cls
    for c in range(n_cls):
        pl.when((i >= c * per) & (i < (c + 1) * per))(functools.partial(body, (c + 1) * per * BLOCK))


def _cast_kernel(x_ref, o_ref):
    o_ref[...] = x_ref[0].astype(o_ref.dtype)


def _cast_bf16(w_stack, l):
    _, R, C = w_stack.shape
    tr = R
    for cand in (1024, 512, 256, 128, 64, 32, 16):
        if R % cand == 0:
            tr = cand
            if cand * C * 4 <= 8 * 1024 * 1024:
                break
    return pl.pallas_call(
        _cast_kernel,
        grid=(R // tr,),
        in_specs=[pl.BlockSpec((1, tr, C), lambda i: (l, i, 0))],
        out_specs=pl.BlockSpec((tr, C), lambda i: (i, 0)),
        out_shape=jax.ShapeDtypeStruct((R, C), BF16),
        compiler_params=_cparams(("parallel",)),
        name="cast_bf16",
    )(w_stack)


def _sigmoid(z):
    return 0.5 * jnp.tanh(0.5 * z) + 0.5


def _k_steps(nk, first, middle, last):
    k = pl.program_id(2)
    if nk == 1:
        last(False)
        return
    pl.when(k == 0)(first)
    if nk > 2:
        pl.when((k > 0) & (k < nk - 1))(middle)
    pl.when(k == nk - 1)(functools.partial(last, True))


def _mm_kernel(x_ref, w_ref, o_ref, *acc, nk):
    def prod():
        return jnp.dot(x_ref[...], w_ref[...], preferred_element_type=F32)

    def first():
        acc[0][...] = prod()

    def middle():
        acc[0][...] += prod()

    def last(has_acc):
        o_ref[...] = ((acc[0][...] + prod()) if has_acc else prod()).astype(o_ref.dtype)

    _k_steps(nk, first, middle, last)


def _matmul(x, w, out_dtype, tiles=(1024, 1024, 4096)):
    M, K = x.shape
    N = w.shape[1]
    tm, tn, tk = _tile(M, tiles[0]), _tile(N, tiles[1]), _tile(K, tiles[2])
    nk = K // tk
    return pl.pallas_call(
        functools.partial(_mm_kernel, nk=nk),
        grid=(M // tm, N // tn, nk),
        in_specs=[pl.BlockSpec((tm, tk), lambda i, j, k: (i, k)),
                  pl.BlockSpec((tk, tn), lambda i, j, k: (k, j))],
        out_specs=pl.BlockSpec((tm, tn), lambda i, j, k: (i, j)),
        out_shape=jax.ShapeDtypeStruct((M, N), out_dtype),
        scratch_shapes=[pltpu.VMEM((tm, tn), F32)] if nk > 1 else [],
        compiler_params=_cparams(("parallel", "parallel", "arbitrary")),
        name="matmul",
    )(x, w)


def _ffn_up_kernel(x_ref, wg_ref, wu_ref, o_ref, *acc, nk):
    def prods():
        x = x_ref[...]
        return (jnp.dot(x, wg_ref[...], preferred_element_type=F32),
                jnp.dot(x, wu_ref[...], preferred_element_type=F32))

    def first():
        acc[0][...], acc[1][...] = prods()

    def middle():
        g, u = prods()
        acc[0][...] += g
        acc[1][...] += u

    def last(has_acc):
        g, u = prods()
        if has_acc:
            g, u = acc[0][...] + g, acc[1][...] + u
        o_ref[...] = (g * _sigmoid(g) * u).astype(o_ref.dtype)

    _k_steps(nk, first, middle, last)


def _ffn_up(x, wg, wu, tiles=(1024, 512, 4096)):
    M, K = x.shape
    N = wg.shape[1]
    tm, tn, tk = _tile(M, tiles[0]), _tile(N, tiles[1]), _tile(K, tiles[2])
    nk = K // tk
    return pl.pallas_call(
        functools.partial(_ffn_up_kernel, nk=nk),
        grid=(M // tm, N // tn, nk),
        in_specs=[pl.BlockSpec((tm, tk), lambda i, j, k: (i, k)),
                  pl.BlockSpec((tk, tn), lambda i, j, k: (k, j)),
                  pl.BlockSpec((tk, tn), lambda i, j, k: (k, j))],
        out_specs=pl.BlockSpec((tm, tn), lambda i, j, k: (i, j)),
        out_shape=jax.ShapeDtypeStruct((M, N), BF16),
        scratch_shapes=[pltpu.VMEM((tm, tn), F32), pltpu.VMEM((tm, tn), F32)] if nk > 1 else [],
        compiler_params=_cparams(("parallel", "parallel", "arbitrary")),
        name="ffn_up",
    )(x, wg, wu)


def _add_ln_kernel(x_ref, y_ref, g_ref, b_ref, o_ref, ob_ref, *, cy):
    z = ALPHA * x_ref[...] + cy * y_ref[...]
    mu = jnp.mean(z, axis=-1, keepdims=True)
    zc = z - mu
    var = jnp.mean(zc * zc, axis=-1, keepdims=True)
    out = zc * lax.rsqrt(var + LN_EPS) * g_ref[...] + b_ref[...]
    o_ref[...] = out
    ob_ref[...] = out.astype(BF16)


def _add_ln(x, y, g, b, cy):
    M, D = x.shape
    tm = _tile(M, 256)
    row = pl.BlockSpec((tm, D), lambda i: (i, 0))
    vec = pl.BlockSpec((1, D), lambda i: (0, 0))
    return pl.pallas_call(
        functools.partial(_add_ln_kernel, cy=cy),
        grid=(M // tm,),
        in_specs=[row, row, vec, vec],
        out_specs=[row, row],
        out_shape=[jax.ShapeDtypeStruct((M, D), F32), jax.ShapeDtypeStruct((M, D), BF16)],
        compiler_params=_cparams(("parallel",)),
        name="add_ln",
    )(x, y, g.reshape(1, D), b.reshape(1, D))


def _banded_kernel(slope_ref, q_ref, k_ref, v_ref, o_ref, lse_ref, *, window, n_prev, step, nb, hp):
    g = pl.program_id(1)
    i = pl.program_id(2)
    kwb = min(n_prev + 1, nb)
    kw = kwb * BLOCK
    start = pl.multiple_of(jnp.maximum(i - n_prev, 0) * BLOCK, BLOCK)
    qpos = i * BLOCK + lax.broadcasted_iota(jnp.int32, (BLOCK, kw), 0)
    kpos = start + lax.broadcasted_iota(jnp.int32, (BLOCK, kw), 1)
    dist = qpos - kpos
    maskadd = jnp.where((dist >= 0) & (dist <= window), 0.0, NEG)
    krel = (step * lax.broadcasted_iota(jnp.int32, (1, kw), 1)).astype(F32)
    qrel = (step * (i * BLOCK - start + lax.broadcasted_iota(jnp.int32, (BLOCK, 1), 0))).astype(F32)
    for h in range(hp):
        cs = slice(h * HEAD_DIM, (h + 1) * HEAD_DIM)
        slope = slope_ref[g * hp + h]
        q = q_ref[0, :, cs].astype(BF16)
        kwin = k_ref[0, pl.ds(start, kw), cs].astype(BF16)
        vwin = v_ref[0, pl.ds(start, kw), cs].astype(BF16)
        s = _dot_nt(q, kwin) * ATTN_SCALE + (slope * krel + maskadd)
        m = jnp.max(s, axis=-1, keepdims=True)
        e = jnp.exp(s - m)
        den = jnp.maximum(jnp.sum(e, axis=-1, keepdims=True), 1e-30)
        o_ref[0, :, cs] = jnp.dot(e.astype(BF16), vwin, preferred_element_type=F32) / den
        lse_ref[0, 0, :, h:h + 1] = m + jnp.log(den) - slope * qrel


def _banded_attention(qa, ka, va, slopes, *, n_groups, hp, q_blk, k_blk, v_blk, window, n_prev, step):
    B, N, _ = qa.shape
    nb = N // BLOCK
    qspec = pl.BlockSpec((1, BLOCK, hp * HEAD_DIM), lambda b, g, i: (b, i, q_blk(g)))
    kspec = pl.BlockSpec((1, N, hp * HEAD_DIM), lambda b, g, i: (b, 0, k_blk(g)))
    vspec = pl.BlockSpec((1, N, hp * HEAD_DIM), lambda b, g, i: (b, 0, v_blk(g)))
    ospec = pl.BlockSpec((1, BLOCK, hp * HEAD_DIM), lambda b, g, i: (b, i, g))
    oshape = jax.ShapeDtypeStruct((B, N, n_groups * hp * HEAD_DIM), F32)
    lspec = pl.BlockSpec((1, 1, BLOCK, hp), lambda b, g, i: (b, g, i, 0))
    lshape = jax.ShapeDtypeStruct((B, n_groups, N, hp), F32)
    return pl.pallas_call(
        functools.partial(_banded_kernel, window=window, n_prev=n_prev, step=step, nb=nb, hp=hp),
        grid=(B, n_groups, nb),
        in_specs=[pl.BlockSpec(memory_space=pltpu.SMEM), qspec, kspec, vspec],
        out_specs=[ospec, lspec],
        out_shape=[oshape, lshape],
        compiler_params=_cparams(("parallel", "parallel", "arbitrary")),
        name="banded_attention",
    )(slopes, qa, ka, va)


def _a_combine_kernel(o0, o1, o2, l0, l1, l2, out_ref):
    a, b, c = l0[...], l1[...], l2[...]
    m = jnp.maximum(jnp.maximum(a, b), c)
    ea, eb, ec = jnp.exp(a - m), jnp.exp(b - m), jnp.exp(c - m)
    tot = ea + eb + ec
    wa, wb, wc = ea / tot, eb / tot, ec / tot
    for h in range(a.shape[1]):
        cs = slice(h * HEAD_DIM, (h + 1) * HEAD_DIM)
        out = wa[:, h:h + 1] * o0[:, cs] + wb[:, h:h + 1] * o1[:, cs] + wc[:, h:h + 1] * o2[:, cs]
        out_ref[:, cs] = out.astype(out_ref.dtype)


def _a_combine(outs, lses):
    M, C = outs[0].shape
    tm = _tile(M, 512)
    spec = pl.BlockSpec((tm, C), lambda i: (i, 0))
    lspec = pl.BlockSpec((tm, lses[0].shape[1]), lambda i: (i, 0))
    return pl.pallas_call(
        _a_combine_kernel,
        grid=(M // tm,),
        in_specs=[spec] * 3 + [lspec] * 3,
        out_specs=spec,
        out_shape=jax.ShapeDtypeStruct((M, C), BF16),
        compiler_params=_cparams(("parallel",)),
        name="a_combine",
    )(*outs, *lses)


def _dilated_attention(y3, slopes_all):
    B, L, _ = y3.shape
    slopes = slopes_all[np.array(A_SLOPE_IDX)]
    hpp = A_HEADS_PER_PAIR
    gw = 3 * hpp * HEAD_DIM
    outs, lses = [], []
    for g, (window, dil) in enumerate(A_PAIRS):
        n = L // dil
        assert n % BLOCK == 0
        steps = window // dil
        if dil == 1:
            ya, base = y3, CB_A // hpp + g
            blk = lambda part: (lambda r: base + part * (A_HEADS // hpp))
        else:
            cols = [y3[:, :, (CB_A + p * A_HEADS + g * hpp) * LANES:(CB_A + p * A_HEADS + (g + 1) * hpp) * LANES]
                    for p in range(3)]
            ya = jnp.concatenate(cols, axis=-1).reshape(B, n, dil * gw)
            blk = lambda part: (lambda r: r * 3 + part)
        o, lse = _banded_attention(ya, ya, ya, jnp.tile(slopes[g * hpp:(g + 1) * hpp], dil),
                                   n_groups=dil, hp=hpp, q_blk=blk(0), k_blk=blk(1), v_blk=blk(2),
                                   window=steps, n_prev=-(-steps // BLOCK), step=dil)
        outs.append(o.reshape(B * L, hpp * HEAD_DIM))
        lses.append(lse.transpose(0, 2, 1, 3).reshape(B * L, hpp))
    return _a_combine(outs, lses)


def _logsig_cumsum_kernel(z_ref, bias_ref, o_ref):
    L = z_ref.shape[1]
    row = lax.broadcasted_iota(jnp.int32, (BLOCK, BLOCK), 0)
    col = lax.broadcasted_iota(jnp.int32, (BLOCK, BLOCK), 1)
    tri = jnp.where(row >= col, 1.0, 0.0).astype(F32)
    carry = jnp.zeros((1, LANES), F32)
    for j in range(L // BLOCK):
        z = z_ref[0, j * BLOCK:(j + 1) * BLOCK, :] + bias_ref[...]
        ls = jnp.minimum(z, 0.0) - jnp.log(1.0 + jnp.exp(-jnp.abs(z)))
        c = jnp.dot(tri, ls, preferred_element_type=F32, precision=lax.Precision.HIGHEST) + carry
        o_ref[0, j * BLOCK:(j + 1) * BLOCK, :] = c
        carry = c[BLOCK - 1:BLOCK, :]


def _logsig_cumsum(z, bias):
    B, L, _ = z.shape
    spec = pl.BlockSpec((1, L, LANES), lambda b: (b, 0, 0))
    return pl.pallas_call(
        _logsig_cumsum_kernel,
        grid=(B,),
        in_specs=[spec, pl.BlockSpec((1, LANES), lambda b: (0, 0))],
        out_specs=spec,
        out_shape=jax.ShapeDtypeStruct((B, L, LANES), F32),
        compiler_params=_cparams(("parallel",)),
        name="logsig_cumsum",
    )(z, bias)


def _fox_kernel(q_ref, k_ref, v_ref, cq_ref, ck_ref, o_ref):
    i = pl.program_id(1)
    nb = k_ref.shape[1] // BLOCK

    def body(kw):
        qpos = i * BLOCK + lax.broadcasted_iota(jnp.int32, (BLOCK, kw), 0)
        kpos = lax.broadcasted_iota(jnp.int32, (BLOCK, kw), 1)
        maskadd = jnp.where(qpos >= kpos, 0.0, NEG)
        for h in range(C_HEADS):
            cs = slice(h * HEAD_DIM, (h + 1) * HEAD_DIM)
            q = q_ref[0, :, cs].astype(BF16)
            k = k_ref[0, :kw, cs].astype(BF16)
            v = v_ref[0, :kw, cs].astype(BF16)
            s2 = _dot_nt(q, k) * (ATTN_SCALE * LOG2E) + cq_ref[0, h] * LOG2E - ck_ref[0, h, :, :kw] * LOG2E
            o_ref[0, :, cs] = _softmax2_pv(s2 + maskadd, v).astype(o_ref.dtype)

    _for_causal_class(i, nb, body)


def _forgetting_attention(y3, c_col, c_row):
    B, L, _ = y3.shape
    nb = L // BLOCK
    cw = C_HEADS * HEAD_DIM
    return pl.pallas_call(
        _fox_kernel,
        grid=(B, nb),
        in_specs=[pl.BlockSpec((1, BLOCK, cw), lambda b, i: (b, i, CB_CQ // C_HEADS)),
                  pl.BlockSpec((1, L, cw), lambda b, i: (b, 0, CB_CK // C_HEADS)),
                  pl.BlockSpec((1, L, cw), lambda b, i: (b, 0, CB_CV // C_HEADS)),
                  pl.BlockSpec((1, C_HEADS, BLOCK, 1), lambda b, i: (b, 0, i, 0)),
                  pl.BlockSpec((1, C_HEADS, 1, L), lambda b, i: (b, 0, 0, 0))],
        out_specs=pl.BlockSpec((1, BLOCK, cw), lambda b, i: (b, i, 0)),
        out_shape=jax.ShapeDtypeStruct((B, L, cw), BF16),
        compiler_params=_cparams(("parallel", "arbitrary")),
        name="forgetting_attention",
    )(y3, y3, y3, c_col, c_row)


def _order_key(x):
    bits = lax.bitcast_convert_type(x, jnp.int32)
    return bits ^ ((bits >> 31) & jnp.int32(0x7FFFFFFF))


def _kth_largest_key(key, k):
    rows = key.shape[0]

    def count_ge(t):
        return jnp.sum(jnp.where(key >= t, 1.0, 0.0), axis=-1, keepdims=True)

    t0 = jnp.where(count_ge(jnp.zeros((rows, 1), jnp.int32)) >= k,
                   jnp.int32(0), jnp.int32(-2 ** 31)) + jnp.zeros((rows, 1), jnp.int32)

    def two_bits(it, t):
        lo = jnp.left_shift(jnp.int32(1), 29 - 2 * it)
        c1, c2 = t | lo, t | (lo + lo)
        c3 = c2 | lo
        n1, n2, n3 = count_ge(c1), count_ge(c2), count_ge(c3)
        return jnp.where(n3 >= k, c3, jnp.where(n2 >= k, c2, jnp.where(n1 >= k, c1, t)))

    t = lax.fori_loop(0, 15, two_bits, t0)
    last = t | jnp.int32(1)
    return jnp.where(count_ge(last) >= k, last, t)


def _dsa_kernel(slope_ref, iq_ref, ik_ref, iw_ref, q_ref, k_ref, v_ref, o_ref, *, n_top):
    i = pl.program_id(1)
    nb = k_ref.shape[1] // BLOCK
    iw = iw_ref[0]

    def body(kw):
        ik = ik_ref[0, :kw, :].astype(BF16)
        score = jnp.zeros((BLOCK, kw), F32)
        hs = D_HEAD_STACK
        for h0 in range(0, D_IDX_HEADS, hs):
            iq_rows = jnp.concatenate([iq_ref[0, :, (h0 + r) * D_IDX_DIM:(h0 + r + 1) * D_IDX_DIM].astype(BF16)
                                       for r in range(hs)], axis=0)
            z = _dot_nt(iq_rows, ik)
            for r in range(hs):
                rel = jnp.maximum(z[r * BLOCK:(r + 1) * BLOCK], 0.0)
                score = score + iw[:, h0 + r:h0 + r + 1] * rel
        qpos = i * BLOCK + lax.broadcasted_iota(jnp.int32, (BLOCK, kw), 0)
        kpos = lax.broadcasted_iota(jnp.int32, (BLOCK, kw), 1)
        dist = qpos - kpos
        causal = dist >= 0
        key = _order_key(jnp.where(causal, score, NEG))

        thr = _kth_largest_key(key, float(n_top))
        gt = key > thr
        eq = key == thr
        need = float(n_top) - jnp.sum(jnp.where(gt, 1.0, 0.0), axis=-1, keepdims=True)
        row = lax.broadcasted_iota(jnp.int32, (BLOCK, BLOCK), 0)
        col = lax.broadcasted_iota(jnp.int32, (BLOCK, BLOCK), 1)
        upper = jnp.where(row <= col, 1.0, 0.0).astype(BF16)
        eqf = jnp.where(eq, 1.0, 0.0).astype(BF16)
        carry = jnp.zeros((BLOCK, 1), F32)
        rank_tiles = []
        for j in range(kw // BLOCK):
            rank = carry + jnp.dot(eqf[:, j * BLOCK:(j + 1) * BLOCK], upper, preferred_element_type=F32)
            rank_tiles.append(rank)
            carry = rank[:, BLOCK - 1:BLOCK]
        rank = jnp.concatenate(rank_tiles, axis=1)
        mask = (gt | (eq & (rank <= need))) & causal
        maskadd = jnp.where(mask, 0.0, NEG)

        k = k_ref[0, :kw, :].astype(BF16)
        v = v_ref[0, :kw, :].astype(BF16)
        distf = dist.astype(F32)
        hs = D_HEAD_STACK if kw <= D_STACK_MAX_KEYS else D_HEAD_STACK // 2
        for h0 in range(0, D_HEADS, hs):
            q_rows = jnp.concatenate([q_ref[0, :, (h0 + r) * HEAD_DIM:(h0 + r + 1) * HEAD_DIM].astype(BF16)
                                      for r in range(hs)], axis=0)
            outs = _stacked_softmax2_pv(
                q_rows, k, v,
                lambda r, qk, h0=h0: (qk * (ATTN_SCALE * LOG2E) - (slope_ref[h0 + r] * LOG2E) * distf + maskadd), hs)
            for r in range(hs):
                o_ref[0, :, (h0 + r) * HEAD_DIM:(h0 + r + 1) * HEAD_DIM] = outs[r].astype(o_ref.dtype)

    _for_causal_class(i, nb, body)


def _indexed_sparse_attention(y3, ik, iw, slopes_all):
    B, L, _ = y3.shape
    nb = L // BLOCK
    n_top = min(D_TOPK, L // 4)
    slopes = slopes_all[np.array(D_SLOPE_IDX)]
    qw, iqw = D_HEADS * HEAD_DIM, D_IDX_HEADS * D_IDX_DIM
    return pl.pallas_call(
        functools.partial(_dsa_kernel, n_top=n_top),
        grid=(B, nb),
        in_specs=[pl.BlockSpec(memory_space=pltpu.SMEM),
                  pl.BlockSpec((1, BLOCK, iqw), lambda b, i: (b, i, CB_DIQ * LANES // iqw)),
                  pl.BlockSpec((1, L, D_IDX_DIM), lambda b, i: (b, 0, 0)),
                  pl.BlockSpec((1, BLOCK, D_IDX_HEADS), lambda b, i: (b, i, 0)),
                  pl.BlockSpec((1, BLOCK, qw), lambda b, i: (b, i, CB_DQ * LANES // qw)),
                  pl.BlockSpec((1, L, HEAD_DIM), lambda b, i: (b, 0, CB_DK)),
                  pl.BlockSpec((1, L, HEAD_DIM), lambda b, i: (b, 0, CB_DV))],
        out_specs=pl.BlockSpec((1, BLOCK, qw), lambda b, i: (b, i, 0)),
        out_shape=jax.ShapeDtypeStruct((B, L, qw), BF16),
        compiler_params=_cparams(("parallel", "arbitrary")),
        name="indexed_sparse_attention",
    )(slopes, y3, ik, iw, y3, y3, y3)


def _compress_kernel(x_ref, pos_ref, w1_ref, w2_ref, o_ref):
    x = x_ref[0, 0].astype(F32)
    nxt = pltpu.roll(x, x.shape[0] - 1, 0)
    blk = jnp.concatenate([x, nxt], axis=1) + pos_ref[0]
    h = jnp.dot(blk.astype(BF16), w1_ref[0], preferred_element_type=F32)
    h = jax.nn.gelu(h, approximate=True)
    o_ref[0, 0] = jnp.dot(h.astype(BF16), w2_ref[0], preferred_element_type=F32).astype(o_ref.dtype)


def _compress(xc, pos, w1, w2):
    B, _, n_chunk, cw = xc.shape
    G = B_KV_GROUPS
    return pl.pallas_call(
        _compress_kernel,
        grid=(2 * G, B),
        in_specs=[pl.BlockSpec((1, 1, n_chunk, cw), lambda a, b: (b, a, 0, 0)),
                  pl.BlockSpec((1, 1, 2 * cw), lambda a, b: (a // G, 0, 0)),
                  pl.BlockSpec((1, 2 * cw, B_CMP_HIDDEN), lambda a, b: (a // G, 0, 0)),
                  pl.BlockSpec((1, B_CMP_HIDDEN, HEAD_DIM), lambda a, b: (a // G, 0, 0))],
        out_specs=pl.BlockSpec((1, 1, n_chunk, HEAD_DIM), lambda a, b: (b, a, 0, 0)),
        out_shape=jax.ShapeDtypeStruct((B, 2 * G, n_chunk, HEAD_DIM), BF16),
        compiler_params=_cparams(("arbitrary", "arbitrary")),
        name="nsa_compress",
    )(xc, pos, w1, w2)


def _nsa_kernel(slope_ref, *refs, n_cmp, n_slc, window, n_prev):
    G, R = B_KV_GROUPS, B_GROUP_SIZE
    q_refs, ks_refs, vs_refs, kw_refs, vw_refs = (refs[n * G:(n + 1) * G] for n in range(5))
    ckv_ref, gl_ref, o_ref, sel_ref, cmp_ref, win_ref = refs[5 * G:]
    i = pl.program_id(1)
    nb = ks_refs[0].shape[1] // BLOCK
    gate = _sigmoid(gl_ref[0])
    hd = lambda h: slice(h * HEAD_DIM, (h + 1) * HEAD_DIM)

    kwb = min(n_prev + 1, nb)
    wlen = kwb * BLOCK
    start = pl.multiple_of(jnp.maximum(i - n_prev, 0) * BLOCK, BLOCK)
    wq = i * BLOCK + lax.broadcasted_iota(jnp.int32, (BLOCK, wlen), 0)
    wk = start + lax.broadcasted_iota(jnp.int32, (BLOCK, wlen), 1)
    wdist = wq - wk
    wmask = jnp.where((wdist >= 0) & (wdist <= window), 0.0, NEG)
    wdistf = wdist.astype(F32)
    rows = lambda r: slice(r * BLOCK, (r + 1) * BLOCK)
    q_stack = [jnp.concatenate([q_refs[g][0, :, hd(r)].astype(BF16) for r in range(R)], axis=0) for g in range(G)]
    for g in range(G):
        kwin = kw_refs[g][0, pl.ds(start, wlen), :].astype(BF16)
        vwin = vw_refs[g][0, pl.ds(start, wlen), :].astype(BF16)
        outs = _stacked_softmax2_pv(
            q_stack[g], kwin, vwin,
            lambda r, qk, g=g: qk * (ATTN_SCALE * LOG2E) - (slope_ref[g * R + r] * LOG2E) * wdistf + wmask, R)
        for r in range(R):
            h = g * R + r
            win_ref[:, hd(h)] = gate[:, 2 * B_HEADS + h:2 * B_HEADS + h + 1] * outs[r]

    nc = ckv_ref.shape[2]
    t = i * BLOCK + lax.broadcasted_iota(jnp.int32, (BLOCK, nc), 0)
    n = lax.broadcasted_iota(jnp.int32, (BLOCK, nc), 1)
    dist_c = t - (n * B_CMP_STRIDE + B_CMP_LEN - 1)
    mask = (dist_c >= 0) & (n < n_cmp)
    distf = dist_c.astype(F32)

    nn = lax.broadcasted_iota(jnp.int32, (nc, LANES), 0)
    jj = lax.broadcasted_iota(jnp.int32, (nc, LANES), 1)
    off = nn - B_SEL_RATIO * jj + 1
    w = jnp.where((off == 0) | (off == B_SEL_RATIO), 1.0, jnp.where((off > 0) & (off < B_SEL_RATIO), 2.0, 0.0))
    w = jnp.where((nn < n_cmp) & (jj < n_slc), w, 0.0).astype(F32)
    tq = i * BLOCK + lax.broadcasted_iota(jnp.int32, (BLOCK, LANES), 0)
    j = lax.broadcasted_iota(jnp.int32, (BLOCK, LANES), 1)
    cur = tq >> int(np.log2(B_SEL_LEN))
    forced = (j == 0) | (j == cur) | (j == cur - 1)
    scores = []
    for g in range(G):
        kc = ckv_ref[0, g]
        vc = ckv_ref[0, G + g]
        imp = jnp.zeros((BLOCK, nc), F32)
        qk = _dot_nt(q_stack[g], kc)
        ps = []
        for r in range(R):
            s = qk[rows(r)] * ATTN_SCALE
            s = jnp.where(mask, s - slope_ref[g * R + r] * distf, NEG)
            m = jnp.max(s, axis=-1, keepdims=True)
            e = jnp.where(mask, jnp.exp(s - m), 0.0)
            p = e / jnp.maximum(jnp.sum(e, axis=-1, keepdims=True), 1e-30)
            ps.append(p.astype(BF16))
            imp = imp + p
        pv = jnp.dot(jnp.concatenate(ps, axis=0), vc, preferred_element_type=F32)
        for r in range(R):
            h = g * R + r
            cmp_ref[:, hd(h)] = gate[:, h:h + 1] * pv[rows(r)]
        p_slc = jnp.dot(imp, w, preferred_element_type=F32, precision=lax.Precision.HIGHEST)
        score = jnp.where(forced, 1e9, jnp.where(j <= cur, p_slc, -1e9))
        scores.append(jnp.where(j < n_slc, score, -3e38))

    score = jnp.concatenate(scores, axis=0)
    forced2 = jnp.concatenate([forced] * G, axis=0)
    jf = lax.broadcasted_iota(jnp.int32, score.shape, 1).astype(F32)
    sel = jnp.where(forced2, 1.0, 0.0)
    score = jnp.where(forced2, -3e38, score)
    n_forced = 3
    for _ in range(min(B_N_SEL, n_slc) - n_forced):
        m = jnp.max(score, axis=-1, keepdims=True)
        first = jnp.min(jnp.where(score == m, jf, float(LANES)), axis=-1, keepdims=True)
        hit = jf == first
        sel = jnp.where(hit, 1.0, sel)
        score = jnp.where(hit, -3e38, score)
    for g in range(G):
        sel_ref[g] = sel[g * BLOCK:(g + 1) * BLOCK].astype(sel_ref.dtype)

    def body(kw):
        jj = lax.broadcasted_iota(jnp.int32, (LANES, kw), 0)
        ss = lax.broadcasted_iota(jnp.int32, (LANES, kw), 1)
        expand = jnp.where((ss >> int(np.log2(B_SEL_LEN))) == jj, 1.0, 0.0).astype(BF16)
        qpos = i * BLOCK + lax.broadcasted_iota(jnp.int32, (BLOCK, kw), 0)
        kpos = lax.broadcasted_iota(jnp.int32, (BLOCK, kw), 1)
        dist = qpos - kpos
        distf = dist.astype(F32)
        for g in range(G):
            picked = jnp.dot(sel_ref[g], expand, preferred_element_type=F32) > 0.5
            maskadd = jnp.where(picked & (dist >= 0), 0.0, NEG)
            k = ks_refs[g][0, :kw, :].astype(BF16)
            v = vs_refs[g][0, :kw, :].astype(BF16)
            hs = R if kw <= D_STACK_MAX_KEYS else R // 2
            for r0 in range(0, R, hs):
                q_rows = jnp.concatenate([q_refs[g][0, :, hd(r0 + r)].astype(BF16) for r in range(hs)], axis=0)
                outs = _stacked_softmax2_pv(
                    q_rows, k, v,
                    lambda r, qk, h0=g * R + r0, maskadd=maskadd: (
                        qk * (ATTN_SCALE * LOG2E) - (slope_ref[h0 + r] * LOG2E) * distf + maskadd), hs)
                for r in range(hs):
                    h = g * R + r0 + r
                    o_slc = gate[:, B_HEADS + h:B_HEADS + h + 1] * outs[r]
                    o_ref[0, :, hd(h)] = (cmp_ref[:, hd(h)] + o_slc + win_ref[:, hd(h)]).astype(o_ref.dtype)

    _for_causal_class(i, nb, body)


def _nsa_attention(y3, cmp_kv, gate_logits, slopes, n_cmp, n_slc):
    B, L, _ = y3.shape
    G, R = B_KV_GROUPS, B_GROUP_SIZE
    nc = cmp_kv.shape[2]
    w_steps = B_WINDOW - 1
    q_specs = [pl.BlockSpec((1, BLOCK, R * HEAD_DIM), lambda b, i, g=g: (b, i, CB_BQ // R + g)) for g in range(G)]
    kv_specs = [pl.BlockSpec((1, L, HEAD_DIM), lambda b, i, c=CB_BKV + (br * 2 + kv) * G + g: (b, 0, c))
                for br in (1, 2) for kv in range(2) for g in range(G)]
    hw = B_HEADS * HEAD_DIM
    return pl.pallas_call(
        functools.partial(_nsa_kernel, n_cmp=n_cmp, n_slc=n_slc, window=w_steps, n_prev=-(-w_steps // BLOCK)),
        grid=(B, L // BLOCK),
        in_specs=[pl.BlockSpec(memory_space=pltpu.SMEM)] + q_specs + kv_specs +
                 [pl.BlockSpec((1, 2 * G, nc, HEAD_DIM), lambda b, i: (b, 0, 0, 0)),
                  pl.BlockSpec((1, BLOCK, gate_logits.shape[2]), lambda b, i: (b, i, 0))],
        out_specs=pl.BlockSpec((1, BLOCK, hw), lambda b, i: (b, i, 0)),
        out_shape=jax.ShapeDtypeStruct((B, L, hw), BF16),
        scratch_shapes=[pltpu.VMEM((G, BLOCK, LANES), BF16), pltpu.VMEM((BLOCK, hw), F32),
                        pltpu.VMEM((BLOCK, hw), F32)],
        compiler_params=_cparams(("parallel", "arbitrary")),
        name="nsa_attention",
    )(slopes, *([y3] * (5 * G)), cmp_kv, gate_logits)


def _native_sparse_attention(y3, gate_logits, cmp_w1, cmp_w2, cmp_pos, slopes_all):
    B, L, _ = y3.shape
    G, R = B_KV_GROUPS, B_GROUP_SIZE
    slopes = slopes_all[np.array(B_SLOPE_IDX)]
    n_chunk = L // B_CMP_STRIDE
    n_cmp = n_chunk - B_CMP_LEN // B_CMP_STRIDE + 1
    n_slc = L // B_SEL_LEN
    assert B_CMP_LEN == 2 * B_CMP_STRIDE and n_chunk % 8 == 0 and n_slc <= LANES

    xc = y3[:, :, CB_BKV * LANES:(CB_BKV + 2 * G) * LANES].reshape(B, L, 2 * G, HEAD_DIM)
    xc = xc.transpose(0, 2, 1, 3).reshape(B, 2 * G, n_chunk, B_CMP_STRIDE * HEAD_DIM)
    cmp_kv = _compress(xc, cmp_pos.reshape(2, 1, B_CMP_LEN * HEAD_DIM), cmp_w1.astype(BF16), cmp_w2.astype(BF16))

    o = _nsa_attention(y3, cmp_kv, gate_logits.reshape(B, L, -1), slopes, n_cmp, n_slc)
    return o.reshape(B * L, B_HEADS * HEAD_DIM)


def _gated_merge_kernel(x_ref, *refs, nk):
    nbr = N_BRANCH
    wg, o, w, bias = refs[:nbr], refs[nbr:2 * nbr], refs[2 * nbr:3 * nbr], refs[3 * nbr:4 * nbr]
    out_ref, acc = refs[4 * nbr], refs[4 * nbr + 1:]

    def logits(c):
        return jnp.dot(x_ref[...], wg[c][...], preferred_element_type=F32)

    def first():
        for c in range(nbr):
            acc[0][c] = logits(c)

    def middle():
        for c in range(nbr):
            acc[0][c] += logits(c)

    def last(has_acc):
        merged = None
        for c in range(nbr):
            z = (acc[0][c] + logits(c)) if has_acc else logits(c)
            term = _sigmoid(z + bias[c][...]) * jnp.dot(o[c][...], w[c][...], preferred_element_type=F32)
            merged = term if merged is None else merged + term
        out_ref[...] = merged.astype(out_ref.dtype)

    _k_steps(nk, first, middle, last)


def _gated_merge(xb, w_gate, b_gate, branch_outs, branch_ws):
    M, K = xb.shape
    D = branch_ws[0].shape[1]
    tm, tn, tk = _tile(M, 1024), _tile(D, 256), _tile(K, 4096)
    nj, nk = D // tn, K // tk
    wg_specs = [pl.BlockSpec((tk, tn), lambda i, j, k, c=c: (k, c * nj + j)) for c in range(N_BRANCH)]
    o_specs = [pl.BlockSpec((tm, o.shape[1]), lambda i, j, k: (i, 0), pipeline_mode=pl.Buffered(1))
               for o in branch_outs]
    w_specs = [pl.BlockSpec((w.shape[0], tn), lambda i, j, k: (0, j)) for w in branch_ws]
    b_specs = [pl.BlockSpec((1, tn), lambda i, j, k, c=c: (0, c * nj + j)) for c in range(N_BRANCH)]
    return pl.pallas_call(
        functools.partial(_gated_merge_kernel, nk=nk),
        grid=(M // tm, nj, nk),
        in_specs=[pl.BlockSpec((tm, tk), lambda i, j, k: (i, k))] + wg_specs + o_specs + w_specs + b_specs,
        out_specs=pl.BlockSpec((tm, tn), lambda i, j, k: (i, j)),
        out_shape=jax.ShapeDtypeStruct((M, D), BF16),
        scratch_shapes=[pltpu.VMEM((N_BRANCH, tm, tn), F32)] if nk > 1 else [],
        compiler_params=_cparams(("parallel", "parallel", "arbitrary")),
        name="gated_merge",
    )(xb, *([w_gate] * N_BRANCH), *branch_outs, *branch_ws, *([b_gate.reshape(1, -1)] * N_BRANCH))


def _attn_weight(w_in):
    D = w_in.shape[0]
    sizes = (A_HEADS * HEAD_DIM,) * 3 + (B_HEADS * HEAD_DIM, 3 * 2 * B_KV_GROUPS * HEAD_DIM, 3 * B_HEADS) + \
            (C_HEADS * HEAD_DIM,) * 3 + (C_HEADS,) + (D_HEADS * HEAD_DIM, HEAD_DIM, HEAD_DIM) + \
            (D_IDX_HEADS * D_IDX_DIM, D_IDX_DIM, D_IDX_HEADS)
    offs = np.concatenate([[0], np.cumsum(sizes)])
    (a_q, a_k, a_v, b_q, b_kv, b_g, c_q, c_k, c_v, c_f, d_q, d_k, d_v, d_iq, d_ik, d_iw) = [
        w_in[:, offs[n]:offs[n + 1]] for n in range(len(sizes))]
    misc = jnp.zeros((D, 2 * LANES), w_in.dtype)
    misc = misc.at[:, MISC_BG:MISC_BG + 3 * B_HEADS].set(b_g)
    misc = misc.at[:, MISC_CF:MISC_CF + C_HEADS].set(c_f)
    misc = misc.at[:, MISC_IK:MISC_IK + D_IDX_DIM].set(d_ik)
    misc = misc.at[:, MISC_IW:MISC_IW + D_IDX_HEADS].set(d_iw)
    parts = [a_q, a_k, a_v, b_q, b_kv, c_q, c_k, c_v, d_q, d_iq, d_k, d_v, misc]
    w = jnp.concatenate([p.astype(BF16) for p in parts], axis=1)
    assert w.shape[1] == N_CB * LANES
    return w, int(offs[-1])


def _hybrid_mixer(xb, B, L, w_in, b_forget, b_gate, cmp_w1, cmp_w2, cmp_pos, w_branch, w_out):
    M, D = xb.shape
    slopes_all = jnp.exp2(-8.0 * jnp.arange(1, N_ALIBI + 1, dtype=F32) / N_ALIBI)
    w_attn, gate_off = _attn_weight(w_in)
    y3 = _matmul(xb, w_attn, BF16).reshape(B, L, N_CB * LANES)
    misc = _matmul(xb, w_attn[:, CB_MISC * LANES:(CB_MISC + 1) * LANES], F32)

    bias = jnp.zeros((1, LANES), F32).at[0, MISC_CF:MISC_CF + C_HEADS].set(b_forget)
    c = _logsig_cumsum(misc.reshape(B, L, LANES), bias)[:, :, MISC_CF:MISC_CF + C_HEADS].transpose(0, 2, 1)

    o_a = _dilated_attention(y3, slopes_all)
    o_b = _native_sparse_attention(y3, misc[:, MISC_BG:MISC_BG + 3 * B_HEADS], cmp_w1, cmp_w2, cmp_pos, slopes_all)
    o_c = _forgetting_attention(y3, c[..., None], c[:, :, None, :])
    o_d = _indexed_sparse_attention(y3, misc[:, MISC_IK:MISC_IK + D_IDX_DIM].reshape(B, L, D_IDX_DIM),
                                    misc[:, MISC_IW:MISC_IW + D_IDX_HEADS].reshape(B, L, D_IDX_HEADS), slopes_all)

    sizes = (A_HEADS_PER_PAIR * HEAD_DIM, B_HEADS * HEAD_DIM, C_HEADS * HEAD_DIM, D_HEADS * HEAD_DIM)
    offs = np.concatenate([[0], np.cumsum(sizes)])
    ws = [w_branch[offs[n]:offs[n + 1]] for n in range(N_BRANCH)]
    outs = [o_a, o_b, o_c.reshape(M, -1), o_d.reshape(M, -1)]
    merged = _gated_merge(xb, w_in[:, gate_off:].astype(BF16), b_gate, outs, ws)
    return _matmul(merged, w_out, F32)


def kernel(x, ln_g, ln_b, ffn1_w_gate, ffn1_w_up, ffn1_w_down, w_in, b_forget, b_gate, cmp_w1, cmp_w2, cmp_pos,
           w_branch, w_out, ffn2_w_gate, ffn2_w_up, ffn2_w_down):
    B, L, D = x.shape
    assert L % BLOCK == 0 and D % LANES == 0
    xf = x.reshape(B * L, D)
    xb = xf.astype(BF16)

    def ffn(xf, xb, wg, wu, wd, l, g, b):
        h = _ffn_up(xb, _cast_bf16(wg, l), _cast_bf16(wu, l))
        y = _matmul(h, _cast_bf16(wd, l), F32)
        return _add_ln(xf, y, g, b, 0.5)

    for l in range(ln_g.shape[0]):
        xf, xb = ffn(xf, xb, ffn1_w_gate, ffn1_w_up, ffn1_w_down, l, ln_g[l, 0], ln_b[l, 0])
        y = _hybrid_mixer(xb, B, L, w_in[l], b_forget[l], b_gate[l], cmp_w1[l], cmp_w2[l], cmp_pos[l],
                          _cast_bf16(w_branch, l), _cast_bf16(w_out, l))
        xf, xb = _add_ln(xf, y, ln_g[l, 1], ln_b[l, 1], 1.0)
        xf, xb = ffn(xf, xb, ffn2_w_gate, ffn2_w_up, ffn2_w_down, l, ln_g[l, 2], ln_b[l, 2])
    return xf.reshape(B, L, D)
```

```python
import functools

import numpy as np
import jax
import jax.numpy as jnp
from jax import lax
from jax.experimental import pallas as pl
from jax.experimental.pallas import tpu as pltpu

F32 = jnp.float32
BF16 = jnp.bfloat16

HEAD_DIM = 128
BLOCK = 128
LANES = 128
NEG = -1e30
ATTN_SCALE = HEAD_DIM ** -0.5
LOG2E = 1.4426950408889634
LN2 = 0.6931471805599453
DEPTH = 2
ALPHA = (2 * DEPTH) ** 0.25
LN_EPS = 1e-5

A_PAIRS = ((128, 1), (512, 4), (2048, 16))
A_HEADS_PER_PAIR = 4
A_HEADS = A_HEADS_PER_PAIR * len(A_PAIRS)
B_HEADS = 8
B_KV_GROUPS = 2
B_GROUP_SIZE = B_HEADS // B_KV_GROUPS
B_CMP_LEN = 32
B_CMP_STRIDE = 16
B_SEL_LEN = 64
B_SEL_RATIO = B_SEL_LEN // B_CMP_STRIDE
B_N_SEL = 8
B_WINDOW = 512
B_CMP_HIDDEN = 512
C_HEADS = 8
D_HEADS = 8
D_IDX_HEADS = 8
D_IDX_DIM = 64
D_TOPK = 256
D_HEAD_STACK = 4
D_STACK_MAX_KEYS = 768
N_BRANCH = 4
N_ALIBI = A_HEADS + B_HEADS + D_HEADS
A_SLOPE_IDX = (0, 1, 2, 3, 12, 13, 14, 15, 24, 25, 26, 27)
B_SLOPE_IDX = (4, 5, 6, 7, 8, 9, 10, 11)
D_SLOPE_IDX = (16, 17, 18, 19, 20, 21, 22, 23)

N_CB_A = 3 * A_HEADS
CB_BQ = 0
CB_CQ, CB_CK, CB_CV = 8, 16, 24
CB_DQ = 32
CB_BKV = 40
CB_DIQ = 52
CB_DK, CB_DV = 56, 57
CB_MISC = 58
N_CB = 60
MISC_BG, MISC_CF, MISC_IK, MISC_IW = 0, 24, 32, 96

CAUSAL_CLASSES = 8
VMEM_LIMIT = 56 * 1024 * 1024


def _cparams(sem):
    return pltpu.CompilerParams(dimension_semantics=sem, vmem_limit_bytes=VMEM_LIMIT)


def _tile(dim, pref):
    return pref if dim % pref == 0 else dim


def _dot_nt(a, b):
    return lax.dot_general(a, b, (((1,), (1,)), ((), ())), preferred_element_type=F32)


def _softmax2_pv(s2, v):
    m = jnp.max(s2, axis=-1, keepdims=True)
    e = jnp.exp2(s2 - m)
    den = jnp.maximum(jnp.sum(e, axis=-1, keepdims=True), 1e-30)
    return jnp.dot(e.astype(BF16), v, preferred_element_type=F32) / den


def _stacked_softmax2_pv(q_stack, k, v, logits2, n):
    qk = _dot_nt(q_stack, k)
    es, dens = [], []
    for r in range(n):
        s2 = logits2(r, qk[r * BLOCK:(r + 1) * BLOCK])
        e = jnp.exp2(s2 - jnp.max(s2, axis=-1, keepdims=True))
        dens.append(jnp.maximum(jnp.sum(e, axis=-1, keepdims=True), 1e-30))
        es.append(e.astype(BF16))
    pv = jnp.dot(jnp.concatenate(es, axis=0), v, preferred_element_type=F32)
    return [pv[r * BLOCK:(r + 1) * BLOCK] / dens[r] for r in range(n)]


def _for_causal_class(i, nb, body):
    n_cls = CAUSAL_CLASSES if nb % CAUSAL_CLASSES == 0 else 1
    per = nb // n_cls
    for c in range(n_cls):
        pl.when((i >= c * per) & (i < (c + 1) * per))(functools.partial(body, (c + 1) * per * BLOCK))


def _cast_kernel(x_ref, o_ref):
    o_ref[...] = x_ref[0].astype(o_ref.dtype)


def _cast_bf16(w_stack, l):
    _, R, C = w_stack.shape
    tr = R
    for cand in (1024, 512, 256, 128, 64, 32, 16):
        if R % cand == 0:
            tr = cand
            if cand * C * 4 <= 8 * 1024 * 1024:
                break
    return pl.pallas_call(
        _cast_kernel,
        grid=(R // tr,),
        in_specs=[pl.BlockSpec((1, tr, C), lambda i: (l, i, 0))],
        out_specs=pl.BlockSpec((tr, C), lambda i: (i, 0)),
        out_shape=jax.ShapeDtypeStruct((R, C), BF16),
        compiler_params=_cparams(("parallel",)),
        name="cast_bf16",
    )(w_stack)


def _sigmoid(z):
    return 0.5 * jnp.tanh(0.5 * z) + 0.5


def _k_steps(nk, first, middle, last):
    k = pl.program_id(2)
    if nk == 1:
        last(False)
        return
    pl.when(k == 0)(first)
    if nk > 2:
        pl.when((k > 0) & (k < nk - 1))(middle)
    pl.when(k == nk - 1)(functools.partial(last, True))


def _mm_kernel(x_ref, w_ref, o_ref, *acc, nk):
    def prod():
        return jnp.dot(x_ref[...], w_ref[...], preferred_element_type=F32)

    def first():
        acc[0][...] = prod()

    def middle():
        acc[0][...] += prod()

    def last(has_acc):
        o_ref[...] = ((acc[0][...] + prod()) if has_acc else prod()).astype(o_ref.dtype)

    _k_steps(nk, first, middle, last)


def _matmul(x, w, out_dtype, tiles=(1024, 1024, 4096)):
    M, K = x.shape
    N = w.shape[1]
    tm, tn, tk = _tile(M, tiles[0]), _tile(N, tiles[1]), _tile(K, tiles[2])
    nk = K // tk
    return pl.pallas_call(
        functools.partial(_mm_kernel, nk=nk),
        grid=(M // tm, N // tn, nk),
        in_specs=[pl.BlockSpec((tm, tk), lambda i, j, k: (i, k)),
                  pl.BlockSpec((tk, tn), lambda i, j, k: (k, j))],
        out_specs=pl.BlockSpec((tm, tn), lambda i, j, k: (i, j)),
        out_shape=jax.ShapeDtypeStruct((M, N), out_dtype),
        scratch_shapes=[pltpu.VMEM((tm, tn), F32)] if nk > 1 else [],
        compiler_params=_cparams(("parallel", "parallel", "arbitrary")),
        name="matmul",
    )(x, w)


def _ffn_up_kernel(x_ref, wg_ref, wu_ref, o_ref, *acc, nk):
    def prods():
        x = x_ref[...]
        return (jnp.dot(x, wg_ref[...], preferred_element_type=F32),
                jnp.dot(x, wu_ref[...], preferred_element_type=F32))

    def first():
        acc[0][...], acc[1][...] = prods()

    def middle():
        g, u = prods()
        acc[0][...] += g
        acc[1][...] += u

    def last(has_acc):
        g, u = prods()
        if has_acc:
            g, u = acc[0][...] + g, acc[1][...] + u
        o_ref[...] = (g * _sigmoid(g) * u).astype(o_ref.dtype)

    _k_steps(nk, first, middle, last)


def _ffn_up(x, wg, wu, tiles=(1024, 512, 4096)):
    M, K = x.shape
    N = wg.shape[1]
    tm, tn, tk = _tile(M, tiles[0]), _tile(N, tiles[1]), _tile(K, tiles[2])
    nk = K // tk
    return pl.pallas_call(
        functools.partial(_ffn_up_kernel, nk=nk),
        grid=(M // tm, N // tn, nk),
        in_specs=[pl.BlockSpec((tm, tk), lambda i, j, k: (i, k)),
                  pl.BlockSpec((tk, tn), lambda i, j, k: (k, j)),
                  pl.BlockSpec((tk, tn), lambda i, j, k: (k, j))],
        out_specs=pl.BlockSpec((tm, tn), lambda i, j, k: (i, j)),
        out_shape=jax.ShapeDtypeStruct((M, N), BF16),
        scratch_shapes=[pltpu.VMEM((tm, tn), F32), pltpu.VMEM((tm, tn), F32)] if nk > 1 else [],
        compiler_params=_cparams(("parallel", "parallel", "arbitrary")),
        name="ffn_up",
    )(x, wg, wu)


def _add_ln_kernel(x_ref, y_ref, g_ref, b_ref, o_ref, ob_ref, *, cy):
    z = ALPHA * x_ref[...] + cy * y_ref[...]
    mu = jnp.mean(z, axis=-1, keepdims=True)
    zc = z - mu
    var = jnp.mean(zc * zc, axis=-1, keepdims=True)
    out = zc * lax.rsqrt(var + LN_EPS) * g_ref[...] + b_ref[...]
    o_ref[...] = out
    ob_ref[...] = out.astype(BF16)


def _add_ln(x, y, g, b, cy):
    M, D = x.shape
    tm = _tile(M, 256)
    row = pl.BlockSpec((tm, D), lambda i: (i, 0))
    vec = pl.BlockSpec((1, D), lambda i: (0, 0))
    return pl.pallas_call(
        functools.partial(_add_ln_kernel, cy=cy),
        grid=(M // tm,),
        in_specs=[row, row, vec, vec],
        out_specs=[row, row],
        out_shape=[jax.ShapeDtypeStruct((M, D), F32), jax.ShapeDtypeStruct((M, D), BF16)],
        compiler_params=_cparams(("parallel",)),
        name="add_ln",
    )(x, y, g.reshape(1, D), b.reshape(1, D))


def _dilated_kernel(slope_ref, *refs):
    ng = len(A_PAIRS)
    qkv = [refs[3 * g:3 * g + 3] for g in range(ng)]
    out_ref, o_scr, l_scr = refs[3 * ng:]
    h = pl.program_id(1)
    L = out_ref.shape[1]
    n_units = L // BLOCK

    def unit(u, carry):
        for g, (window, dil) in enumerate(A_PAIRS):
            q_ref, k_ref, v_ref = qkv[g]
            steps = window // dil
            n_prev = -(-steps // BLOCK)
            nbg = n_units // dil
            kw = min(n_prev + 1, nbg) * BLOCK
            r, i = u // nbg, u % nbg
            sb = jnp.maximum(i - n_prev, 0)
            rows_q = pl.ds(i * (BLOCK * dil) + r, BLOCK, stride=dil)
            rows_k = pl.ds(sb * (BLOCK * dil) + r, kw, stride=dil)
            q = q_ref[0, rows_q, :].astype(BF16)
            kwin = k_ref[0, rows_k, :].astype(BF16)
            vwin = v_ref[0, rows_k, :].astype(BF16)
            qpos = i * BLOCK + lax.broadcasted_iota(jnp.int32, (BLOCK, kw), 0)
            kpos = sb * BLOCK + lax.broadcasted_iota(jnp.int32, (BLOCK, kw), 1)
            dist = qpos - kpos
            maskadd = jnp.where((dist >= 0) & (dist <= steps), 0.0, NEG)
            s = _dot_nt(q, kwin) * ATTN_SCALE
            s = s - slope_ref[g * A_HEADS_PER_PAIR + h] * (dil * dist).astype(F32) + maskadd
            m = jnp.max(s, axis=-1, keepdims=True)
            e = jnp.exp(s - m)
            den = jnp.maximum(jnp.sum(e, axis=-1, keepdims=True), 1e-30)
            o_scr[g, rows_q, :] = jnp.dot(e.astype(BF16), vwin, preferred_element_type=F32) / den
            l_scr[g, rows_q, :] = jnp.broadcast_to(m + jnp.log(den), (BLOCK, LANES))
        return carry

    lax.fori_loop(0, n_units, unit, 0, unroll=8)

    for c in range(n_units):
        rows = slice(c * BLOCK, (c + 1) * BLOCK)
        ls = [l_scr[g, rows, :] for g in range(ng)]
        m = functools.reduce(jnp.maximum, ls)
        es = [jnp.exp(l - m) for l in ls]
        tot = functools.reduce(lambda a, b: a + b, es)
        out = None
        for g in range(ng):
            term = (es[g] / tot) * o_scr[g, rows, :]
            out = term if out is None else out + term
        out_ref[0, rows, :] = out.astype(out_ref.dtype)


def _dilated_attention(ya, slopes_all):
    B, L, _ = ya.shape
    hpp = A_HEADS_PER_PAIR
    assert all((L // dil) % BLOCK == 0 for _, dil in A_PAIRS)
    specs = [pl.BlockSpec((1, L, HEAD_DIM), lambda b, h, c=part * A_HEADS + g * hpp: (b, 0, c + h))
             for g in range(len(A_PAIRS)) for part in range(3)]
    scratch = pltpu.VMEM((len(A_PAIRS), L, HEAD_DIM), F32)
    out = pl.pallas_call(
        _dilated_kernel,
        grid=(B, hpp),
        in_specs=[pl.BlockSpec(memory_space=pltpu.SMEM)] + specs,
        out_specs=pl.BlockSpec((1, L, HEAD_DIM), lambda b, h: (b, 0, h)),
        out_shape=jax.ShapeDtypeStruct((B, L, hpp * HEAD_DIM), BF16),
        scratch_shapes=[scratch, scratch],
        compiler_params=_cparams(("parallel", "arbitrary")),
        name="dilated_attention",
    )(slopes_all[np.array(A_SLOPE_IDX)], *([ya] * (3 * len(A_PAIRS))))
    return out.reshape(B * L, hpp * HEAD_DIM)


def _logsig_cumsum_kernel(z_ref, bias_ref, o_ref):
    L = z_ref.shape[1]
    row = lax.broadcasted_iota(jnp.int32, (BLOCK, BLOCK), 0)
    col = lax.broadcasted_iota(jnp.int32, (BLOCK, BLOCK), 1)
    tri = jnp.where(row >= col, 1.0, 0.0).astype(F32)
    carry = jnp.zeros((1, LANES), F32)
    for j in range(L // BLOCK):
        z = z_ref[0, j * BLOCK:(j + 1) * BLOCK, :] + bias_ref[...]
        ls = jnp.minimum(z, 0.0) - jnp.log(1.0 + jnp.exp(-jnp.abs(z)))
        c = jnp.dot(tri, ls, preferred_element_type=F32, precision=lax.Precision.HIGHEST) + carry
        o_ref[0, j * BLOCK:(j + 1) * BLOCK, :] = c
        carry = c[BLOCK - 1:BLOCK, :]


def _logsig_cumsum(z, bias):
    B, L, _ = z.shape
    spec = pl.BlockSpec((1, L, LANES), lambda b: (b, 0, 0))
    return pl.pallas_call(
        _logsig_cumsum_kernel,
        grid=(B,),
        in_specs=[spec, pl.BlockSpec((1, LANES), lambda b: (0, 0))],
        out_specs=spec,
        out_shape=jax.ShapeDtypeStruct((B, L, LANES), F32),
        compiler_params=_cparams(("parallel",)),
        name="logsig_cumsum",
    )(z, bias)


def _fox_kernel(q_ref, k_ref, v_ref, cq_ref, ck_ref, o_ref):
    i = pl.program_id(1)
    nb = k_ref.shape[1] // BLOCK

    def body(kw):
        qpos = i * BLOCK + lax.broadcasted_iota(jnp.int32, (BLOCK, kw), 0)
        kpos = lax.broadcasted_iota(jnp.int32, (BLOCK, kw), 1)
        maskadd = jnp.where(qpos >= kpos, 0.0, NEG)
        for h in range(C_HEADS):
            cs = slice(h * HEAD_DIM, (h + 1) * HEAD_DIM)
            q = q_ref[0, :, cs].astype(BF16)
            k = k_ref[0, :kw, cs].astype(BF16)
            v = v_ref[0, :kw, cs].astype(BF16)
            s2 = _dot_nt(q, k) * (ATTN_SCALE * LOG2E) + cq_ref[0, h] * LOG2E - ck_ref[0, h, :, :kw] * LOG2E
            o_ref[0, :, cs] = _softmax2_pv(s2 + maskadd, v).astype(o_ref.dtype)

    _for_causal_class(i, nb, body)


def _forgetting_attention(y3, c_col, c_row):
    B, L, _ = y3.shape
    nb = L // BLOCK
    cw = C_HEADS * HEAD_DIM
    return pl.pallas_call(
        _fox_kernel,
        grid=(B, nb),
        in_specs=[pl.BlockSpec((1, BLOCK, cw), lambda b, i: (b, i, CB_CQ // C_HEADS)),
                  pl.BlockSpec((1, L, cw), lambda b, i: (b, 0, CB_CK // C_HEADS)),
                  pl.BlockSpec((1, L, cw), lambda b, i: (b, 0, CB_CV // C_HEADS)),
                  pl.BlockSpec((1, C_HEADS, BLOCK, 1), lambda b, i: (b, 0, i, 0)),
                  pl.BlockSpec((1, C_HEADS, 1, L), lambda b, i: (b, 0, 0, 0))],
        out_specs=pl.BlockSpec((1, BLOCK, cw), lambda b, i: (b, i, 0)),
        out_shape=jax.ShapeDtypeStruct((B, L, cw), BF16),
        compiler_params=_cparams(("parallel", "arbitrary")),
        name="forgetting_attention",
    )(y3, y3, y3, c_col, c_row)


def _order_key(x):
    bits = lax.bitcast_convert_type(x, jnp.int32)
    return bits ^ ((bits >> 31) & jnp.int32(0x7FFFFFFF))


def _kth_largest_key(key, k):
    rows = key.shape[0]

    def count_ge(t):
        return jnp.sum(jnp.where(key >= t, 1.0, 0.0), axis=-1, keepdims=True)

    t0 = jnp.where(count_ge(jnp.zeros((rows, 1), jnp.int32)) >= k,
                   jnp.int32(0), jnp.int32(-2 ** 31)) + jnp.zeros((rows, 1), jnp.int32)

    def two_bits(it, t):
        lo = jnp.left_shift(jnp.int32(1), 29 - 2 * it)
        c1, c2 = t | lo, t | (lo + lo)
        c3 = c2 | lo
        n1, n2, n3 = count_ge(c1), count_ge(c2), count_ge(c3)
        return jnp.where(n3 >= k, c3, jnp.where(n2 >= k, c2, jnp.where(n1 >= k, c1, t)))

    t = lax.fori_loop(0, 15, two_bits, t0)
    last = t | jnp.int32(1)
    return jnp.where(count_ge(last) >= k, last, t)


def _dsa_kernel(slope_ref, iq_ref, ik_ref, iw_ref, q_ref, k_ref, v_ref, o_ref, *, n_top):
    i = pl.program_id(1)
    nb = k_ref.shape[1] // BLOCK
    iw = iw_ref[0]

    def body(kw):
        ik = ik_ref[0, :kw, :].astype(BF16)
        score = jnp.zeros((BLOCK, kw), F32)
        hs = D_HEAD_STACK
        for h0 in range(0, D_IDX_HEADS, hs):
            iq_rows = jnp.concatenate([iq_ref[0, :, (h0 + r) * D_IDX_DIM:(h0 + r + 1) * D_IDX_DIM].astype(BF16)
                                       for r in range(hs)], axis=0)
            z = _dot_nt(iq_rows, ik)
            for r in range(hs):
                rel = jnp.maximum(z[r * BLOCK:(r + 1) * BLOCK], 0.0)
                score = score + iw[:, h0 + r:h0 + r + 1] * rel
        qpos = i * BLOCK + lax.broadcasted_iota(jnp.int32, (BLOCK, kw), 0)
        kpos = lax.broadcasted_iota(jnp.int32, (BLOCK, kw), 1)
        dist = qpos - kpos
        causal = dist >= 0
        key = _order_key(jnp.where(causal, score, NEG))

        thr = _kth_largest_key(key, float(n_top))
        gt = key > thr
        eq = key == thr
        need = float(n_top) - jnp.sum(jnp.where(gt, 1.0, 0.0), axis=-1, keepdims=True)
        row = lax.broadcasted_iota(jnp.int32, (BLOCK, BLOCK), 0)
        col = lax.broadcasted_iota(jnp.int32, (BLOCK, BLOCK), 1)
        upper = jnp.where(row <= col, 1.0, 0.0).astype(BF16)
        eqf = jnp.where(eq, 1.0, 0.0).astype(BF16)
        carry = jnp.zeros((BLOCK, 1), F32)
        rank_tiles = []
        for j in range(kw // BLOCK):
            rank = carry + jnp.dot(eqf[:, j * BLOCK:(j + 1) * BLOCK], upper, preferred_element_type=F32)
            rank_tiles.append(rank)
            carry = rank[:, BLOCK - 1:BLOCK]
        rank = jnp.concatenate(rank_tiles, axis=1)
        mask = (gt | (eq & (rank <= need))) & causal
        maskadd = jnp.where(mask, 0.0, NEG)

        k = k_ref[0, :kw, :].astype(BF16)
        v = v_ref[0, :kw, :].astype(BF16)
        distf = dist.astype(F32)
        hs = D_HEAD_STACK if kw <= D_STACK_MAX_KEYS else D_HEAD_STACK // 2
        for h0 in range(0, D_HEADS, hs):
            q_rows = jnp.concatenate([q_ref[0, :, (h0 + r) * HEAD_DIM:(h0 + r + 1) * HEAD_DIM].astype(BF16)
                                      for r in range(hs)], axis=0)
            outs = _stacked_softmax2_pv(
                q_rows, k, v,
                lambda r, qk, h0=h0: (qk * (ATTN_SCALE * LOG2E) - (slope_ref[h0 + r] * LOG2E) * distf + maskadd), hs)
            for r in range(hs):
                o_ref[0, :, (h0 + r) * HEAD_DIM:(h0 + r + 1) * HEAD_DIM] = outs[r].astype(o_ref.dtype)

    _for_causal_class(i, nb, body)


def _indexed_sparse_attention(y3, ik, iw, slopes_all):
    B, L, _ = y3.shape
    nb = L // BLOCK
    n_top = min(D_TOPK, L // 4)
    slopes = slopes_all[np.array(D_SLOPE_IDX)]
    qw, iqw = D_HEADS * HEAD_DIM, D_IDX_HEADS * D_IDX_DIM
    return pl.pallas_call(
        functools.partial(_dsa_kernel, n_top=n_top),
        grid=(B, nb),
        in_specs=[pl.BlockSpec(memory_space=pltpu.SMEM),
                  pl.BlockSpec((1, BLOCK, iqw), lambda b, i: (b, i, CB_DIQ * LANES // iqw)),
                  pl.BlockSpec((1, L, D_IDX_DIM), lambda b, i: (b, 0, 0)),
                  pl.BlockSpec((1, BLOCK, D_IDX_HEADS), lambda b, i: (b, i, 0)),
                  pl.BlockSpec((1, BLOCK, qw), lambda b, i: (b, i, CB_DQ * LANES // qw)),
                  pl.BlockSpec((1, L, HEAD_DIM), lambda b, i: (b, 0, CB_DK)),
                  pl.BlockSpec((1, L, HEAD_DIM), lambda b, i: (b, 0, CB_DV))],
        out_specs=pl.BlockSpec((1, BLOCK, qw), lambda b, i: (b, i, 0)),
        out_shape=jax.ShapeDtypeStruct((B, L, qw), BF16),
        compiler_params=_cparams(("parallel", "arbitrary")),
        name="indexed_sparse_attention",
    )(slopes, y3, ik, iw, y3, y3, y3)


def _compress_kernel(x_ref, pos_ref, w1_ref, w2_ref, o_ref):
    x = x_ref[0, 0].astype(F32)
    nxt = pltpu.roll(x, x.shape[0] - 1, 0)
    blk = jnp.concatenate([x, nxt], axis=1) + pos_ref[0]
    h = jnp.dot(blk.astype(BF16), w1_ref[0], preferred_element_type=F32)
    h = jax.nn.gelu(h, approximate=True)
    o_ref[0, 0] = jnp.dot(h.astype(BF16), w2_ref[0], preferred_element_type=F32).astype(o_ref.dtype)


def _compress(xc, pos, w1, w2):
    B, _, n_chunk, cw = xc.shape
    G = B_KV_GROUPS
    return pl.pallas_call(
        _compress_kernel,
        grid=(2 * G, B),
        in_specs=[pl.BlockSpec((1, 1, n_chunk, cw), lambda a, b: (b, a, 0, 0)),
                  pl.BlockSpec((1, 1, 2 * cw), lambda a, b: (a // G, 0, 0)),
                  pl.BlockSpec((1, 2 * cw, B_CMP_HIDDEN), lambda a, b: (a // G, 0, 0)),
                  pl.BlockSpec((1, B_CMP_HIDDEN, HEAD_DIM), lambda a, b: (a // G, 0, 0))],
        out_specs=pl.BlockSpec((1, 1, n_chunk, HEAD_DIM), lambda a, b: (b, a, 0, 0)),
        out_shape=jax.ShapeDtypeStruct((B, 2 * G, n_chunk, HEAD_DIM), BF16),
        compiler_params=_cparams(("arbitrary", "arbitrary")),
        name="nsa_compress",
    )(xc, pos, w1, w2)


def _nsa_kernel(slope_ref, *refs, n_cmp, n_slc, window, n_prev):
    G, R = B_KV_GROUPS, B_GROUP_SIZE
    q_refs, ks_refs, vs_refs, kw_refs, vw_refs = (refs[n * G:(n + 1) * G] for n in range(5))
    ckv_ref, gl_ref, o_ref, sel_ref, cmp_ref, win_ref = refs[5 * G:]
    i = pl.program_id(1)
    nb = ks_refs[0].shape[1] // BLOCK
    gate = _sigmoid(gl_ref[0])
    hd = lambda h: slice(h * HEAD_DIM, (h + 1) * HEAD_DIM)

    kwb = min(n_prev + 1, nb)
    wlen = kwb * BLOCK
    start = pl.multiple_of(jnp.maximum(i - n_prev, 0) * BLOCK, BLOCK)
    wq = i * BLOCK + lax.broadcasted_iota(jnp.int32, (BLOCK, wlen), 0)
    wk = start + lax.broadcasted_iota(jnp.int32, (BLOCK, wlen), 1)
    wdist = wq - wk
    wmask = jnp.where((wdist >= 0) & (wdist <= window), 0.0, NEG)
    wdistf = wdist.astype(F32)
    rows = lambda r: slice(r * BLOCK, (r + 1) * BLOCK)
    q_stack = [jnp.concatenate([q_refs[g][0, :, hd(r)].astype(BF16) for r in range(R)], axis=0) for g in range(G)]
    for g in range(G):
        kwin = kw_refs[g][0, pl.ds(start, wlen), :].astype(BF16)
        vwin = vw_refs[g][0, pl.ds(start, wlen), :].astype(BF16)
        outs = _stacked_softmax2_pv(
            q_stack[g], kwin, vwin,
            lambda r, qk, g=g: qk * (ATTN_SCALE * LOG2E) - (slope_ref[g * R + r] * LOG2E) * wdistf + wmask, R)
        for r in range(R):
            h = g * R + r
            win_ref[:, hd(h)] = gate[:, 2 * B_HEADS + h:2 * B_HEADS + h + 1] * outs[r]

    nc = ckv_ref.shape[2]
    t = i * BLOCK + lax.broadcasted_iota(jnp.int32, (BLOCK, nc), 0)
    n = lax.broadcasted_iota(jnp.int32, (BLOCK, nc), 1)
    dist_c = t - (n * B_CMP_STRIDE + B_CMP_LEN - 1)
    mask = (dist_c >= 0) & (n < n_cmp)
    distf = dist_c.astype(F32)

    nn = lax.broadcasted_iota(jnp.int32, (nc, LANES), 0)
    jj = lax.broadcasted_iota(jnp.int32, (nc, LANES), 1)
    off = nn - B_SEL_RATIO * jj + 1
    w = jnp.where((off == 0) | (off == B_SEL_RATIO), 1.0, jnp.where((off > 0) & (off < B_SEL_RATIO), 2.0, 0.0))
    w = jnp.where((nn < n_cmp) & (jj < n_slc), w, 0.0).astype(F32)
    tq = i * BLOCK + lax.broadcasted_iota(jnp.int32, (BLOCK, LANES), 0)
    j = lax.broadcasted_iota(jnp.int32, (BLOCK, LANES), 1)
    cur = tq >> int(np.log2(B_SEL_LEN))
    forced = (j == 0) | (j == cur) | (j == cur - 1)
    scores = []
    for g in range(G):
        kc = ckv_ref[0, g]
        vc = ckv_ref[0, G + g]
        imp = jnp.zeros((BLOCK, nc), F32)
        qk = _dot_nt(q_stack[g], kc)
        ps = []
        for r in range(R):
            s = qk[rows(r)] * ATTN_SCALE
            s = jnp.where(mask, s - slope_ref[g * R + r] * distf, NEG)
            m = jnp.max(s, axis=-1, keepdims=True)
            e = jnp.where(mask, jnp.exp(s - m), 0.0)
            p = e / jnp.maximum(jnp.sum(e, axis=-1, keepdims=True), 1e-30)
            ps.append(p.astype(BF16))
            imp = imp + p
        pv = jnp.dot(jnp.concatenate(ps, axis=0), vc, preferred_element_type=F32)
        for r in range(R):
            h = g * R + r
            cmp_ref[:, hd(h)] = gate[:, h:h + 1] * pv[rows(r)]
        p_slc = jnp.dot(imp, w, preferred_element_type=F32, precision=lax.Precision.HIGHEST)
        score = jnp.where(forced, 1e9, jnp.where(j <= cur, p_slc, -1e9))
        scores.append(jnp.where(j < n_slc, score, -3e38))

    score = jnp.concatenate(scores, axis=0)
    forced2 = jnp.concatenate([forced] * G, axis=0)
    jf = lax.broadcasted_iota(jnp.int32, score.shape, 1).astype(F32)
    sel = jnp.where(forced2, 1.0, 0.0)
    score = jnp.where(forced2, -3e38, score)
    n_forced = 3
    for _ in range(min(B_N_SEL, n_slc) - n_forced):
        m = jnp.max(score, axis=-1, keepdims=True)
        first = jnp.min(jnp.where(score == m, jf, float(LANES)), axis=-1, keepdims=True)
        hit = jf == first
        sel = jnp.where(hit, 1.0, sel)
        score = jnp.where(hit, -3e38, score)
    for g in range(G):
        sel_ref[g] = sel[g * BLOCK:(g + 1) * BLOCK].astype(sel_ref.dtype)

    def body(kw):
        jj = lax.broadcasted_iota(jnp.int32, (LANES, kw), 0)
        ss = lax.broadcasted_iota(jnp.int32, (LANES, kw), 1)
        expand = jnp.where((ss >> int(np.log2(B_SEL_LEN))) == jj, 1.0, 0.0).astype(BF16)
        qpos = i * BLOCK + lax.broadcasted_iota(jnp.int32, (BLOCK, kw), 0)
        kpos = lax.broadcasted_iota(jnp.int32, (BLOCK, kw), 1)
        dist = qpos - kpos
        distf = dist.astype(F32)
        for g in range(G):
            picked = jnp.dot(sel_ref[g], expand, preferred_element_type=F32) > 0.5
            maskadd = jnp.where(picked & (dist >= 0), 0.0, NEG)
            k = ks_refs[g][0, :kw, :].astype(BF16)
            v = vs_refs[g][0, :kw, :].astype(BF16)
            hs = R if kw <= D_STACK_MAX_KEYS else R // 2
            for r0 in range(0, R, hs):
                q_rows = jnp.concatenate([q_refs[g][0, :, hd(r0 + r)].astype(BF16) for r in range(hs)], axis=0)
                outs = _stacked_softmax2_pv(
                    q_rows, k, v,
                    lambda r, qk, h0=g * R + r0, maskadd=maskadd: (
                        qk * (ATTN_SCALE * LOG2E) - (slope_ref[h0 + r] * LOG2E) * distf + maskadd), hs)
                for r in range(hs):
                    h = g * R + r0 + r
                    o_slc = gate[:, B_HEADS + h:B_HEADS + h + 1] * outs[r]
                    o_ref[0, :, hd(h)] = (cmp_ref[:, hd(h)] + o_slc + win_ref[:, hd(h)]).astype(o_ref.dtype)

    _for_causal_class(i, nb, body)


def _nsa_attention(y3, cmp_kv, gate_logits, slopes, n_cmp, n_slc):
    B, L, _ = y3.shape
    G, R = B_KV_GROUPS, B_GROUP_SIZE
    nc = cmp_kv.shape[2]
    w_steps = B_WINDOW - 1
    q_specs = [pl.BlockSpec((1, BLOCK, R * HEAD_DIM), lambda b, i, g=g: (b, i, CB_BQ // R + g)) for g in range(G)]
    kv_specs = [pl.BlockSpec((1, L, HEAD_DIM), lambda b, i, c=CB_BKV + (br * 2 + kv) * G + g: (b, 0, c))
                for br in (1, 2) for kv in range(2) for g in range(G)]
    hw = B_HEADS * HEAD_DIM
    return pl.pallas_call(
        functools.partial(_nsa_kernel, n_cmp=n_cmp, n_slc=n_slc, window=w_steps, n_prev=-(-w_steps // BLOCK)),
        grid=(B, L // BLOCK),
        in_specs=[pl.BlockSpec(memory_space=pltpu.SMEM)] + q_specs + kv_specs +
                 [pl.BlockSpec((1, 2 * G, nc, HEAD_DIM), lambda b, i: (b, 0, 0, 0)),
                  pl.BlockSpec((1, BLOCK, gate_logits.shape[2]), lambda b, i: (b, i, 0))],
        out_specs=pl.BlockSpec((1, BLOCK, hw), lambda b, i: (b, i, 0)),
        out_shape=jax.ShapeDtypeStruct((B, L, hw), BF16),
        scratch_shapes=[pltpu.VMEM((G, BLOCK, LANES), BF16), pltpu.VMEM((BLOCK, hw), F32),
                        pltpu.VMEM((BLOCK, hw), F32)],
        compiler_params=_cparams(("parallel", "arbitrary")),
        name="nsa_attention",
    )(slopes, *([y3] * (5 * G)), cmp_kv, gate_logits)


def _native_sparse_attention(y3, gate_logits, cmp_w1, cmp_w2, cmp_pos, slopes_all):
    B, L, _ = y3.shape
    G, R = B_KV_GROUPS, B_GROUP_SIZE
    slopes = slopes_all[np.array(B_SLOPE_IDX)]
    n_chunk = L // B_CMP_STRIDE
    n_cmp = n_chunk - B_CMP_LEN // B_CMP_STRIDE + 1
    n_slc = L // B_SEL_LEN
    assert B_CMP_LEN == 2 * B_CMP_STRIDE and n_chunk % 8 == 0 and n_slc <= LANES

    xc = y3[:, :, CB_BKV * LANES:(CB_BKV + 2 * G) * LANES].reshape(B, L, 2 * G, HEAD_DIM)
    xc = xc.transpose(0, 2, 1, 3).reshape(B, 2 * G, n_chunk, B_CMP_STRIDE * HEAD_DIM)
    cmp_kv = _compress(xc, cmp_pos.reshape(2, 1, B_CMP_LEN * HEAD_DIM), cmp_w1.astype(BF16), cmp_w2.astype(BF16))

    o = _nsa_attention(y3, cmp_kv, gate_logits.reshape(B, L, -1), slopes, n_cmp, n_slc)
    return o.reshape(B * L, B_HEADS * HEAD_DIM)


def _gated_merge_kernel(x_ref, *refs, nk):
    nbr = N_BRANCH
    wg, o, w, bias = refs[:nbr], refs[nbr:2 * nbr], refs[2 * nbr:3 * nbr], refs[3 * nbr:4 * nbr]
    out_ref, acc = refs[4 * nbr], refs[4 * nbr + 1:]

    def logits(c):
        return jnp.dot(x_ref[...], wg[c][...], preferred_element_type=F32)

    def first():
        for c in range(nbr):
            acc[0][c] = logits(c)

    def middle():
        for c in range(nbr):
            acc[0][c] += logits(c)

    def last(has_acc):
        merged = None
        for c in range(nbr):
            z = (acc[0][c] + logits(c)) if has_acc else logits(c)
            term = _sigmoid(z + bias[c][...]) * jnp.dot(o[c][...], w[c][...], preferred_element_type=F32)
            merged = term if merged is None else merged + term
        out_ref[...] = merged.astype(out_ref.dtype)

    _k_steps(nk, first, middle, last)


def _gated_merge(xb, w_gate, b_gate, branch_outs, branch_ws):
    M, K = xb.shape
    D = branch_ws[0].shape[1]
    tm, tn, tk = _tile(M, 1024), _tile(D, 256), _tile(K, 4096)
    nj, nk = D // tn, K // tk
    wg_specs = [pl.BlockSpec((tk, tn), lambda i, j, k, c=c: (k, c * nj + j)) for c in range(N_BRANCH)]
    o_specs = [pl.BlockSpec((tm, o.shape[1]), lambda i, j, k: (i, 0), pipeline_mode=pl.Buffered(1))
               for o in branch_outs]
    w_specs = [pl.BlockSpec((w.shape[0], tn), lambda i, j, k: (0, j)) for w in branch_ws]
    b_specs = [pl.BlockSpec((1, tn), lambda i, j, k, c=c: (0, c * nj + j)) for c in range(N_BRANCH)]
    return pl.pallas_call(
        functools.partial(_gated_merge_kernel, nk=nk),
        grid=(M // tm, nj, nk),
        in_specs=[pl.BlockSpec((tm, tk), lambda i, j, k: (i, k))] + wg_specs + o_specs + w_specs + b_specs,
        out_specs=pl.BlockSpec((tm, tn), lambda i, j, k: (i, j)),
        out_shape=jax.ShapeDtypeStruct((M, D), BF16),
        scratch_shapes=[pltpu.VMEM((N_BRANCH, tm, tn), F32)] if nk > 1 else [],
        compiler_params=_cparams(("parallel", "parallel", "arbitrary")),
        name="gated_merge",
    )(xb, *([w_gate] * N_BRANCH), *branch_outs, *branch_ws, *([b_gate.reshape(1, -1)] * N_BRANCH))


def _attn_weight(w_in):
    D = w_in.shape[0]
    sizes = (A_HEADS * HEAD_DIM,) * 3 + (B_HEADS * HEAD_DIM, 3 * 2 * B_KV_GROUPS * HEAD_DIM, 3 * B_HEADS) + \
            (C_HEADS * HEAD_DIM,) * 3 + (C_HEADS,) + (D_HEADS * HEAD_DIM, HEAD_DIM, HEAD_DIM) + \
            (D_IDX_HEADS * D_IDX_DIM, D_IDX_DIM, D_IDX_HEADS)
    offs = np.concatenate([[0], np.cumsum(sizes)])
    (a_q, a_k, a_v, b_q, b_kv, b_g, c_q, c_k, c_v, c_f, d_q, d_k, d_v, d_iq, d_ik, d_iw) = [
        w_in[:, offs[n]:offs[n + 1]] for n in range(len(sizes))]
    misc = jnp.zeros((D, 2 * LANES), w_in.dtype)
    misc = misc.at[:, MISC_BG:MISC_BG + 3 * B_HEADS].set(b_g)
    misc = misc.at[:, MISC_CF:MISC_CF + C_HEADS].set(c_f)
    misc = misc.at[:, MISC_IK:MISC_IK + D_IDX_DIM].set(d_ik)
    misc = misc.at[:, MISC_IW:MISC_IW + D_IDX_HEADS].set(d_iw)
    w_a = jnp.concatenate([p.astype(BF16) for p in (a_q, a_k, a_v)], axis=1)
    parts = [b_q, c_q, c_k, c_v, d_q, b_kv, d_iq, d_k, d_v, misc]
    w = jnp.concatenate([p.astype(BF16) for p in parts], axis=1)
    assert w_a.shape[1] == N_CB_A * LANES and w.shape[1] == N_CB * LANES
    return w_a, w, int(offs[-1])


def _hybrid_mixer(xb, B, L, w_in, b_forget, b_gate, cmp_w1, cmp_w2, cmp_pos, w_branch, w_out):
    M, D = xb.shape
    slopes_all = jnp.exp2(-8.0 * jnp.arange(1, N_ALIBI + 1, dtype=F32) / N_ALIBI)
    w_a, w_attn, gate_off = _attn_weight(w_in)
    ya = _matmul(xb, w_a, F32, (1024, 768, 4096)).reshape(B, L, N_CB_A * LANES)
    y3 = _matmul(xb, w_attn, BF16, (1024, 1280, 4096)).reshape(B, L, N_CB * LANES)
    misc = _matmul(xb, w_attn[:, CB_MISC * LANES:(CB_MISC + 1) * LANES], F32)

    bias = jnp.zeros((1, LANES), F32).at[0, MISC_CF:MISC_CF + C_HEADS].set(b_forget)
    c = _logsig_cumsum(misc.reshape(B, L, LANES), bias)[:, :, MISC_CF:MISC_CF + C_HEADS].transpose(0, 2, 1)

    o_a = _dilated_attention(ya, slopes_all)
    o_b = _native_sparse_attention(y3, misc[:, MISC_BG:MISC_BG + 3 * B_HEADS], cmp_w1, cmp_w2, cmp_pos, slopes_all)
    o_c = _forgetting_attention(y3, c[..., None], c[:, :, None, :])
    o_d = _indexed_sparse_attention(y3, misc[:, MISC_IK:MISC_IK + D_IDX_DIM].reshape(B, L, D_IDX_DIM),
                                    misc[:, MISC_IW:MISC_IW + D_IDX_HEADS].reshape(B, L, D_IDX_HEADS), slopes_all)

    sizes = (A_HEADS_PER_PAIR * HEAD_DIM, B_HEADS * HEAD_DIM, C_HEADS * HEAD_DIM, D_HEADS * HEAD_DIM)
    offs = np.concatenate([[0], np.cumsum(sizes)])
    ws = [w_branch[offs[n]:offs[n + 1]] for n in range(N_BRANCH)]
    outs = [o_a, o_b, o_c.reshape(M, -1), o_d.reshape(M, -1)]
    merged = _gated_merge(xb, w_in[:, gate_off:].astype(BF16), b_gate, outs, ws)
    return _matmul(merged, w_out, F32)


def kernel(x, ln_g, ln_b, ffn1_w_gate, ffn1_w_up, ffn1_w_down, w_in, b_forget, b_gate, cmp_w1, cmp_w2, cmp_pos,
           w_branch, w_out, ffn2_w_gate, ffn2_w_up, ffn2_w_down):
    B, L, D = x.shape
    assert L % BLOCK == 0 and D % LANES == 0
    xf = x.reshape(B * L, D)
    xb = xf.astype(BF16)

    def ffn(xf, xb, wg, wu, wd, l, g, b):
        h = _ffn_up(xb, _cast_bf16(wg, l), _cast_bf16(wu, l))
        y = _matmul(h, _cast_bf16(wd, l), F32)
        return _add_ln(xf, y, g, b, 0.5)

    for l in range(ln_g.shape[0]):
        xf, xb = ffn(xf, xb, ffn1_w_gate, ffn1_w_up, ffn1_w_down, l, ln_g[l, 0], ln_b[l, 0])
        y = _hybrid_mixer(xb, B, L, w_in[l], b_forget[l], b_gate[l], cmp_w1[l], cmp_w2[l], cmp_pos[l],
                          _cast_bf16(w_branch, l), _cast_bf16(w_out, l))
        xf, xb = _add_ln(xf, y, ln_g[l, 1], ln_b[l, 1], 1.0)
        xf, xb = ffn(xf, xb, ffn2_w_gate, ffn2_w_up, ffn2_w_down, l, ln_g[l, 2], ln_b[l, 2])
    return xf.reshape(B, L, D)
```

```python
import functools

import numpy as np
import jax
import jax.numpy as jnp
from jax import lax
from jax.experimental import pallas as pl
from jax.experimental.pallas import tpu as pltpu

F32 = jnp.float32
BF16 = jnp.bfloat16

HEAD_DIM = 128
BLOCK = 128
LANES = 128
NEG = -1e30
ATTN_SCALE = HEAD_DIM ** -0.5
LOG2E = 1.4426950408889634
DEPTH = 2
ALPHA = (2 * DEPTH) ** 0.25
LN_EPS = 1e-5

A_PAIRS = ((128, 1), (512, 4), (2048, 16))
A_HEADS_PER_PAIR = 4
A_HEADS = A_HEADS_PER_PAIR * len(A_PAIRS)
B_HEADS = 8
B_KV_GROUPS = 2
B_GROUP_SIZE = B_HEADS // B_KV_GROUPS
B_CMP_LEN = 32
B_CMP_STRIDE = 16
B_SEL_LEN = 64
B_SEL_RATIO = B_SEL_LEN // B_CMP_STRIDE
B_N_SEL = 8
B_WINDOW = 512
B_CMP_HIDDEN = 512
C_HEADS = 8
D_HEADS = 8
D_IDX_HEADS = 8
D_IDX_DIM = 64
D_TOPK = 256
D_HEAD_STACK = 4
D_STACK_MAX_KEYS = 768
N_BRANCH = 4
N_ALIBI = A_HEADS + B_HEADS + D_HEADS
A_SLOPE_IDX = (0, 1, 2, 3, 12, 13, 14, 15, 24, 25, 26, 27)
B_SLOPE_IDX = (4, 5, 6, 7, 8, 9, 10, 11)
D_SLOPE_IDX = (16, 17, 18, 19, 20, 21, 22, 23)

N_CB_A = 3 * A_HEADS
CB_BQ = 0
CB_CQ, CB_CK, CB_CV = 8, 16, 24
CB_DQ = 32
CB_BKV = 40
CB_DIQ = 52
CB_DK, CB_DV = 56, 57
CB_MISC = 58
N_CB = 60
MISC_BG, MISC_CF, MISC_IK, MISC_IW = 0, 24, 32, 96

CAUSAL_CLASSES = 8
VMEM_LIMIT = 56 * 1024 * 1024


def _cparams(sem):
    return pltpu.CompilerParams(dimension_semantics=sem, vmem_limit_bytes=VMEM_LIMIT)


def _tile(dim, pref):
    return pref if dim % pref == 0 else dim


def _dot_nt(a, b):
    return lax.dot_general(a, b, (((1,), (1,)), ((), ())), preferred_element_type=F32)


def _softmax2_pv(s2, v):
    m = jnp.max(s2, axis=-1, keepdims=True)
    e = jnp.exp2(s2 - m)
    den = jnp.maximum(jnp.sum(e, axis=-1, keepdims=True), 1e-30)
    return jnp.dot(e.astype(BF16), v, preferred_element_type=F32) / den


def _stacked_softmax2_pv(q_stack, k, v, logits2, n):
    qk = _dot_nt(q_stack, k)
    es, dens = [], []
    for r in range(n):
        s2 = logits2(r, qk[r * BLOCK:(r + 1) * BLOCK])
        e = jnp.exp2(s2 - jnp.max(s2, axis=-1, keepdims=True))
        dens.append(jnp.maximum(jnp.sum(e, axis=-1, keepdims=True), 1e-30))
        es.append(e.astype(BF16))
    pv = jnp.dot(jnp.concatenate(es, axis=0), v, preferred_element_type=F32)
    return [pv[r * BLOCK:(r + 1) * BLOCK] / dens[r] for r in range(n)]


def _for_causal_class(i, nb, body):
    n_cls = CAUSAL_CLASSES if nb % CAUSAL_CLASSES == 0 else 1
    per = nb // n_cls
    for c in range(n_cls):
        pl.when((i >= c * per) & (i < (c + 1) * per))(functools.partial(body, (c + 1) * per * BLOCK))


def _cast_kernel(x_ref, o_ref):
    o_ref[...] = x_ref[0].astype(o_ref.dtype)


def _cast_bf16(w_stack, l):
    _, R, C = w_stack.shape
    tr = R
    for cand in (1024, 512, 256, 128, 64, 32, 16):
        if R % cand == 0:
            tr = cand
            if cand * C * 4 <= 8 * 1024 * 1024:
                break
    return pl.pallas_call(
        _cast_kernel,
        grid=(R // tr,),
        in_specs=[pl.BlockSpec((1, tr, C), lambda i: (l, i, 0))],
        out_specs=pl.BlockSpec((tr, C), lambda i: (i, 0)),
        out_shape=jax.ShapeDtypeStruct((R, C), BF16),
        compiler_params=_cparams(("parallel",)),
        name="cast_bf16",
    )(w_stack)


def _sigmoid(z):
    return 0.5 * jnp.tanh(0.5 * z) + 0.5


def _k_steps(nk, first, middle, last):
    k = pl.program_id(2)
    if nk == 1:
        last(False)
        return
    pl.when(k == 0)(first)
    if nk > 2:
        pl.when((k > 0) & (k < nk - 1))(middle)
    pl.when(k == nk - 1)(functools.partial(last, True))


def _mm_kernel(x_ref, w_ref, o_ref, *acc, nk):
    def prod():
        return jnp.dot(x_ref[...], w_ref[...], preferred_element_type=F32)

    def first():
        acc[0][...] = prod()

    def middle():
        acc[0][...] += prod()

    def last(has_acc):
        o_ref[...] = ((acc[0][...] + prod()) if has_acc else prod()).astype(o_ref.dtype)

    _k_steps(nk, first, middle, last)


def _matmul(x, w, out_dtype, tiles=(1024, 1024, 4096)):
    M, K = x.shape
    N = w.shape[1]
    tm, tn, tk = _tile(M, tiles[0]), _tile(N, tiles[1]), _tile(K, tiles[2])
    nk = K // tk
    return pl.pallas_call(
        functools.partial(_mm_kernel, nk=nk),
        grid=(M // tm, N // tn, nk),
        in_specs=[pl.BlockSpec((tm, tk), lambda i, j, k: (i, k)),
                  pl.BlockSpec((tk, tn), lambda i, j, k: (k, j))],
        out_specs=pl.BlockSpec((tm, tn), lambda i, j, k: (i, j)),
        out_shape=jax.ShapeDtypeStruct((M, N), out_dtype),
        scratch_shapes=[pltpu.VMEM((tm, tn), F32)] if nk > 1 else [],
        compiler_params=_cparams(("parallel", "parallel", "arbitrary")),
        name="matmul",
    )(x, w)


def _ffn_up_kernel(x_ref, wg_ref, wu_ref, o_ref, *acc, nk):
    def prods():
        x = x_ref[...]
        return (jnp.dot(x, wg_ref[...], preferred_element_type=F32),
                jnp.dot(x, wu_ref[...], preferred_element_type=F32))

    def first():
        acc[0][...], acc[1][...] = prods()

    def middle():
        g, u = prods()
        acc[0][...] += g
        acc[1][...] += u

    def last(has_acc):
        g, u = prods()
        if has_acc:
            g, u = acc[0][...] + g, acc[1][...] + u
        o_ref[...] = (g * _sigmoid(g) * u).astype(o_ref.dtype)

    _k_steps(nk, first, middle, last)


def _ffn_up(x, wg, wu, tiles=(1024, 512, 4096)):
    M, K = x.shape
    N = wg.shape[1]
    tm, tn, tk = _tile(M, tiles[0]), _tile(N, tiles[1]), _tile(K, tiles[2])
    nk = K // tk
    return pl.pallas_call(
        functools.partial(_ffn_up_kernel, nk=nk),
        grid=(M // tm, N // tn, nk),
        in_specs=[pl.BlockSpec((tm, tk), lambda i, j, k: (i, k)),
                  pl.BlockSpec((tk, tn), lambda i, j, k: (k, j)),
                  pl.BlockSpec((tk, tn), lambda i, j, k: (k, j))],
        out_specs=pl.BlockSpec((tm, tn), lambda i, j, k: (i, j)),
        out_shape=jax.ShapeDtypeStruct((M, N), BF16),
        scratch_shapes=[pltpu.VMEM((tm, tn), F32), pltpu.VMEM((tm, tn), F32)] if nk > 1 else [],
        compiler_params=_cparams(("parallel", "parallel", "arbitrary")),
        name="ffn_up",
    )(x, wg, wu)


def _add_ln_kernel(x_ref, y_ref, g_ref, b_ref, o_ref, ob_ref, *, cy):
    z = ALPHA * x_ref[...] + cy * y_ref[...]
    mu = jnp.mean(z, axis=-1, keepdims=True)
    zc = z - mu
    var = jnp.mean(zc * zc, axis=-1, keepdims=True)
    out = zc * lax.rsqrt(var + LN_EPS) * g_ref[...] + b_ref[...]
    o_ref[...] = out
    ob_ref[...] = out.astype(BF16)


def _add_ln(x, y, g, b, cy):
    M, D = x.shape
    tm = _tile(M, 256)
    row = pl.BlockSpec((tm, D), lambda i: (i, 0))
    vec = pl.BlockSpec((1, D), lambda i: (0, 0))
    return pl.pallas_call(
        functools.partial(_add_ln_kernel, cy=cy),
        grid=(M // tm,),
        in_specs=[row, row, vec, vec],
        out_specs=[row, row],
        out_shape=[jax.ShapeDtypeStruct((M, D), F32), jax.ShapeDtypeStruct((M, D), BF16)],
        compiler_params=_cparams(("parallel",)),
        name="add_ln",
    )(x, y, g.reshape(1, D), b.reshape(1, D))


def _dilated_kernel(slope_ref, *refs):
    ng = len(A_PAIRS)
    qkv = [refs[3 * g:3 * g + 3] for g in range(ng)]
    out_ref, o_scr, l_scr = refs[3 * ng:]
    h = pl.program_id(1)
    L = out_ref.shape[1]
    n_units = L // BLOCK

    def unit(u, carry):
        for g, (window, dil) in enumerate(A_PAIRS):
            q_ref, k_ref, v_ref = qkv[g]
            steps = window // dil
            n_prev = -(-steps // BLOCK)
            nbg = n_units // dil
            kw = min(n_prev + 1, nbg) * BLOCK
            r, i = u // nbg, u % nbg
            sb = jnp.maximum(i - n_prev, 0)
            rows_q = pl.ds(i * (BLOCK * dil) + r, BLOCK, stride=dil)
            rows_k = pl.ds(sb * (BLOCK * dil) + r, kw, stride=dil)
            q = q_ref[0, rows_q, :].astype(BF16)
            kwin = k_ref[0, rows_k, :].astype(BF16)
            vwin = v_ref[0, rows_k, :].astype(BF16)
            qpos = i * BLOCK + lax.broadcasted_iota(jnp.int32, (BLOCK, kw), 0)
            kpos = sb * BLOCK + lax.broadcasted_iota(jnp.int32, (BLOCK, kw), 1)
            dist = qpos - kpos
            maskadd = jnp.where((dist >= 0) & (dist <= steps), 0.0, NEG)
            s = _dot_nt(q, kwin) * ATTN_SCALE
            s = s - slope_ref[g * A_HEADS_PER_PAIR + h] * (dil * dist).astype(F32) + maskadd
            m = jnp.max(s, axis=-1, keepdims=True)
            e = jnp.exp(s - m)
            den = jnp.maximum(jnp.sum(e, axis=-1, keepdims=True), 1e-30)
            o_scr[g, rows_q, :] = jnp.dot(e.astype(BF16), vwin, preferred_element_type=F32) / den
            l_scr[g, rows_q, :] = jnp.broadcast_to(m + jnp.log(den), (BLOCK, LANES))
        return carry

    lax.fori_loop(0, n_units, unit, 0, unroll=8)

    for c in range(n_units):
        rows = slice(c * BLOCK, (c + 1) * BLOCK)
        ls = [l_scr[g, rows, :] for g in range(ng)]
        m = functools.reduce(jnp.maximum, ls)
        es = [jnp.exp(l - m) for l in ls]
        tot = functools.reduce(lambda a, b: a + b, es)
        out = None
        for g in range(ng):
            term = (es[g] / tot) * o_scr[g, rows, :]
            out = term if out is None else out + term
        out_ref[0, rows, :] = out.astype(out_ref.dtype)


def _dilated_attention(ya, slopes_all):
    B, L, _ = ya.shape
    hpp = A_HEADS_PER_PAIR
    assert all((L // dil) % BLOCK == 0 for _, dil in A_PAIRS)
    specs = [pl.BlockSpec((1, L, HEAD_DIM), lambda b, h, c=part * A_HEADS + g * hpp: (b, 0, c + h))
             for g in range(len(A_PAIRS)) for part in range(3)]
    scratch = pltpu.VMEM((len(A_PAIRS), L, HEAD_DIM), F32)
    out = pl.pallas_call(
        _dilated_kernel,
        grid=(B, hpp),
        in_specs=[pl.BlockSpec(memory_space=pltpu.SMEM)] + specs,
        out_specs=pl.BlockSpec((1, L, HEAD_DIM), lambda b, h: (b, 0, h)),
        out_shape=jax.ShapeDtypeStruct((B, L, hpp * HEAD_DIM), BF16),
        scratch_shapes=[scratch, scratch],
        compiler_params=_cparams(("parallel", "arbitrary")),
        name="dilated_attention",
    )(slopes_all[np.array(A_SLOPE_IDX)], *([ya] * (3 * len(A_PAIRS))))
    return out.reshape(B * L, hpp * HEAD_DIM)


def _logsig_cumsum_kernel(z_ref, bias_ref, o_ref):
    L = z_ref.shape[1]
    row = lax.broadcasted_iota(jnp.int32, (BLOCK, BLOCK), 0)
    col = lax.broadcasted_iota(jnp.int32, (BLOCK, BLOCK), 1)
    tri = jnp.where(row >= col, 1.0, 0.0).astype(F32)
    carry = jnp.zeros((1, LANES), F32)
    for j in range(L // BLOCK):
        z = z_ref[0, j * BLOCK:(j + 1) * BLOCK, :] + bias_ref[...]
        ls = jnp.minimum(z, 0.0) - jnp.log(1.0 + jnp.exp(-jnp.abs(z)))
        c = jnp.dot(tri, ls, preferred_element_type=F32, precision=lax.Precision.HIGHEST) + carry
        o_ref[0, j * BLOCK:(j + 1) * BLOCK, :] = c
        carry = c[BLOCK - 1:BLOCK, :]


def _logsig_cumsum(z, bias):
    B, L, _ = z.shape
    spec = pl.BlockSpec((1, L, LANES), lambda b: (b, 0, 0))
    return pl.pallas_call(
        _logsig_cumsum_kernel,
        grid=(B,),
        in_specs=[spec, pl.BlockSpec((1, LANES), lambda b: (0, 0))],
        out_specs=spec,
        out_shape=jax.ShapeDtypeStruct((B, L, LANES), F32),
        compiler_params=_cparams(("parallel",)),
        name="logsig_cumsum",
    )(z, bias)


def _fox_kernel(q_ref, k_ref, v_ref, cq_ref, ck_ref, o_ref):
    i = pl.program_id(1)
    nb = k_ref.shape[1] // BLOCK

    def body(kw):
        qpos = i * BLOCK + lax.broadcasted_iota(jnp.int32, (BLOCK, kw), 0)
        kpos = lax.broadcasted_iota(jnp.int32, (BLOCK, kw), 1)
        maskadd = jnp.where(qpos >= kpos, 0.0, NEG)
        for h in range(C_HEADS):
            cs = slice(h * HEAD_DIM, (h + 1) * HEAD_DIM)
            q = q_ref[0, :, cs].astype(BF16)
            k = k_ref[0, :kw, cs].astype(BF16)
            v = v_ref[0, :kw, cs].astype(BF16)
            s2 = _dot_nt(q, k) * (ATTN_SCALE * LOG2E) + cq_ref[0, h] * LOG2E - ck_ref[0, h, :, :kw] * LOG2E
            o_ref[0, :, cs] = _softmax2_pv(s2 + maskadd, v).astype(o_ref.dtype)

    _for_causal_class(i, nb, body)


def _forgetting_attention(y3, c_col, c_row):
    B, L, _ = y3.shape
    nb = L // BLOCK
    cw = C_HEADS * HEAD_DIM
    return pl.pallas_call(
        _fox_kernel,
        grid=(B, nb),
        in_specs=[pl.BlockSpec((1, BLOCK, cw), lambda b, i: (b, i, CB_CQ // C_HEADS)),
                  pl.BlockSpec((1, L, cw), lambda b, i: (b, 0, CB_CK // C_HEADS)),
                  pl.BlockSpec((1, L, cw), lambda b, i: (b, 0, CB_CV // C_HEADS)),
                  pl.BlockSpec((1, C_HEADS, BLOCK, 1), lambda b, i: (b, 0, i, 0)),
                  pl.BlockSpec((1, C_HEADS, 1, L), lambda b, i: (b, 0, 0, 0))],
        out_specs=pl.BlockSpec((1, BLOCK, cw), lambda b, i: (b, i, 0)),
        out_shape=jax.ShapeDtypeStruct((B, L, cw), BF16),
        compiler_params=_cparams(("parallel", "arbitrary")),
        name="forgetting_attention",
    )(y3, y3, y3, c_col, c_row)


def _order_key(x):
    bits = lax.bitcast_convert_type(x, jnp.int32)
    return bits ^ ((bits >> 31) & jnp.int32(0x7FFFFFFF))


def _kth_largest_key(key, k):
    rows = key.shape[0]

    def count_ge(t):
        return jnp.sum(jnp.where(key >= t, 1.0, 0.0), axis=-1, keepdims=True)

    t0 = jnp.where(count_ge(jnp.zeros((rows, 1), jnp.int32)) >= k,
                   jnp.int32(0), jnp.int32(-2 ** 31)) + jnp.zeros((rows, 1), jnp.int32)

    def two_bits(it, t):
        lo = jnp.left_shift(jnp.int32(1), 29 - 2 * it)
        c1, c2 = t | lo, t | (lo + lo)
        c3 = c2 | lo
        n1, n2, n3 = count_ge(c1), count_ge(c2), count_ge(c3)
        return jnp.where(n3 >= k, c3, jnp.where(n2 >= k, c2, jnp.where(n1 >= k, c1, t)))

    t = lax.fori_loop(0, 15, two_bits, t0)
    last = t | jnp.int32(1)
    return jnp.where(count_ge(last) >= k, last, t)


def _dsa_kernel(slope_ref, iq_ref, ik_ref, iw_ref, q_ref, k_ref, v_ref, o_ref, *, n_top):
    i = pl.program_id(1)
    nb = k_ref.shape[1] // BLOCK
    iw = iw_ref[0]

    def body(kw):
        ik = ik_ref[0, :kw, :].astype(BF16)
        score = jnp.zeros((BLOCK, kw), F32)
        hs = D_HEAD_STACK
        for h0 in range(0, D_IDX_HEADS, hs):
            iq_rows = jnp.concatenate([iq_ref[0, :, (h0 + r) * D_IDX_DIM:(h0 + r + 1) * D_IDX_DIM].astype(BF16)
                                       for r in range(hs)], axis=0)
            z = _dot_nt(iq_rows, ik)
            for r in range(hs):
                rel = jnp.maximum(z[r * BLOCK:(r + 1) * BLOCK], 0.0)
                score = score + iw[:, h0 + r:h0 + r + 1] * rel
        qpos = i * BLOCK + lax.broadcasted_iota(jnp.int32, (BLOCK, kw), 0)
        kpos = lax.broadcasted_iota(jnp.int32, (BLOCK, kw), 1)
        dist = qpos - kpos
        causal = dist >= 0
        key = _order_key(jnp.where(causal, score, NEG))

        thr = _kth_largest_key(key, float(n_top))
        gt = key > thr
        eq = key == thr
        need = float(n_top) - jnp.sum(jnp.where(gt, 1.0, 0.0), axis=-1, keepdims=True)
        row = lax.broadcasted_iota(jnp.int32, (BLOCK, BLOCK), 0)
        col = lax.broadcasted_iota(jnp.int32, (BLOCK, BLOCK), 1)
        upper = jnp.where(row <= col, 1.0, 0.0).astype(BF16)
        eqf = jnp.where(eq, 1.0, 0.0).astype(BF16)
        carry = jnp.zeros((BLOCK, 1), F32)
        rank_tiles = []
        for j in range(kw // BLOCK):
            rank = carry + jnp.dot(eqf[:, j * BLOCK:(j + 1) * BLOCK], upper, preferred_element_type=F32)
            rank_tiles.append(rank)
            carry = rank[:, BLOCK - 1:BLOCK]
        rank = jnp.concatenate(rank_tiles, axis=1)
        mask = (gt | (eq & (rank <= need))) & causal
        maskadd = jnp.where(mask, 0.0, NEG)

        k = k_ref[0, :kw, :].astype(BF16)
        v = v_ref[0, :kw, :].astype(BF16)
        distf = dist.astype(F32)
        hs = D_HEAD_STACK if kw <= D_STACK_MAX_KEYS else D_HEAD_STACK // 2
        for h0 in range(0, D_HEADS, hs):
            q_rows = jnp.concatenate([q_ref[0, :, (h0 + r) * HEAD_DIM:(h0 + r + 1) * HEAD_DIM].astype(BF16)
                                      for r in range(hs)], axis=0)
            outs = _stacked_softmax2_pv(
                q_rows, k, v,
                lambda r, qk, h0=h0: (qk * (ATTN_SCALE * LOG2E) - (slope_ref[h0 + r] * LOG2E) * distf + maskadd), hs)
            for r in range(hs):
                o_ref[0, :, (h0 + r) * HEAD_DIM:(h0 + r + 1) * HEAD_DIM] = outs[r].astype(o_ref.dtype)

    _for_causal_class(i, nb, body)


def _indexed_sparse_attention(y3, ik, iw, slopes_all):
    B, L, _ = y3.shape
    nb = L // BLOCK
    n_top = min(D_TOPK, L // 4)
    slopes = slopes_all[np.array(D_SLOPE_IDX)]
    qw, iqw = D_HEADS * HEAD_DIM, D_IDX_HEADS * D_IDX_DIM
    return pl.pallas_call(
        functools.partial(_dsa_kernel, n_top=n_top),
        grid=(B, nb),
        in_specs=[pl.BlockSpec(memory_space=pltpu.SMEM),
                  pl.BlockSpec((1, BLOCK, iqw), lambda b, i: (b, i, CB_DIQ * LANES // iqw)),
                  pl.BlockSpec((1, L, D_IDX_DIM), lambda b, i: (b, 0, 0)),
                  pl.BlockSpec((1, BLOCK, D_IDX_HEADS), lambda b, i: (b, i, 0)),
                  pl.BlockSpec((1, BLOCK, qw), lambda b, i: (b, i, CB_DQ * LANES // qw)),
                  pl.BlockSpec((1, L, HEAD_DIM), lambda b, i: (b, 0, CB_DK)),
                  pl.BlockSpec((1, L, HEAD_DIM), lambda b, i: (b, 0, CB_DV))],
        out_specs=pl.BlockSpec((1, BLOCK, qw), lambda b, i: (b, i, 0)),
        out_shape=jax.ShapeDtypeStruct((B, L, qw), BF16),
        compiler_params=_cparams(("parallel", "arbitrary")),
        name="indexed_sparse_attention",
    )(slopes, y3, ik, iw, y3, y3, y3)


def _compress_kernel(x_ref, pos_ref, w1_ref, w2_ref, o_ref):
    x = x_ref[0, 0].astype(F32)
    nxt = pltpu.roll(x, x.shape[0] - 1, 0)
    blk = jnp.concatenate([x, nxt], axis=1) + pos_ref[0]
    h = jnp.dot(blk.astype(BF16), w1_ref[0], preferred_element_type=F32)
    h = jax.nn.gelu(h, approximate=True)
    o_ref[0, 0] = jnp.dot(h.astype(BF16), w2_ref[0], preferred_element_type=F32).astype(o_ref.dtype)


def _compress(xc, pos, w1, w2):
    B, _, n_chunk, cw = xc.shape
    G = B_KV_GROUPS
    return pl.pallas_call(
        _compress_kernel,
        grid=(2 * G, B),
        in_specs=[pl.BlockSpec((1, 1, n_chunk, cw), lambda a, b: (b, a, 0, 0)),
                  pl.BlockSpec((1, 1, 2 * cw), lambda a, b: (a // G, 0, 0)),
                  pl.BlockSpec((1, 2 * cw, B_CMP_HIDDEN), lambda a, b: (a // G, 0, 0)),
                  pl.BlockSpec((1, B_CMP_HIDDEN, HEAD_DIM), lambda a, b: (a // G, 0, 0))],
        out_specs=pl.BlockSpec((1, 1, n_chunk, HEAD_DIM), lambda a, b: (b, a, 0, 0)),
        out_shape=jax.ShapeDtypeStruct((B, 2 * G, n_chunk, HEAD_DIM), BF16),
        compiler_params=_cparams(("arbitrary", "arbitrary")),
        name="nsa_compress",
    )(xc, pos, w1, w2)


def _nsa_kernel(slope_ref, *refs, n_cmp, n_slc, window, n_prev):
    G, R = B_KV_GROUPS, B_GROUP_SIZE
    q_refs, ks_refs, vs_refs, kw_refs, vw_refs = (refs[n * G:(n + 1) * G] for n in range(5))
    ckv_ref, gl_ref, o_ref, sel_ref, cmp_ref, win_ref = refs[5 * G:]
    i = pl.program_id(1)
    nb = ks_refs[0].shape[1] // BLOCK
    gate = _sigmoid(gl_ref[0])
    hd = lambda h: slice(h * HEAD_DIM, (h + 1) * HEAD_DIM)

    kwb = min(n_prev + 1, nb)
    wlen = kwb * BLOCK
    start = pl.multiple_of(jnp.maximum(i - n_prev, 0) * BLOCK, BLOCK)
    wq = i * BLOCK + lax.broadcasted_iota(jnp.int32, (BLOCK, wlen), 0)
    wk = start + lax.broadcasted_iota(jnp.int32, (BLOCK, wlen), 1)
    wdist = wq - wk
    wmask = jnp.where((wdist >= 0) & (wdist <= window), 0.0, NEG)
    wdistf = wdist.astype(F32)
    rows = lambda r: slice(r * BLOCK, (r + 1) * BLOCK)
    q_stack = [jnp.concatenate([q_refs[g][0, :, hd(r)].astype(BF16) for r in range(R)], axis=0) for g in range(G)]
    for g in range(G):
        kwin = kw_refs[g][0, pl.ds(start, wlen), :].astype(BF16)
        vwin = vw_refs[g][0, pl.ds(start, wlen), :].astype(BF16)
        outs = _stacked_softmax2_pv(
            q_stack[g], kwin, vwin,
            lambda r, qk, g=g: qk * (ATTN_SCALE * LOG2E) - (slope_ref[g * R + r] * LOG2E) * wdistf + wmask, R)
        for r in range(R):
            h = g * R + r
            win_ref[:, hd(h)] = gate[:, 2 * B_HEADS + h:2 * B_HEADS + h + 1] * outs[r]

    nc = ckv_ref.shape[2]
    t = i * BLOCK + lax.broadcasted_iota(jnp.int32, (BLOCK, nc), 0)
    n = lax.broadcasted_iota(jnp.int32, (BLOCK, nc), 1)
    dist_c = t - (n * B_CMP_STRIDE + B_CMP_LEN - 1)
    mask = (dist_c >= 0) & (n < n_cmp)
    distf = dist_c.astype(F32)

    nn = lax.broadcasted_iota(jnp.int32, (nc, LANES), 0)
    jj = lax.broadcasted_iota(jnp.int32, (nc, LANES), 1)
    off = nn - B_SEL_RATIO * jj + 1
    w = jnp.where((off == 0) | (off == B_SEL_RATIO), 1.0, jnp.where((off > 0) & (off < B_SEL_RATIO), 2.0, 0.0))
    w = jnp.where((nn < n_cmp) & (jj < n_slc), w, 0.0).astype(F32)
    tq = i * BLOCK + lax.broadcasted_iota(jnp.int32, (BLOCK, LANES), 0)
    j = lax.broadcasted_iota(jnp.int32, (BLOCK, LANES), 1)
    cur = tq >> int(np.log2(B_SEL_LEN))
    forced = (j == 0) | (j == cur) | (j == cur - 1)
    scores = []
    for g in range(G):
        kc = ckv_ref[0, g]
        vc = ckv_ref[0, G + g]
        imp = jnp.zeros((BLOCK, nc), F32)
        qk = _dot_nt(q_stack[g], kc)
        ps = []
        for r in range(R):
            s = qk[rows(r)] * ATTN_SCALE
            s = jnp.where(mask, s - slope_ref[g * R + r] * distf, NEG)
            m = jnp.max(s, axis=-1, keepdims=True)
            e = jnp.where(mask, jnp.exp(s - m), 0.0)
            p = e / jnp.maximum(jnp.sum(e, axis=-1, keepdims=True), 1e-30)
            ps.append(p.astype(BF16))
            imp = imp + p
        pv = jnp.dot(jnp.concatenate(ps, axis=0), vc, preferred_element_type=F32)
        for r in range(R):
            h = g * R + r
            cmp_ref[:, hd(h)] = gate[:, h:h + 1] * pv[rows(r)]
        p_slc = jnp.dot(imp, w, preferred_element_type=F32, precision=lax.Precision.HIGHEST)
        score = jnp.where(forced, 1e9, jnp.where(j <= cur, p_slc, -1e9))
        scores.append(jnp.where(j < n_slc, score, -3e38))

    score = jnp.concatenate(scores, axis=0)
    forced2 = jnp.concatenate([forced] * G, axis=0)
    jf = lax.broadcasted_iota(jnp.int32, score.shape, 1).astype(F32)
    sel = jnp.where(forced2, 1.0, 0.0)
    score = jnp.where(forced2, -3e38, score)
    n_forced = 3
    for _ in range(min(B_N_SEL, n_slc) - n_forced):
        m = jnp.max(score, axis=-1, keepdims=True)
        first = jnp.min(jnp.where(score == m, jf, float(LANES)), axis=-1, keepdims=True)
        hit = jf == first
        sel = jnp.where(hit, 1.0, sel)
        score = jnp.where(hit, -3e38, score)
    for g in range(G):
        sel_ref[g] = sel[g * BLOCK:(g + 1) * BLOCK].astype(sel_ref.dtype)

    def body(kw):
        jj = lax.broadcasted_iota(jnp.int32, (LANES, kw), 0)
        ss = lax.broadcasted_iota(jnp.int32, (LANES, kw), 1)
        expand = jnp.where((ss >> int(np.log2(B_SEL_LEN))) == jj, 1.0, 0.0).astype(BF16)
        qpos = i * BLOCK + lax.broadcasted_iota(jnp.int32, (BLOCK, kw), 0)
        kpos = lax.broadcasted_iota(jnp.int32, (BLOCK, kw), 1)
        dist = qpos - kpos
        distf = dist.astype(F32)
        for g in range(G):
            picked = jnp.dot(sel_ref[g], expand, preferred_element_type=F32) > 0.5
            maskadd = jnp.where(picked & (dist >= 0), 0.0, NEG)
            k = ks_refs[g][0, :kw, :].astype(BF16)
            v = vs_refs[g][0, :kw, :].astype(BF16)
            hs = R if kw <= D_STACK_MAX_KEYS else R // 2
            for r0 in range(0, R, hs):
                q_rows = jnp.concatenate([q_refs[g][0, :, hd(r0 + r)].astype(BF16) for r in range(hs)], axis=0)
                outs = _stacked_softmax2_pv(
                    q_rows, k, v,
                    lambda r, qk, h0=g * R + r0, maskadd=maskadd: (
                        qk * (ATTN_SCALE * LOG2E) - (slope_ref[h0 + r] * LOG2E) * distf + maskadd), hs)
                for r in range(hs):
                    h = g * R + r0 + r
                    o_slc = gate[:, B_HEADS + h:B_HEADS + h + 1] * outs[r]
                    o_ref[0, :, hd(h)] = (cmp_ref[:, hd(h)] + o_slc + win_ref[:, hd(h)]).astype(o_ref.dtype)

    _for_causal_class(i, nb, body)


def _nsa_attention(y3, cmp_kv, gate_logits, slopes, n_cmp, n_slc):
    B, L, _ = y3.shape
    G, R = B_KV_GROUPS, B_GROUP_SIZE
    nc = cmp_kv.shape[2]
    w_steps = B_WINDOW - 1
    q_specs = [pl.BlockSpec((1, BLOCK, R * HEAD_DIM), lambda b, i, g=g: (b, i, CB_BQ // R + g)) for g in range(G)]
    kv_specs = [pl.BlockSpec((1, L, HEAD_DIM), lambda b, i, c=CB_BKV + (br * 2 + kv) * G + g: (b, 0, c))
                for br in (1, 2) for kv in range(2) for g in range(G)]
    hw = B_HEADS * HEAD_DIM
    return pl.pallas_call(
        functools.partial(_nsa_kernel, n_cmp=n_cmp, n_slc=n_slc, window=w_steps, n_prev=-(-w_steps // BLOCK)),
        grid=(B, L // BLOCK),
        in_specs=[pl.BlockSpec(memory_space=pltpu.SMEM)] + q_specs + kv_specs +
                 [pl.BlockSpec((1, 2 * G, nc, HEAD_DIM), lambda b, i: (b, 0, 0, 0)),
                  pl.BlockSpec((1, BLOCK, gate_logits.shape[2]), lambda b, i: (b, i, 0))],
        out_specs=pl.BlockSpec((1, BLOCK, hw), lambda b, i: (b, i, 0)),
        out_shape=jax.ShapeDtypeStruct((B, L, hw), BF16),
        scratch_shapes=[pltpu.VMEM((G, BLOCK, LANES), BF16), pltpu.VMEM((BLOCK, hw), F32),
                        pltpu.VMEM((BLOCK, hw), F32)],
        compiler_params=_cparams(("parallel", "arbitrary")),
        name="nsa_attention",
    )(slopes, *([y3] * (5 * G)), cmp_kv, gate_logits)


def _native_sparse_attention(y3, gate_logits, cmp_w1, cmp_w2, cmp_pos, slopes_all):
    B, L, _ = y3.shape
    G, R = B_KV_GROUPS, B_GROUP_SIZE
    slopes = slopes_all[np.array(B_SLOPE_IDX)]
    n_chunk = L // B_CMP_STRIDE
    n_cmp = n_chunk - B_CMP_LEN // B_CMP_STRIDE + 1
    n_slc = L // B_SEL_LEN
    assert B_CMP_LEN == 2 * B_CMP_STRIDE and n_chunk % 8 == 0 and n_slc <= LANES

    xc = y3[:, :, CB_BKV * LANES:(CB_BKV + 2 * G) * LANES].reshape(B, L, 2 * G, HEAD_DIM)
    xc = xc.transpose(0, 2, 1, 3).reshape(B, 2 * G, n_chunk, B_CMP_STRIDE * HEAD_DIM)
    cmp_kv = _compress(xc, cmp_pos.reshape(2, 1, B_CMP_LEN * HEAD_DIM), cmp_w1.astype(BF16), cmp_w2.astype(BF16))

    o = _nsa_attention(y3, cmp_kv, gate_logits.reshape(B, L, -1), slopes, n_cmp, n_slc)
    return o.reshape(B * L, B_HEADS * HEAD_DIM)


def _gated_merge_kernel(x_ref, *refs, nk):
    nbr = N_BRANCH
    wg, o, w, bias = refs[:nbr], refs[nbr:2 * nbr], refs[2 * nbr:3 * nbr], refs[3 * nbr:4 * nbr]
    out_ref, acc = refs[4 * nbr], refs[4 * nbr + 1:]

    def logits(c):
        return jnp.dot(x_ref[...], wg[c][...], preferred_element_type=F32)

    def first():
        for c in range(nbr):
            acc[0][c] = logits(c)

    def middle():
        for c in range(nbr):
            acc[0][c] += logits(c)

    def last(has_acc):
        merged = None
        for c in range(nbr):
            z = (acc[0][c] + logits(c)) if has_acc else logits(c)
            term = _sigmoid(z + bias[c][...]) * jnp.dot(o[c][...], w[c][...], preferred_element_type=F32)
            merged = term if merged is None else merged + term
        out_ref[...] = merged.astype(out_ref.dtype)

    _k_steps(nk, first, middle, last)


def _gated_merge(xb, w_gate, b_gate, branch_outs, branch_ws):
    M, K = xb.shape
    D = branch_ws[0].shape[1]
    tm, tn, tk = _tile(M, 1024), _tile(D, 256), _tile(K, 4096)
    nj, nk = D // tn, K // tk
    wg_specs = [pl.BlockSpec((tk, tn), lambda i, j, k, c=c: (k, c * nj + j)) for c in range(N_BRANCH)]
    o_specs = [pl.BlockSpec((tm, o.shape[1]), lambda i, j, k: (i, 0), pipeline_mode=pl.Buffered(1))
               for o in branch_outs]
    w_specs = [pl.BlockSpec((w.shape[0], tn), lambda i, j, k: (0, j)) for w in branch_ws]
    b_specs = [pl.BlockSpec((1, tn), lambda i, j, k, c=c: (0, c * nj + j)) for c in range(N_BRANCH)]
    return pl.pallas_call(
        functools.partial(_gated_merge_kernel, nk=nk),
        grid=(M // tm, nj, nk),
        in_specs=[pl.BlockSpec((tm, tk), lambda i, j, k: (i, k))] + wg_specs + o_specs + w_specs + b_specs,
        out_specs=pl.BlockSpec((tm, tn), lambda i, j, k: (i, j)),
        out_shape=jax.ShapeDtypeStruct((M, D), BF16),
        scratch_shapes=[pltpu.VMEM((N_BRANCH, tm, tn), F32)] if nk > 1 else [],
        compiler_params=_cparams(("parallel", "parallel", "arbitrary")),
        name="gated_merge",
    )(xb, *([w_gate] * N_BRANCH), *branch_outs, *branch_ws, *([b_gate.reshape(1, -1)] * N_BRANCH))


def _attn_weight(w_in):
    D = w_in.shape[0]
    sizes = (A_HEADS * HEAD_DIM,) * 3 + (B_HEADS * HEAD_DIM, 3 * 2 * B_KV_GROUPS * HEAD_DIM, 3 * B_HEADS) + \
            (C_HEADS * HEAD_DIM,) * 3 + (C_HEADS,) + (D_HEADS * HEAD_DIM, HEAD_DIM, HEAD_DIM) + \
            (D_IDX_HEADS * D_IDX_DIM, D_IDX_DIM, D_IDX_HEADS)
    offs = np.concatenate([[0], np.cumsum(sizes)])
    (a_q, a_k, a_v, b_q, b_kv, b_g, c_q, c_k, c_v, c_f, d_q, d_k, d_v, d_iq, d_ik, d_iw) = [
        w_in[:, offs[n]:offs[n + 1]] for n in range(len(sizes))]
    misc = jnp.zeros((D, 2 * LANES), w_in.dtype)
    misc = misc.at[:, MISC_BG:MISC_BG + 3 * B_HEADS].set(b_g)
    misc = misc.at[:, MISC_CF:MISC_CF + C_HEADS].set(c_f)
    misc = misc.at[:, MISC_IK:MISC_IK + D_IDX_DIM].set(d_ik)
    misc = misc.at[:, MISC_IW:MISC_IW + D_IDX_HEADS].set(d_iw)
    w_a = jnp.concatenate([p.astype(BF16) for p in (a_q, a_k, a_v)], axis=1)
    parts = [b_q, c_q, c_k, c_v, d_q, b_kv, d_iq, d_k, d_v, misc]
    w = jnp.concatenate([p.astype(BF16) for p in parts], axis=1)
    assert w_a.shape[1] == N_CB_A * LANES and w.shape[1] == N_CB * LANES
    return w_a, w, int(offs[-1])


def _hybrid_mixer(xb, B, L, w_in, b_forget, b_gate, cmp_w1, cmp_w2, cmp_pos, w_branch, w_out):
    M, D = xb.shape
    slopes_all = jnp.exp2(-8.0 * jnp.arange(1, N_ALIBI + 1, dtype=F32) / N_ALIBI)
    w_a, w_attn, gate_off = _attn_weight(w_in)
    ya = _matmul(xb, w_a, F32, (1024, 768, 4096)).reshape(B, L, N_CB_A * LANES)
    y3 = _matmul(xb, w_attn, BF16, (1024, 1280, 4096)).reshape(B, L, N_CB * LANES)
    misc = _matmul(xb, w_attn[:, CB_MISC * LANES:(CB_MISC + 1) * LANES], F32)

    bias = jnp.zeros((1, LANES), F32).at[0, MISC_CF:MISC_CF + C_HEADS].set(b_forget)
    c = _logsig_cumsum(misc.reshape(B, L, LANES), bias)[:, :, MISC_CF:MISC_CF + C_HEADS].transpose(0, 2, 1)

    o_a = _dilated_attention(ya, slopes_all)
    o_b = _native_sparse_attention(y3, misc[:, MISC_BG:MISC_BG + 3 * B_HEADS], cmp_w1, cmp_w2, cmp_pos, slopes_all)
    o_c = _forgetting_attention(y3, c[..., None], c[:, :, None, :])
    o_d = _indexed_sparse_attention(y3, misc[:, MISC_IK:MISC_IK + D_IDX_DIM].reshape(B, L, D_IDX_DIM),
                                    misc[:, MISC_IW:MISC_IW + D_IDX_HEADS].reshape(B, L, D_IDX_HEADS), slopes_all)

    sizes = (A_HEADS_PER_PAIR * HEAD_DIM, B_HEADS * HEAD_DIM, C_HEADS * HEAD_DIM, D_HEADS * HEAD_DIM)
    offs = np.concatenate([[0], np.cumsum(sizes)])
    ws = [w_branch[offs[n]:offs[n + 1]] for n in range(N_BRANCH)]
    outs = [o_a, o_b, o_c.reshape(M, -1), o_d.reshape(M, -1)]
    merged = _gated_merge(xb, w_in[:, gate_off:].astype(BF16), b_gate, outs, ws)
    return _matmul(merged, w_out, F32)


def kernel(x, ln_g, ln_b, ffn1_w_gate, ffn1_w_up, ffn1_w_down, w_in, b_forget, b_gate, cmp_w1, cmp_w2, cmp_pos,
           w_branch, w_out, ffn2_w_gate, ffn2_w_up, ffn2_w_down):
    B, L, D = x.shape
    assert L % BLOCK == 0 and D % LANES == 0
    xf = x.reshape(B * L, D)
    xb = xf.astype(BF16)

    def ffn(xf, xb, wg, wu, wd, l, g, b):
        h = _ffn_up(xb, _cast_bf16(wg, l), _cast_bf16(wu, l))
        y = _matmul(h, _cast_bf16(wd, l), F32)
        return _add_ln(xf, y, g, b, 0.5)

    for l in range(ln_g.shape[0]):
        xf, xb = ffn(xf, xb, ffn1_w_gate, ffn1_w_up, ffn1_w_down, l, ln_g[l, 0], ln_b[l, 0])
        y = _hybrid_mixer(xb, B, L, w_in[l], b_forget[l], b_gate[l], cmp_w1[l], cmp_w2[l], cmp_pos[l],
                          _cast_bf16(w_branch, l), _cast_bf16(w_out, l))
        xf, xb = _add_ln(xf, y, ln_g[l, 1], ln_b[l, 1], 1.0)
        xf, xb = ffn(xf, xb, ffn2_w_gate, ffn2_w_up, ffn2_w_down, l, ln_g[l, 2], ln_b[l, 2])
    return xf.reshape(B, L, D)
```

```python
import functools

import numpy as np
import jax
import jax.numpy as jnp
from jax import lax
from jax.experimental import pallas as pl
from jax.experimental.pallas import tpu as pltpu

F32 = jnp.float32
BF16 = jnp.bfloat16

HEAD_DIM = 128
BLOCK = 128
LANES = 128
NEG = -1e30
ATTN_SCALE = HEAD_DIM ** -0.5
LOG2E = 1.4426950408889634
DEPTH = 2
ALPHA = (2 * DEPTH) ** 0.25
LN_EPS = 1e-5

A_PAIRS = ((128, 1), (512, 4), (2048, 16))
A_HEADS_PER_PAIR = 4
A_HEADS = A_HEADS_PER_PAIR * len(A_PAIRS)
B_HEADS = 8
B_KV_GROUPS = 2
B_GROUP_SIZE = B_HEADS // B_KV_GROUPS
B_CMP_LEN = 32
B_CMP_STRIDE = 16
B_SEL_LEN = 64
B_SEL_RATIO = B_SEL_LEN // B_CMP_STRIDE
B_N_SEL = 8
B_WINDOW = 512
B_CMP_HIDDEN = 512
C_HEADS = 8
D_HEADS = 8
D_IDX_HEADS = 8
D_IDX_DIM = 64
D_TOPK = 256
D_HEAD_STACK = 4
D_STACK_MAX_KEYS = 768
N_BRANCH = 4
N_ALIBI = A_HEADS + B_HEADS + D_HEADS
A_SLOPE_IDX = (0, 1, 2, 3, 12, 13, 14, 15, 24, 25, 26, 27)
B_SLOPE_IDX = (4, 5, 6, 7, 8, 9, 10, 11)
D_SLOPE_IDX = (16, 17, 18, 19, 20, 21, 22, 23)

N_CB_A = 3 * A_HEADS
CB_BQ = 0
CB_CQ, CB_CK, CB_CV = 8, 16, 24
CB_DQ = 32
CB_BKV = 40
CB_DIQ = 52
CB_DK, CB_DV = 56, 57
CB_MISC = 58
N_CB = 60
MISC_BG, MISC_CF, MISC_IK, MISC_IW = 0, 24, 32, 96

CAUSAL_CLASSES = 16
VMEM_LIMIT = 56 * 1024 * 1024


def _cparams(sem):
    return pltpu.CompilerParams(dimension_semantics=sem, vmem_limit_bytes=VMEM_LIMIT)


def _tile(dim, pref):
    return pref if dim % pref == 0 else dim


def _dot_nt(a, b):
    return lax.dot_general(a, b, (((1,), (1,)), ((), ())), preferred_element_type=F32)


def _softmax2_pv(s2, v):
    m = jnp.max(s2, axis=-1, keepdims=True)
    e = jnp.exp2(s2 - m)
    den = jnp.maximum(jnp.sum(e, axis=-1, keepdims=True), 1e-30)
    return jnp.dot(e.astype(BF16), v, preferred_element_type=F32) / den


def _stacked_softmax2_pv(q_stack, k, v, logits2, n):
    qk = _dot_nt(q_stack, k)
    es, dens = [], []
    for r in range(n):
        s2 = logits2(r, qk[r * BLOCK:(r + 1) * BLOCK])
        e = jnp.exp2(s2 - jnp.max(s2, axis=-1, keepdims=True))
        dens.append(jnp.maximum(jnp.sum(e, axis=-1, keepdims=True), 1e-30))
        es.append(e.astype(BF16))
    pv = jnp.dot(jnp.concatenate(es, axis=0), v, preferred_element_type=F32)
    return [pv[r * BLOCK:(r + 1) * BLOCK] / dens[r] for r in range(n)]


def _for_causal_class(i, nb, body):
    n_cls = CAUSAL_CLASSES if nb % CAUSAL_CLASSES == 0 else 1
    per = nb // n_cls
    for c in range(n_cls):
        pl.when((i >= c * per) & (i < (c + 1) * per))(functools.partial(body, (c + 1) * per * BLOCK))


def _cast_kernel(x_ref, o_ref):
    o_ref[...] = x_ref[0].astype(o_ref.dtype)


def _cast_bf16(w_stack, l):
    _, R, C = w_stack.shape
    tr = R
    for cand in (1024, 512, 256, 128, 64, 32, 16):
        if R % cand == 0:
            tr = cand
            if cand * C * 4 <= 8 * 1024 * 1024:
                break
    return pl.pallas_call(
        _cast_kernel,
        grid=(R // tr,),
        in_specs=[pl.BlockSpec((1, tr, C), lambda i: (l, i, 0))],
        out_specs=pl.BlockSpec((tr, C), lambda i: (i, 0)),
        out_shape=jax.ShapeDtypeStruct((R, C), BF16),
        compiler_params=_cparams(("parallel",)),
        name="cast_bf16",
    )(w_stack)


def _sigmoid(z):
    return 0.5 * jnp.tanh(0.5 * z) + 0.5


def _k_steps(nk, first, middle, last):
    k = pl.program_id(2)
    if nk == 1:
        last(False)
        return
    pl.when(k == 0)(first)
    if nk > 2:
        pl.when((k > 0) & (k < nk - 1))(middle)
    pl.when(k == nk - 1)(functools.partial(last, True))


def _mm_kernel(x_ref, w_ref, o_ref, *acc, nk):
    def prod():
        return jnp.dot(x_ref[...], w_ref[...], preferred_element_type=F32)

    def first():
        acc[0][...] = prod()

    def middle():
        acc[0][...] += prod()

    def last(has_acc):
        o_ref[...] = ((acc[0][...] + prod()) if has_acc else prod()).astype(o_ref.dtype)

    _k_steps(nk, first, middle, last)


def _matmul(x, w, out_dtype, tiles=(1024, 1024, 4096)):
    M, K = x.shape
    N = w.shape[1]
    tm, tn, tk = _tile(M, tiles[0]), _tile(N, tiles[1]), _tile(K, tiles[2])
    nk = K // tk
    return pl.pallas_call(
        functools.partial(_mm_kernel, nk=nk),
        grid=(M // tm, N // tn, nk),
        in_specs=[pl.BlockSpec((tm, tk), lambda i, j, k: (i, k)),
                  pl.BlockSpec((tk, tn), lambda i, j, k: (k, j))],
        out_specs=pl.BlockSpec((tm, tn), lambda i, j, k: (i, j)),
        out_shape=jax.ShapeDtypeStruct((M, N), out_dtype),
        scratch_shapes=[pltpu.VMEM((tm, tn), F32)] if nk > 1 else [],
        compiler_params=_cparams(("parallel", "parallel", "arbitrary")),
        name="matmul",
    )(x, w)


def _ffn_up_kernel(x_ref, wg_ref, wu_ref, o_ref, *acc, nk):
    def prods():
        x = x_ref[...]
        return (jnp.dot(x, wg_ref[...], preferred_element_type=F32),
                jnp.dot(x, wu_ref[...], preferred_element_type=F32))

    def first():
        acc[0][...], acc[1][...] = prods()

    def middle():
        g, u = prods()
        acc[0][...] += g
        acc[1][...] += u

    def last(has_acc):
        g, u = prods()
        if has_acc:
            g, u = acc[0][...] + g, acc[1][...] + u
        o_ref[...] = (g * _sigmoid(g) * u).astype(o_ref.dtype)

    _k_steps(nk, first, middle, last)


def _ffn_up(x, wg, wu, tiles=(1024, 512, 4096)):
    M, K = x.shape
    N = wg.shape[1]
    tm, tn, tk = _tile(M, tiles[0]), _tile(N, tiles[1]), _tile(K, tiles[2])
    nk = K // tk
    return pl.pallas_call(
        functools.partial(_ffn_up_kernel, nk=nk),
        grid=(M // tm, N // tn, nk),
        in_specs=[pl.BlockSpec((tm, tk), lambda i, j, k: (i, k)),
                  pl.BlockSpec((tk, tn), lambda i, j, k: (k, j)),
                  pl.BlockSpec((tk, tn), lambda i, j, k: (k, j))],
        out_specs=pl.BlockSpec((tm, tn), lambda i, j, k: (i, j)),
        out_shape=jax.ShapeDtypeStruct((M, N), BF16),
        scratch_shapes=[pltpu.VMEM((tm, tn), F32), pltpu.VMEM((tm, tn), F32)] if nk > 1 else [],
        compiler_params=_cparams(("parallel", "parallel", "arbitrary")),
        name="ffn_up",
    )(x, wg, wu)


def _add_ln_kernel(x_ref, y_ref, g_ref, b_ref, o_ref, ob_ref, *, cy):
    z = ALPHA * x_ref[...] + cy * y_ref[...]
    mu = jnp.mean(z, axis=-1, keepdims=True)
    zc = z - mu
    var = jnp.mean(zc * zc, axis=-1, keepdims=True)
    out = zc * lax.rsqrt(var + LN_EPS) * g_ref[...] + b_ref[...]
    o_ref[...] = out
    ob_ref[...] = out.astype(BF16)


def _add_ln(x, y, g, b, cy):
    M, D = x.shape
    tm = _tile(M, 256)
    row = pl.BlockSpec((tm, D), lambda i: (i, 0))
    vec = pl.BlockSpec((1, D), lambda i: (0, 0))
    return pl.pallas_call(
        functools.partial(_add_ln_kernel, cy=cy),
        grid=(M // tm,),
        in_specs=[row, row, vec, vec],
        out_specs=[row, row],
        out_shape=[jax.ShapeDtypeStruct((M, D), F32), jax.ShapeDtypeStruct((M, D), BF16)],
        compiler_params=_cparams(("parallel",)),
        name="add_ln",
    )(x, y, g.reshape(1, D), b.reshape(1, D))


def _dilated_kernel(slope_ref, *refs):
    ng = len(A_PAIRS)
    qkv = [refs[3 * g:3 * g + 3] for g in range(ng)]
    out_ref, o_scr, l_scr = refs[3 * ng:]
    h = pl.program_id(1)
    L = out_ref.shape[1]
    n_units = L // BLOCK

    def unit(u, carry):
        for g, (window, dil) in enumerate(A_PAIRS):
            q_ref, k_ref, v_ref = qkv[g]
            steps = window // dil
            n_prev = -(-steps // BLOCK)
            nbg = n_units // dil
            kw = min(n_prev + 1, nbg) * BLOCK
            r, i = u // nbg, u % nbg
            sb = jnp.maximum(i - n_prev, 0)
            rows_q = pl.ds(i * (BLOCK * dil) + r, BLOCK, stride=dil)
            rows_k = pl.ds(sb * (BLOCK * dil) + r, kw, stride=dil)
            q = q_ref[0, rows_q, :].astype(BF16)
            kwin = k_ref[0, rows_k, :].astype(BF16)
            vwin = v_ref[0, rows_k, :].astype(BF16)
            qpos = i * BLOCK + lax.broadcasted_iota(jnp.int32, (BLOCK, kw), 0)
            kpos = sb * BLOCK + lax.broadcasted_iota(jnp.int32, (BLOCK, kw), 1)
            dist = qpos - kpos
            maskadd = jnp.where((dist >= 0) & (dist <= steps), 0.0, NEG)
            s = _dot_nt(q, kwin) * ATTN_SCALE
            s = s - slope_ref[g * A_HEADS_PER_PAIR + h] * (dil * dist).astype(F32) + maskadd
            m = jnp.max(s, axis=-1, keepdims=True)
            e = jnp.exp(s - m)
            den = jnp.maximum(jnp.sum(e, axis=-1, keepdims=True), 1e-30)
            o_scr[g, rows_q, :] = jnp.dot(e.astype(BF16), vwin, preferred_element_type=F32) / den
            l_scr[g, rows_q, :] = jnp.broadcast_to(m + jnp.log(den), (BLOCK, LANES))
        return carry

    lax.fori_loop(0, n_units, unit, 0, unroll=8)

    for c in range(n_units):
        rows = slice(c * BLOCK, (c + 1) * BLOCK)
        ls = [l_scr[g, rows, :] for g in range(ng)]
        m = functools.reduce(jnp.maximum, ls)
        es = [jnp.exp(l - m) for l in ls]
        tot = functools.reduce(lambda a, b: a + b, es)
        out = None
        for g in range(ng):
            term = (es[g] / tot) * o_scr[g, rows, :]
            out = term if out is None else out + term
        out_ref[0, rows, :] = out.astype(out_ref.dtype)


def _dilated_attention(ya, slopes_all):
    B, L, _ = ya.shape
    hpp = A_HEADS_PER_PAIR
    assert all((L // dil) % BLOCK == 0 for _, dil in A_PAIRS)
    specs = [pl.BlockSpec((1, L, HEAD_DIM), lambda b, h, c=part * A_HEADS + g * hpp: (b, 0, c + h))
             for g in range(len(A_PAIRS)) for part in range(3)]
    scratch = pltpu.VMEM((len(A_PAIRS), L, HEAD_DIM), F32)
    out = pl.pallas_call(
        _dilated_kernel,
        grid=(B, hpp),
        in_specs=[pl.BlockSpec(memory_space=pltpu.SMEM)] + specs,
        out_specs=pl.BlockSpec((1, L, HEAD_DIM), lambda b, h: (b, 0, h)),
        out_shape=jax.ShapeDtypeStruct((B, L, hpp * HEAD_DIM), BF16),
        scratch_shapes=[scratch, scratch],
        compiler_params=_cparams(("parallel", "arbitrary")),
        name="dilated_attention",
    )(slopes_all[np.array(A_SLOPE_IDX)], *([ya] * (3 * len(A_PAIRS))))
    return out.reshape(B * L, hpp * HEAD_DIM)


def _logsig_cumsum_kernel(z_ref, bias_ref, o_ref):
    L = z_ref.shape[1]
    row = lax.broadcasted_iota(jnp.int32, (BLOCK, BLOCK), 0)
    col = lax.broadcasted_iota(jnp.int32, (BLOCK, BLOCK), 1)
    tri = jnp.where(row >= col, 1.0, 0.0).astype(F32)
    carry = jnp.zeros((1, LANES), F32)
    for j in range(L // BLOCK):
        z = z_ref[0, j * BLOCK:(j + 1) * BLOCK, :] + bias_ref[...]
        ls = jnp.minimum(z, 0.0) - jnp.log(1.0 + jnp.exp(-jnp.abs(z)))
        c = jnp.dot(tri, ls, preferred_element_type=F32, precision=lax.Precision.HIGHEST) + carry
        o_ref[0, j * BLOCK:(j + 1) * BLOCK, :] = c
        carry = c[BLOCK - 1:BLOCK, :]


def _logsig_cumsum(z, bias):
    B, L, _ = z.shape
    spec = pl.BlockSpec((1, L, LANES), lambda b: (b, 0, 0))
    return pl.pallas_call(
        _logsig_cumsum_kernel,
        grid=(B,),
        in_specs=[spec, pl.BlockSpec((1, LANES), lambda b: (0, 0))],
        out_specs=spec,
        out_shape=jax.ShapeDtypeStruct((B, L, LANES), F32),
        compiler_params=_cparams(("parallel",)),
        name="logsig_cumsum",
    )(z, bias)


def _fox_kernel(q_ref, k_ref, v_ref, cq_ref, ck_ref, o_ref):
    i = pl.program_id(1)
    nb = k_ref.shape[1] // BLOCK

    def body(kw):
        qpos = i * BLOCK + lax.broadcasted_iota(jnp.int32, (BLOCK, kw), 0)
        kpos = lax.broadcasted_iota(jnp.int32, (BLOCK, kw), 1)
        maskadd = jnp.where(qpos >= kpos, 0.0, NEG)
        for h in range(C_HEADS):
            cs = slice(h * HEAD_DIM, (h + 1) * HEAD_DIM)
            q = q_ref[0, :, cs].astype(BF16)
            k = k_ref[0, :kw, cs].astype(BF16)
            v = v_ref[0, :kw, cs].astype(BF16)
            s2 = _dot_nt(q, k) * (ATTN_SCALE * LOG2E) + cq_ref[0, h] * LOG2E - ck_ref[0, h, :, :kw] * LOG2E
            o_ref[0, :, cs] = _softmax2_pv(s2 + maskadd, v).astype(o_ref.dtype)

    _for_causal_class(i, nb, body)


def _forgetting_attention(y3, c_col, c_row):
    B, L, _ = y3.shape
    nb = L // BLOCK
    cw = C_HEADS * HEAD_DIM
    return pl.pallas_call(
        _fox_kernel,
        grid=(B, nb),
        in_specs=[pl.BlockSpec((1, BLOCK, cw), lambda b, i: (b, i, CB_CQ // C_HEADS)),
                  pl.BlockSpec((1, L, cw), lambda b, i: (b, 0, CB_CK // C_HEADS)),
                  pl.BlockSpec((1, L, cw), lambda b, i: (b, 0, CB_CV // C_HEADS)),
                  pl.BlockSpec((1, C_HEADS, BLOCK, 1), lambda b, i: (b, 0, i, 0)),
                  pl.BlockSpec((1, C_HEADS, 1, L), lambda b, i: (b, 0, 0, 0))],
        out_specs=pl.BlockSpec((1, BLOCK, cw), lambda b, i: (b, i, 0)),
        out_shape=jax.ShapeDtypeStruct((B, L, cw), BF16),
        compiler_params=_cparams(("parallel", "arbitrary")),
        name="forgetting_attention",
    )(y3, y3, y3, c_col, c_row)


def _order_key(x):
    bits = lax.bitcast_convert_type(x, jnp.int32)
    return bits ^ ((bits >> 31) & jnp.int32(0x7FFFFFFF))


def _kth_largest_key(key, k):
    rows = key.shape[0]

    def count_ge(t):
        return jnp.sum(jnp.where(key >= t, 1.0, 0.0), axis=-1, keepdims=True)

    t0 = jnp.where(count_ge(jnp.zeros((rows, 1), jnp.int32)) >= k,
                   jnp.int32(0), jnp.int32(-2 ** 31)) + jnp.zeros((rows, 1), jnp.int32)

    def two_bits(it, t):
        lo = jnp.left_shift(jnp.int32(1), 29 - 2 * it)
        c1, c2 = t | lo, t | (lo + lo)
        c3 = c2 | lo
        n1, n2, n3 = count_ge(c1), count_ge(c2), count_ge(c3)
        return jnp.where(n3 >= k, c3, jnp.where(n2 >= k, c2, jnp.where(n1 >= k, c1, t)))

    t = lax.fori_loop(0, 15, two_bits, t0)
    last = t | jnp.int32(1)
    return jnp.where(count_ge(last) >= k, last, t)


def _dsa_kernel(slope_ref, iq_ref, ik_ref, iw_ref, q_ref, k_ref, v_ref, o_ref, *, n_top):
    i = pl.program_id(1)
    nb = k_ref.shape[1] // BLOCK
    iw = iw_ref[0]

    def body(kw):
        ik = ik_ref[0, :kw, :].astype(BF16)
        score = jnp.zeros((BLOCK, kw), F32)
        hs = D_HEAD_STACK
        for h0 in range(0, D_IDX_HEADS, hs):
            iq_rows = jnp.concatenate([iq_ref[0, :, (h0 + r) * D_IDX_DIM:(h0 + r + 1) * D_IDX_DIM].astype(BF16)
                                       for r in range(hs)], axis=0)
            z = _dot_nt(iq_rows, ik)
            for r in range(hs):
                rel = jnp.maximum(z[r * BLOCK:(r + 1) * BLOCK], 0.0)
                score = score + iw[:, h0 + r:h0 + r + 1] * rel
        qpos = i * BLOCK + lax.broadcasted_iota(jnp.int32, (BLOCK, kw), 0)
        kpos = lax.broadcasted_iota(jnp.int32, (BLOCK, kw), 1)
        dist = qpos - kpos
        causal = dist >= 0
        key = _order_key(jnp.where(causal, score, NEG))

        thr = _kth_largest_key(key, float(n_top))
        gt = key > thr
        eq = key == thr
        need = float(n_top) - jnp.sum(jnp.where(gt, 1.0, 0.0), axis=-1, keepdims=True)
        row = lax.broadcasted_iota(jnp.int32, (BLOCK, BLOCK), 0)
        col = lax.broadcasted_iota(jnp.int32, (BLOCK, BLOCK), 1)
        upper = jnp.where(row <= col, 1.0, 0.0).astype(BF16)
        eqf = jnp.where(eq, 1.0, 0.0).astype(BF16)
        carry = jnp.zeros((BLOCK, 1), F32)
        rank_tiles = []
        for j in range(kw // BLOCK):
            rank = carry + jnp.dot(eqf[:, j * BLOCK:(j + 1) * BLOCK], upper, preferred_element_type=F32)
            rank_tiles.append(rank)
            carry = rank[:, BLOCK - 1:BLOCK]
        rank = jnp.concatenate(rank_tiles, axis=1)
        mask = (gt | (eq & (rank <= need))) & causal
        maskadd = jnp.where(mask, 0.0, NEG)

        k = k_ref[0, :kw, :].astype(BF16)
        v = v_ref[0, :kw, :].astype(BF16)
        distf = dist.astype(F32)
        hs = D_HEAD_STACK if kw <= D_STACK_MAX_KEYS else D_HEAD_STACK // 2
        for h0 in range(0, D_HEADS, hs):
            q_rows = jnp.concatenate([q_ref[0, :, (h0 + r) * HEAD_DIM:(h0 + r + 1) * HEAD_DIM].astype(BF16)
                                      for r in range(hs)], axis=0)
            outs = _stacked_softmax2_pv(
                q_rows, k, v,
                lambda r, qk, h0=h0: (qk * (ATTN_SCALE * LOG2E) - (slope_ref[h0 + r] * LOG2E) * distf + maskadd), hs)
            for r in range(hs):
                o_ref[0, :, (h0 + r) * HEAD_DIM:(h0 + r + 1) * HEAD_DIM] = outs[r].astype(o_ref.dtype)

    _for_causal_class(i, nb, body)


def _indexed_sparse_attention(y3, ik, iw, slopes_all):
    B, L, _ = y3.shape
    nb = L // BLOCK
    n_top = min(D_TOPK, L // 4)
    slopes = slopes_all[np.array(D_SLOPE_IDX)]
    qw, iqw = D_HEADS * HEAD_DIM, D_IDX_HEADS * D_IDX_DIM
    return pl.pallas_call(
        functools.partial(_dsa_kernel, n_top=n_top),
        grid=(B, nb),
        in_specs=[pl.BlockSpec(memory_space=pltpu.SMEM),
                  pl.BlockSpec((1, BLOCK, iqw), lambda b, i: (b, i, CB_DIQ * LANES // iqw)),
                  pl.BlockSpec((1, L, D_IDX_DIM), lambda b, i: (b, 0, 0)),
                  pl.BlockSpec((1, BLOCK, D_IDX_HEADS), lambda b, i: (b, i, 0)),
                  pl.BlockSpec((1, BLOCK, qw), lambda b, i: (b, i, CB_DQ * LANES // qw)),
                  pl.BlockSpec((1, L, HEAD_DIM), lambda b, i: (b, 0, CB_DK)),
                  pl.BlockSpec((1, L, HEAD_DIM), lambda b, i: (b, 0, CB_DV))],
        out_specs=pl.BlockSpec((1, BLOCK, qw), lambda b, i: (b, i, 0)),
        out_shape=jax.ShapeDtypeStruct((B, L, qw), BF16),
        compiler_params=_cparams(("parallel", "arbitrary")),
        name="indexed_sparse_attention",
    )(slopes, y3, ik, iw, y3, y3, y3)


def _compress_kernel(x_ref, pos_ref, w1_ref, w2_ref, o_ref):
    x = x_ref[0, 0].astype(F32)
    nxt = pltpu.roll(x, x.shape[0] - 1, 0)
    blk = jnp.concatenate([x, nxt], axis=1) + pos_ref[0]
    h = jnp.dot(blk.astype(BF16), w1_ref[0], preferred_element_type=F32)
    h = jax.nn.gelu(h, approximate=True)
    o_ref[0, 0] = jnp.dot(h.astype(BF16), w2_ref[0], preferred_element_type=F32).astype(o_ref.dtype)


def _compress(xc, pos, w1, w2):
    B, _, n_chunk, cw = xc.shape
    G = B_KV_GROUPS
    return pl.pallas_call(
        _compress_kernel,
        grid=(2 * G, B),
        in_specs=[pl.BlockSpec((1, 1, n_chunk, cw), lambda a, b: (b, a, 0, 0)),
                  pl.BlockSpec((1, 1, 2 * cw), lambda a, b: (a // G, 0, 0)),
                  pl.BlockSpec((1, 2 * cw, B_CMP_HIDDEN), lambda a, b: (a // G, 0, 0)),
                  pl.BlockSpec((1, B_CMP_HIDDEN, HEAD_DIM), lambda a, b: (a // G, 0, 0))],
        out_specs=pl.BlockSpec((1, 1, n_chunk, HEAD_DIM), lambda a, b: (b, a, 0, 0)),
        out_shape=jax.ShapeDtypeStruct((B, 2 * G, n_chunk, HEAD_DIM), BF16),
        compiler_params=_cparams(("arbitrary", "arbitrary")),
        name="nsa_compress",
    )(xc, pos, w1, w2)


def _nsa_kernel(slope_ref, *refs, n_cmp, n_slc, window, n_prev):
    G, R = B_KV_GROUPS, B_GROUP_SIZE
    q_refs, ks_refs, vs_refs, kw_refs, vw_refs = (refs[n * G:(n + 1) * G] for n in range(5))
    ckv_ref, gl_ref, o_ref, sel_ref, cmp_ref, win_ref = refs[5 * G:]
    i = pl.program_id(1)
    nb = ks_refs[0].shape[1] // BLOCK
    gate = _sigmoid(gl_ref[0])
    hd = lambda h: slice(h * HEAD_DIM, (h + 1) * HEAD_DIM)

    kwb = min(n_prev + 1, nb)
    wlen = kwb * BLOCK
    start = pl.multiple_of(jnp.maximum(i - n_prev, 0) * BLOCK, BLOCK)
    wq = i * BLOCK + lax.broadcasted_iota(jnp.int32, (BLOCK, wlen), 0)
    wk = start + lax.broadcasted_iota(jnp.int32, (BLOCK, wlen), 1)
    wdist = wq - wk
    wmask = jnp.where((wdist >= 0) & (wdist <= window), 0.0, NEG)
    wdistf = wdist.astype(F32)
    rows = lambda r: slice(r * BLOCK, (r + 1) * BLOCK)
    q_stack = [jnp.concatenate([q_refs[g][0, :, hd(r)].astype(BF16) for r in range(R)], axis=0) for g in range(G)]
    for g in range(G):
        kwin = kw_refs[g][0, pl.ds(start, wlen), :].astype(BF16)
        vwin = vw_refs[g][0, pl.ds(start, wlen), :].astype(BF16)
        outs = _stacked_softmax2_pv(
            q_stack[g], kwin, vwin,
            lambda r, qk, g=g: qk * (ATTN_SCALE * LOG2E) - (slope_ref[g * R + r] * LOG2E) * wdistf + wmask, R)
        for r in range(R):
            h = g * R + r
            win_ref[:, hd(h)] = gate[:, 2 * B_HEADS + h:2 * B_HEADS + h + 1] * outs[r]

    nc = ckv_ref.shape[2]
    t = i * BLOCK + lax.broadcasted_iota(jnp.int32, (BLOCK, nc), 0)
    n = lax.broadcasted_iota(jnp.int32, (BLOCK, nc), 1)
    dist_c = t - (n * B_CMP_STRIDE + B_CMP_LEN - 1)
    mask = (dist_c >= 0) & (n < n_cmp)
    distf = dist_c.astype(F32)

    nn = lax.broadcasted_iota(jnp.int32, (nc, LANES), 0)
    jj = lax.broadcasted_iota(jnp.int32, (nc, LANES), 1)
    off = nn - B_SEL_RATIO * jj + 1
    w = jnp.where((off == 0) | (off == B_SEL_RATIO), 1.0, jnp.where((off > 0) & (off < B_SEL_RATIO), 2.0, 0.0))
    w = jnp.where((nn < n_cmp) & (jj < n_slc), w, 0.0).astype(F32)
    tq = i * BLOCK + lax.broadcasted_iota(jnp.int32, (BLOCK, LANES), 0)
    j = lax.broadcasted_iota(jnp.int32, (BLOCK, LANES), 1)
    cur = tq >> int(np.log2(B_SEL_LEN))
    forced = (j == 0) | (j == cur) | (j == cur - 1)
    scores = []
    for g in range(G):
        kc = ckv_ref[0, g]
        vc = ckv_ref[0, G + g]
        imp = jnp.zeros((BLOCK, nc), F32)
        qk = _dot_nt(q_stack[g], kc)
        ps = []
        for r in range(R):
            s = qk[rows(r)] * ATTN_SCALE
            s = jnp.where(mask, s - slope_ref[g * R + r] * distf, NEG)
            m = jnp.max(s, axis=-1, keepdims=True)
            e = jnp.where(mask, jnp.exp(s - m), 0.0)
            p = e / jnp.maximum(jnp.sum(e, axis=-1, keepdims=True), 1e-30)
            ps.append(p.astype(BF16))
            imp = imp + p
        pv = jnp.dot(jnp.concatenate(ps, axis=0), vc, preferred_element_type=F32)
        for r in range(R):
            h = g * R + r
            cmp_ref[:, hd(h)] = gate[:, h:h + 1] * pv[rows(r)]
        p_slc = jnp.dot(imp, w, preferred_element_type=F32, precision=lax.Precision.HIGHEST)
        score = jnp.where(forced, 1e9, jnp.where(j <= cur, p_slc, -1e9))
        scores.append(jnp.where(j < n_slc, score, -3e38))

    score = jnp.concatenate(scores, axis=0)
    forced2 = jnp.concatenate([forced] * G, axis=0)
    jf = lax.broadcasted_iota(jnp.int32, score.shape, 1).astype(F32)
    sel = jnp.where(forced2, 1.0, 0.0)
    score = jnp.where(forced2, -3e38, score)
    n_forced = 3
    for _ in range(min(B_N_SEL, n_slc) - n_forced):
        m = jnp.max(score, axis=-1, keepdims=True)
        first = jnp.min(jnp.where(score == m, jf, float(LANES)), axis=-1, keepdims=True)
        hit = jf == first
        sel = jnp.where(hit, 1.0, sel)
        score = jnp.where(hit, -3e38, score)
    for g in range(G):
        sel_ref[g] = sel[g * BLOCK:(g + 1) * BLOCK].astype(sel_ref.dtype)

    def body(kw):
        jj = lax.broadcasted_iota(jnp.int32, (LANES, kw), 0)
        ss = lax.broadcasted_iota(jnp.int32, (LANES, kw), 1)
        expand = jnp.where((ss >> int(np.log2(B_SEL_LEN))) == jj, 1.0, 0.0).astype(BF16)
        qpos = i * BLOCK + lax.broadcasted_iota(jnp.int32, (BLOCK, kw), 0)
        kpos = lax.broadcasted_iota(jnp.int32, (BLOCK, kw), 1)
        dist = qpos - kpos
        distf = dist.astype(F32)
        for g in range(G):
            picked = jnp.dot(sel_ref[g], expand, preferred_element_type=F32) > 0.5
            maskadd = jnp.where(picked & (dist >= 0), 0.0, NEG)
            k = ks_refs[g][0, :kw, :].astype(BF16)
            v = vs_refs[g][0, :kw, :].astype(BF16)
            hs = R if kw <= D_STACK_MAX_KEYS else R // 2
            for r0 in range(0, R, hs):
                q_rows = jnp.concatenate([q_refs[g][0, :, hd(r0 + r)].astype(BF16) for r in range(hs)], axis=0)
                outs = _stacked_softmax2_pv(
                    q_rows, k, v,
                    lambda r, qk, h0=g * R + r0, maskadd=maskadd: (
                        qk * (ATTN_SCALE * LOG2E) - (slope_ref[h0 + r] * LOG2E) * distf + maskadd), hs)
                for r in range(hs):
                    h = g * R + r0 + r
                    o_slc = gate[:, B_HEADS + h:B_HEADS + h + 1] * outs[r]
                    o_ref[0, :, hd(h)] = (cmp_ref[:, hd(h)] + o_slc + win_ref[:, hd(h)]).astype(o_ref.dtype)

    _for_causal_class(i, nb, body)


def _nsa_attention(y3, cmp_kv, gate_logits, slopes, n_cmp, n_slc):
    B, L, _ = y3.shape
    G, R = B_KV_GROUPS, B_GROUP_SIZE
    nc = cmp_kv.shape[2]
    w_steps = B_WINDOW - 1
    q_specs = [pl.BlockSpec((1, BLOCK, R * HEAD_DIM), lambda b, i, g=g: (b, i, CB_BQ // R + g)) for g in range(G)]
    kv_specs = [pl.BlockSpec((1, L, HEAD_DIM), lambda b, i, c=CB_BKV + (br * 2 + kv) * G + g: (b, 0, c))
                for br in (1, 2) for kv in range(2) for g in range(G)]
    hw = B_HEADS * HEAD_DIM
    return pl.pallas_call(
        functools.partial(_nsa_kernel, n_cmp=n_cmp, n_slc=n_slc, window=w_steps, n_prev=-(-w_steps // BLOCK)),
        grid=(B, L // BLOCK),
        in_specs=[pl.BlockSpec(memory_space=pltpu.SMEM)] + q_specs + kv_specs +
                 [pl.BlockSpec((1, 2 * G, nc, HEAD_DIM), lambda b, i: (b, 0, 0, 0)),
                  pl.BlockSpec((1, BLOCK, gate_logits.shape[2]), lambda b, i: (b, i, 0))],
        out_specs=pl.BlockSpec((1, BLOCK, hw), lambda b, i: (b, i, 0)),
        out_shape=jax.ShapeDtypeStruct((B, L, hw), BF16),
        scratch_shapes=[pltpu.VMEM((G, BLOCK, LANES), BF16), pltpu.VMEM((BLOCK, hw), F32),
                        pltpu.VMEM((BLOCK, hw), F32)],
        compiler_params=_cparams(("parallel", "arbitrary")),
        name="nsa_attention",
    )(slopes, *([y3] * (5 * G)), cmp_kv, gate_logits)


def _native_sparse_attention(y3, gate_logits, cmp_w1, cmp_w2, cmp_pos, slopes_all):
    B, L, _ = y3.shape
    G, R = B_KV_GROUPS, B_GROUP_SIZE
    slopes = slopes_all[np.array(B_SLOPE_IDX)]
    n_chunk = L // B_CMP_STRIDE
    n_cmp = n_chunk - B_CMP_LEN // B_CMP_STRIDE + 1
    n_slc = L // B_SEL_LEN
    assert B_CMP_LEN == 2 * B_CMP_STRIDE and n_chunk % 8 == 0 and n_slc <= LANES

    xc = y3[:, :, CB_BKV * LANES:(CB_BKV + 2 * G) * LANES].reshape(B, L, 2 * G, HEAD_DIM)
    xc = xc.transpose(0, 2, 1, 3).reshape(B, 2 * G, n_chunk, B_CMP_STRIDE * HEAD_DIM)
    cmp_kv = _compress(xc, cmp_pos.reshape(2, 1, B_CMP_LEN * HEAD_DIM), cmp_w1.astype(BF16), cmp_w2.astype(BF16))

    o = _nsa_attention(y3, cmp_kv, gate_logits.reshape(B, L, -1), slopes, n_cmp, n_slc)
    return o.reshape(B * L, B_HEADS * HEAD_DIM)


def _gated_merge_kernel(x_ref, *refs, nk):
    nbr = N_BRANCH
    wg, o, w, bias = refs[:nbr], refs[nbr:2 * nbr], refs[2 * nbr:3 * nbr], refs[3 * nbr:4 * nbr]
    out_ref, acc = refs[4 * nbr], refs[4 * nbr + 1:]

    def logits(c):
        return jnp.dot(x_ref[...], wg[c][...], preferred_element_type=F32)

    def first():
        for c in range(nbr):
            acc[0][c] = logits(c)

    def middle():
        for c in range(nbr):
            acc[0][c] += logits(c)

    def last(has_acc):
        merged = None
        for c in range(nbr):
            z = (acc[0][c] + logits(c)) if has_acc else logits(c)
            term = _sigmoid(z + bias[c][...]) * jnp.dot(o[c][...], w[c][...], preferred_element_type=F32)
            merged = term if merged is None else merged + term
        out_ref[...] = merged.astype(out_ref.dtype)

    _k_steps(nk, first, middle, last)


def _gated_merge(xb, w_gate, b_gate, branch_outs, branch_ws):
    M, K = xb.shape
    D = branch_ws[0].shape[1]
    tm, tn, tk = _tile(M, 1024), _tile(D, 256), _tile(K, 4096)
    nj, nk = D // tn, K // tk
    wg_specs = [pl.BlockSpec((tk, tn), lambda i, j, k, c=c: (k, c * nj + j)) for c in range(N_BRANCH)]
    o_specs = [pl.BlockSpec((tm, o.shape[1]), lambda i, j, k: (i, 0), pipeline_mode=pl.Buffered(1))
               for o in branch_outs]
    w_specs = [pl.BlockSpec((w.shape[0], tn), lambda i, j, k: (0, j)) for w in branch_ws]
    b_specs = [pl.BlockSpec((1, tn), lambda i, j, k, c=c: (0, c * nj + j)) for c in range(N_BRANCH)]
    return pl.pallas_call(
        functools.partial(_gated_merge_kernel, nk=nk),
        grid=(M // tm, nj, nk),
        in_specs=[pl.BlockSpec((tm, tk), lambda i, j, k: (i, k))] + wg_specs + o_specs + w_specs + b_specs,
        out_specs=pl.BlockSpec((tm, tn), lambda i, j, k: (i, j)),
        out_shape=jax.ShapeDtypeStruct((M, D), BF16),
        scratch_shapes=[pltpu.VMEM((N_BRANCH, tm, tn), F32)] if nk > 1 else [],
        compiler_params=_cparams(("parallel", "parallel", "arbitrary")),
        name="gated_merge",
    )(xb, *([w_gate] * N_BRANCH), *branch_outs, *branch_ws, *([b_gate.reshape(1, -1)] * N_BRANCH))


def _attn_weight(w_in):
    D = w_in.shape[0]
    sizes = (A_HEADS * HEAD_DIM,) * 3 + (B_HEADS * HEAD_DIM, 3 * 2 * B_KV_GROUPS * HEAD_DIM, 3 * B_HEADS) + \
            (C_HEADS * HEAD_DIM,) * 3 + (C_HEADS,) + (D_HEADS * HEAD_DIM, HEAD_DIM, HEAD_DIM) + \
            (D_IDX_HEADS * D_IDX_DIM, D_IDX_DIM, D_IDX_HEADS)
    offs = np.concatenate([[0], np.cumsum(sizes)])
    (a_q, a_k, a_v, b_q, b_kv, b_g, c_q, c_k, c_v, c_f, d_q, d_k, d_v, d_iq, d_ik, d_iw) = [
        w_in[:, offs[n]:offs[n + 1]] for n in range(len(sizes))]
    misc = jnp.zeros((D, 2 * LANES), w_in.dtype)
    misc = misc.at[:, MISC_BG:MISC_BG + 3 * B_HEADS].set(b_g)
    misc = misc.at[:, MISC_CF:MISC_CF + C_HEADS].set(c_f)
    misc = misc.at[:, MISC_IK:MISC_IK + D_IDX_DIM].set(d_ik)
    misc = misc.at[:, MISC_IW:MISC_IW + D_IDX_HEADS].set(d_iw)
    w_a = jnp.concatenate([p.astype(BF16) for p in (a_q, a_k, a_v)], axis=1)
    parts = [b_q, c_q, c_k, c_v, d_q, b_kv, d_iq, d_k, d_v, misc]
    w = jnp.concatenate([p.astype(BF16) for p in parts], axis=1)
    assert w_a.shape[1] == N_CB_A * LANES and w.shape[1] == N_CB * LANES
    return w_a, w, int(offs[-1])


def _hybrid_mixer(xb, B, L, w_in, b_forget, b_gate, cmp_w1, cmp_w2, cmp_pos, w_branch, w_out):
    M, D = xb.shape
    slopes_all = jnp.exp2(-8.0 * jnp.arange(1, N_ALIBI + 1, dtype=F32) / N_ALIBI)
    w_a, w_attn, gate_off = _attn_weight(w_in)
    ya = _matmul(xb, w_a, F32, (1024, 768, 4096)).reshape(B, L, N_CB_A * LANES)
    y3 = _matmul(xb, w_attn, BF16, (1024, 1280, 4096)).reshape(B, L, N_CB * LANES)
    misc = _matmul(xb, w_attn[:, CB_MISC * LANES:(CB_MISC + 1) * LANES], F32)

    bias = jnp.zeros((1, LANES), F32).at[0, MISC_CF:MISC_CF + C_HEADS].set(b_forget)
    c = _logsig_cumsum(misc.reshape(B, L, LANES), bias)[:, :, MISC_CF:MISC_CF + C_HEADS].transpose(0, 2, 1)

    o_a = _dilated_attention(ya, slopes_all)
    o_b = _native_sparse_attention(y3, misc[:, MISC_BG:MISC_BG + 3 * B_HEADS], cmp_w1, cmp_w2, cmp_pos, slopes_all)
    o_c = _forgetting_attention(y3, c[..., None], c[:, :, None, :])
    o_d = _indexed_sparse_attention(y3, misc[:, MISC_IK:MISC_IK + D_IDX_DIM].reshape(B, L, D_IDX_DIM),
                                    misc[:, MISC_IW:MISC_IW + D_IDX_HEADS].reshape(B, L, D_IDX_HEADS), slopes_all)

    sizes = (A_HEADS_PER_PAIR * HEAD_DIM, B_HEADS * HEAD_DIM, C_HEADS * HEAD_DIM, D_HEADS * HEAD_DIM)
    offs = np.concatenate([[0], np.cumsum(sizes)])
    ws = [w_branch[offs[n]:offs[n + 1]] for n in range(N_BRANCH)]
    outs = [o_a, o_b, o_c.reshape(M, -1), o_d.reshape(M, -1)]
    merged = _gated_merge(xb, w_in[:, gate_off:].astype(BF16), b_gate, outs, ws)
    return _matmul(merged, w_out, F32)


def kernel(x, ln_g, ln_b, ffn1_w_gate, ffn1_w_up, ffn1_w_down, w_in, b_forget, b_gate, cmp_w1, cmp_w2, cmp_pos,
           w_branch, w_out, ffn2_w_gate, ffn2_w_up, ffn2_w_down):
    B, L, D = x.shape
    assert L % BLOCK == 0 and D % LANES == 0
    xf = x.reshape(B * L, D)
    xb = xf.astype(BF16)

    def ffn(xf, xb, wg, wu, wd, l, g, b):
        h = _ffn_up(xb, _cast_bf16(wg, l), _cast_bf16(wu, l))
        y = _matmul(h, _cast_bf16(wd, l), F32)
        return _add_ln(xf, y, g, b, 0.5)

    for l in range(ln_g.shape[0]):
        xf, xb = ffn(xf, xb, ffn1_w_gate, ffn1_w_up, ffn1_w_down, l, ln_g[l, 0], ln_b[l, 0])
        y = _hybrid_mixer(xb, B, L, w_in[l], b_forget[l], b_gate[l], cmp_w1[l], cmp_w2[l], cmp_pos[l],
                          _cast_bf16(w_branch, l), _cast_bf16(w_out, l))
        xf, xb = _add_ln(xf, y, ln_g[l, 1], ln_b[l, 1], 1.0)
        xf, xb = ffn(xf, xb, ffn2_w_gate, ffn2_w_up, ffn2_w_down, l, ln_g[l, 2], ln_b[l, 2])
    return xf.reshape(B, L, D)
```
